```python
import jax
import jax.numpy as jnp
from jax import lax
import numpy as np

D_MODEL = 1024
BATCH = 4
SEQ = 8192
DEPTH = 4

HEAD_DIM = 64
N_HEADS = 4
BRANCH_W = N_HEADS * HEAD_DIM
N_BRANCH = 4
Q_BLOCK = 128
CMP_BLOCK = 32
SEL_BLOCK = 64
SEL_TOPK = 16
NSA_WINDOW = 512
SWA_WINDOW = 128
SWA_KV_HEADS = 2
RET_CHUNK = 128
RMS_EPS = 1e-6
LN_EPS = 1e-5
NEG_INF = -1e30
TINY = 1e-30
FORCED_SCORE = 1e4

IN_WIDTHS = (
    BRANCH_W, 6 * HEAD_DIM, 3 * N_HEADS, BRANCH_W,
    BRANCH_W, SWA_KV_HEADS * HEAD_DIM, SWA_KV_HEADS * HEAD_DIM, BRANCH_W,
    BRANCH_W, BRANCH_W, BRANCH_W, BRANCH_W,
    BRANCH_W, BRANCH_W, BRANCH_W, BRANCH_W,
)
IN_W = sum(IN_WIDTHS)
IN_OFFSETS = tuple(int(v) for v in np.cumsum(IN_WIDTHS)[:-1])

kernel_name = "hybrid_nsa_swa_stickbreak_retention"


def alibi_slopes(n):
    return jnp.asarray(2.0 ** (-8.0 * (np.arange(n) + 1) / n), dtype=jnp.float32)


def rms_norm(x, g):
    xf = x.astype(jnp.float32)
    y = xf * lax.rsqrt(jnp.mean(xf * xf, axis=-1, keepdims=True) + RMS_EPS)
    return (y * g.astype(jnp.float32)).astype(x.dtype)


def masked_softmax(s, mask, sink=None):
    s = jnp.where(mask, s, NEG_INF)
    m = jnp.max(s, axis=-1, keepdims=True)
    if sink is not None:
        m = jnp.maximum(m, sink)
    e = jnp.where(mask, jnp.exp(s - m), 0.0)
    den = jnp.sum(e, axis=-1, keepdims=True)
    if sink is not None:
        den = den + jnp.exp(sink - m)
    return e / jnp.maximum(den, TINY)


def band_windows(k, n_prev):
    B, S = k.shape[:2]
    nblk = S // Q_BLOCK
    pad = [(0, 0), (n_prev * Q_BLOCK, 0)] + [(0, 0)] * (k.ndim - 2)
    kp = jnp.pad(k, pad).reshape(B, nblk + n_prev, Q_BLOCK, *k.shape[2:])
    return jnp.concatenate([kp[:, j:j + nblk] for j in range(n_prev + 1)], axis=2)


def banded_attention(q, k, v, window, slopes, sink=None):
    B, S, H, Dh = q.shape
    Hkv = k.shape[2]
    G = H // Hkv
    nblk = S // Q_BLOCK
    n_prev = -(-(window - 1) // Q_BLOCK)
    K = (n_prev + 1) * Q_BLOCK
    kw = band_windows(k, n_prev)
    vw = band_windows(v, n_prev)
    qb = q.reshape(B, nblk, Q_BLOCK, Hkv, G, Dh)
    s = jnp.einsum("bnqhgd,bnkhd->bnhgqk", qb, kw).astype(jnp.float32) * Dh ** -0.5
    rel = jnp.arange(Q_BLOCK)[:, None] + n_prev * Q_BLOCK - jnp.arange(K)[None, :]
    kpos = jnp.arange(nblk)[:, None] * Q_BLOCK - n_prev * Q_BLOCK + jnp.arange(K)[None, :]
    mask = ((rel >= 0) & (rel < window))[None] & (kpos >= 0)[:, None, :]
    s = s - slopes.reshape(Hkv, G, 1, 1) * rel.astype(jnp.float32)
    sk = None if sink is None else sink.astype(jnp.float32).reshape(1, 1, Hkv, G, 1, 1)
    p = masked_softmax(s, mask[None, :, None, None], sk)
    o = jnp.einsum("bnhgqk,bnkhd->bnqhgd", p.astype(v.dtype), vw)
    return o.reshape(B, S, H, Dh)


def nsa_attention(q, kv, gate_logits, pos_emb, w1, w2):
    B, S, H, Dh = q.shape
    k_cmp, v_cmp, k_sel, v_sel, k_win, v_win = jnp.split(kv, 6, axis=-1)
    slopes = alibi_slopes(H)
    scale = Dh ** -0.5
    t = jnp.arange(S)

    nc = S // CMP_BLOCK
    def compress(z, j):
        zb = z.reshape(B, nc, CMP_BLOCK, Dh) + pos_emb[j]
        return jax.nn.silu(zb.reshape(B, nc, CMP_BLOCK * Dh) @ w1[j]) @ w2[j]
    kc = compress(k_cmp, 0)
    vc = compress(v_cmp, 1)
    blk_end = jnp.arange(nc) * CMP_BLOCK + CMP_BLOCK - 1
    dist = t[:, None] - blk_end[None, :]
    s = jnp.einsum("bshd,bcd->bhsc", q, kc).astype(jnp.float32) * scale
    s = s - slopes[:, None, None] * dist.astype(jnp.float32)
    p_cmp = masked_softmax(s, dist >= 0)
    o_cmp = jnp.einsum("bhsc,bcd->bshd", p_cmp.astype(vc.dtype), vc)

    ns = S // SEL_BLOCK
    ratio = SEL_BLOCK // CMP_BLOCK
    imp = p_cmp.sum(axis=1).reshape(B, S, ns, ratio).sum(axis=-1)
    blk = jnp.arange(ns)[None, :]
    cur = (t // SEL_BLOCK)[:, None]
    future = blk * SEL_BLOCK > t[:, None]
    forced = (blk == 0) | (blk == cur) | (blk == cur - 1)
    imp = jnp.where(forced, FORCED_SCORE, jnp.where(future, -1.0, imp))
    topk = min(SEL_TOPK, ns)
    _, idx = lax.top_k(imp, topk)

    kb = k_sel.reshape(B, ns, SEL_BLOCK, Dh)
    vb = v_sel.reshape(B, ns, SEL_BLOCK, Dh)
    nq = S // Q_BLOCK
    qs = q.reshape(B, nq, Q_BLOCK, H, Dh).swapaxes(0, 1)
    ids = idx.reshape(B, nq, Q_BLOCK, topk).swapaxes(0, 1)
    qpos = t.reshape(nq, Q_BLOCK)
    bidx = jnp.arange(B)[:, None, None]
    n_keys = topk * SEL_BLOCK

    def sel_block(args):
        qi, ii, pi = args
        kg = kb[bidx, ii].reshape(B, Q_BLOCK, n_keys, Dh)
        vg = vb[bidx, ii].reshape(B, Q_BLOCK, n_keys, Dh)
        kpos = (ii[..., None] * SEL_BLOCK + jnp.arange(SEL_BLOCK)).reshape(B, Q_BLOCK, n_keys)
        d = pi[None, :, None] - kpos
        sc = jnp.einsum("bqhd,bqkd->bhqk", qi, kg).astype(jnp.float32) * scale
        sc = sc - slopes[:, None, None] * d[:, None].astype(jnp.float32)
        pr = masked_softmax(sc, (d >= 0)[:, None])
        return jnp.einsum("bhqk,bqkd->bqhd", pr.astype(vg.dtype), vg)

    o_sel = lax.map(sel_block, (qs, ids, qpos)).swapaxes(0, 1).reshape(B, S, H, Dh)

    o_win = banded_attention(q, k_win[:, :, None], v_win[:, :, None], NSA_WINDOW, slopes)

    g = jax.nn.sigmoid(gate_logits.astype(jnp.float32)).reshape(B, S, H, 3).astype(q.dtype)
    return g[..., 0:1] * o_cmp + g[..., 1:2] * o_sel + g[..., 2:3] * o_win


def stick_breaking_attention(q, k, v):
    B, S, H, Dh = q.shape
    nq = S // Q_BLOCK
    qs = q.reshape(B, nq, Q_BLOCK, H, Dh).swapaxes(0, 1)
    qpos = jnp.arange(S).reshape(nq, Q_BLOCK)
    kpos = jnp.arange(S)
    scale = Dh ** -0.5

    def one_block(args):
        qi, pi = args
        z = jnp.einsum("bqhd,bkhd->bhqk", qi, k).astype(jnp.float32) * scale
        mask = kpos[None, :] < pi[:, None]
        log_beta = jax.nn.log_sigmoid(z)
        log_1m = jnp.where(mask, jax.nn.log_sigmoid(-z), 0.0)
        suffix = lax.cumsum(log_1m, axis=3, reverse=True) - log_1m
        a = jnp.where(mask, jnp.exp(log_beta + suffix), 0.0)
        return jnp.einsum("bhqk,bkhd->bqhd", a.astype(v.dtype), v)

    return lax.map(one_block, (qs, qpos)).swapaxes(0, 1).reshape(B, S, H, Dh)


def retention(q, k, v):
    B, S, H, Dh = q.shape
    C = RET_CHUNK
    n = S // C
    log_g = jnp.log(1.0 - jnp.asarray(2.0 ** (-5.0 - np.arange(H)), dtype=jnp.float32))
    qc = (q.astype(jnp.float32) * Dh ** -0.5).reshape(B, n, C, H, Dh)
    kc = k.astype(jnp.float32).reshape(B, n, C, H, Dh)
    vc = v.astype(jnp.float32).reshape(B, n, C, H, Dh)
    i = jnp.arange(C)
    diff = (i[:, None] - i[None, :]).astype(jnp.float32)
    dmat = jnp.where(diff >= 0, jnp.exp(log_g[:, None, None] * jnp.maximum(diff, 0.0)), 0.0)
    s = jnp.einsum("bnihd,bnjhd->bnhij", qc, kc) * dmat
    intra = jnp.einsum("bnhij,bnjhd->bnihd", s, vc)
    zeta = jnp.exp(log_g[:, None] * (C - 1 - i)[None, :].astype(jnp.float32))
    u = jnp.einsum("bnjhd,bnjhe,hj->nbhde", kc, vc, zeta)
    g_chunk = jnp.exp(log_g * C)[None, :, None, None]

    def step(r, u_n):
        return r * g_chunk + u_n, r

    _, r_prev = lax.scan(step, jnp.zeros((B, H, Dh, Dh), jnp.float32), u)
    xi = jnp.exp(log_g[:, None] * (i + 1)[None, :].astype(jnp.float32))
    cross = jnp.einsum("bnihd,nbhde,hi->bnihe", qc, r_prev, xi)
    o = (intra + cross).reshape(B, S, H, Dh)
    mu = jnp.mean(o, axis=-1, keepdims=True)
    var = jnp.mean((o - mu) ** 2, axis=-1, keepdims=True)
    return ((o - mu) * lax.rsqrt(var + LN_EPS)).astype(q.dtype)


def hybrid_layer(x, c, w_ada, b_ada, norm_g, w_in, cmp_pos, cmp_w1, cmp_w2, sink, w_merge, w_br, w_out):
    B, S, _ = x.shape
    mod = (jax.nn.silu(c) @ w_ada + b_ada)[:, None, :]
    shift, scale, gate = jnp.split(mod, 3, axis=-1)
    h = rms_norm(x, norm_g) * (1.0 + scale) + shift
    (a_q, a_kv, a_g, a_z, b_q, b_k, b_v, b_z,
     c_q, c_k, c_v, c_z, d_q, d_k, d_v, d_z) = jnp.split(h @ w_in, IN_OFFSETS, axis=-1)

    def heads(z):
        return z.reshape(B, S, -1, HEAD_DIM)

    y_a = nsa_attention(heads(a_q), a_kv, a_g, cmp_pos, cmp_w1, cmp_w2)
    y_b = banded_attention(heads(b_q), heads(b_k), heads(b_v), SWA_WINDOW, alibi_slopes(N_HEADS), sink)
    y_c = stick_breaking_attention(heads(c_q), heads(c_k), heads(c_v))
    y_d = retention(heads(d_q), heads(d_k), heads(d_v))

    branches = ((y_a, a_z), (y_b, b_z), (y_c, c_z), (y_d, d_z))
    merged = None
    for i, (y, z) in enumerate(branches):
        yi = y.reshape(B, S, BRANCH_W) * jax.nn.silu(z)
        term = jax.nn.sigmoid(h @ w_merge[i]) * (yi @ w_br[i])
        merged = term if merged is None else merged + term
    return x + gate * (merged @ w_out)


def setup_inputs(seed: int = 0) -> dict:
    key = jax.random.key(seed)
    ks = jax.random.split(key, 14)
    D = D_MODEL
    f32 = jnp.float32

    def nrm(k, shape, fan):
        return jax.random.normal(k, shape, f32) * fan ** -0.5

    return {
        "x": jax.random.normal(ks[0], (BATCH, SEQ, D), f32),
        "c": jax.random.normal(ks[1], (BATCH, D), f32),
        "w_ada": nrm(ks[2], (DEPTH, D, 3 * D), D) * 0.5,
        "b_ada": 0.02 * jax.random.normal(ks[3], (DEPTH, 3 * D), f32),
        "norm_g": 1.0 + 0.05 * jax.random.normal(ks[4], (DEPTH, D), f32),
        "w_in": nrm(ks[5], (DEPTH, D, IN_W), D),
        "cmp_pos": 0.1 * jax.random.normal(ks[6], (DEPTH, 2, CMP_BLOCK, HEAD_DIM), f32),
        "cmp_w1": nrm(ks[7], (DEPTH, 2, CMP_BLOCK * HEAD_DIM, HEAD_DIM), CMP_BLOCK * HEAD_DIM),
        "cmp_w2": nrm(ks[8], (DEPTH, 2, HEAD_DIM, HEAD_DIM), HEAD_DIM),
        "sink": 0.5 * jax.random.normal(ks[9], (DEPTH, N_HEADS), f32),
        "w_merge": nrm(ks[10], (DEPTH, N_BRANCH, D, D), D),
        "w_br": nrm(ks[11], (DEPTH, N_BRANCH, BRANCH_W, D), BRANCH_W),
        "w_out": nrm(ks[12], (DEPTH, D, D), D),
        "final_g": 1.0 + 0.05 * jax.random.normal(ks[13], (D,), f32),
    }


def reference(x, c, w_ada, b_ada, norm_g, w_in, cmp_pos, cmp_w1, cmp_w2, sink, w_merge, w_br, w_out, final_g):
    for l in range(DEPTH):
        x = hybrid_layer(x, c, w_ada[l], b_ada[l], norm_g[l], w_in[l], cmp_pos[l], cmp_w1[l], cmp_w2[l],
                         sink[l], w_merge[l], w_br[l], w_out[l])
    return rms_norm(x, final_g)
```

```python
import functools

import numpy as np
import jax
import jax.numpy as jnp
from jax import lax
from jax.experimental import pallas as pl
from jax.experimental.pallas import tpu as pltpu

F32 = jnp.float32
BF16 = jnp.bfloat16

HEAD_DIM = 64
N_HEADS = 4
BRANCH_W = N_HEADS * HEAD_DIM
CMP_BLOCK = 32
SEL_BLOCK = 64
SEL_TOPK = 16
NSA_WINDOW = 512
SWA_WINDOW = 128
RET_CHUNK = 128
RMS_EPS = 1e-6
LN_EPS = 1e-5
NEG_INF = -1e30
TINY = 1e-30
FORCED_SCORE = 1e4
QK_SCALE = HEAD_DIM ** -0.5
STICK_EXIT = -110.0

VMEM_LIMIT = 56 * 1024 * 1024
SLOPES = tuple(float(2.0 ** (-8.0 * (h + 1) / N_HEADS)) for h in range(N_HEADS))

PROJ_OUTS = (
    ("aq", 0, 256, BF16),
    ("acmp", 256, 128, F32),
    ("akv", 384, 256, BF16),
    ("ag", 640, 128, F32),
    ("z", 768, 1024, F32),
    ("bq", 1792, 256, BF16),
    ("bkv", 2048, 256, BF16),
    ("cq", 2304, 256, BF16),
    ("ck", 2560, 256, BF16),
    ("cv", 2816, 256, BF16),
    ("dq", 3072, 256, BF16),
    ("dk", 3328, 256, BF16),
    ("dv", 3584, 256, BF16),
)
PROJ_W = 3840
SWA_HEAD_ORDER = (0, 2, 1, 3)


def _dot(a, b):
    return jnp.dot(a, b, preferred_element_type=F32)


def _dot_t(a, b):
    return lax.dot_general(a, b, (((1,), (1,)), ((), ())), preferred_element_type=F32)


def _dot_split(a, b):
    hi = a.astype(BF16)
    lo = (a - hi.astype(F32)).astype(BF16)
    return _dot(hi, b) + _dot(lo, b)


def _sigmoid(v):
    return 1.0 / (1.0 + jnp.exp(-v))


def _silu(v):
    return v * _sigmoid(v)


def _params(*sem):
    return pltpu.CompilerParams(dimension_semantics=sem, vmem_limit_bytes=VMEM_LIMIT)


def _mod_kernel(c_ref, w_ref, b_ref, o_ref):
    cc = c_ref[...]
    o_ref[0] = _dot(_silu(cc).astype(BF16), w_ref[0].astype(BF16)) + b_ref[0]


def _modulation(c, w_ada, b_ada):
    depth, d, n = w_ada.shape
    b = c.shape[0]
    rows = 8
    cp = jnp.zeros((rows, d), F32).at[:b].set(c)
    tn = 512
    out = pl.pallas_call(
        _mod_kernel,
        grid=(depth, n // tn),
        in_specs=[
            pl.BlockSpec((rows, d), lambda l, j: (0, 0)),
            pl.BlockSpec((1, d, tn), lambda l, j: (l, 0, j)),
            pl.BlockSpec((1, 1, tn), lambda l, j: (l, 0, j)),
        ],
        out_specs=pl.BlockSpec((1, rows, tn), lambda l, j: (l, 0, j)),
        out_shape=jax.ShapeDtypeStruct((depth, rows, n), F32),
        compiler_params=_params("arbitrary", "arbitrary"),
        name="adaln_mod",
    )(cp, w_ada, b_ada.reshape(depth, 1, n))
    return out[:, :b].reshape(depth, b, 1, n)


def _modulated_norm(xf, g, mod, d):
    ms = jnp.mean(xf * xf, axis=-1, keepdims=True)
    y = xf * lax.rsqrt(ms + RMS_EPS) * g
    return y * (1.0 + mod[:, d:2 * d]) + mod[:, 0:d]


def _proj_kernel(x_ref, mod_ref, g_ref, w_ref, *out_refs, d):
    h = _modulated_norm(x_ref[0], g_ref[...], mod_ref[0], d).astype(BF16)
    for (_, start, width, dtype), o_ref in zip(PROJ_OUTS, out_refs):
        for c0 in range(0, width, 256):
            cw = min(256, width - c0)
            o_ref[0, :, c0:c0 + cw] = _dot(h, w_ref[:, start + c0:start + c0 + cw]).astype(dtype)


def _project(x, mod, g, w_p, tm):
    b, s, d = x.shape
    out_shapes = [jax.ShapeDtypeStruct((b, s, wd), dt) for (_, _, wd, dt) in PROJ_OUTS]
    out_specs = [pl.BlockSpec((1, tm, wd), lambda i, j: (i, j, 0)) for (_, _, wd, _) in PROJ_OUTS]
    outs = pl.pallas_call(
        functools.partial(_proj_kernel, d=d),
        grid=(b, s // tm),
        in_specs=[
            pl.BlockSpec((1, tm, d), lambda i, j: (i, j, 0)),
            pl.BlockSpec((1, 1, 3 * d), lambda i, j: (i, 0, 0)),
            pl.BlockSpec((1, d), lambda i, j: (0, 0)),
            pl.BlockSpec((d, PROJ_W), lambda i, j: (0, 0)),
        ],
        out_specs=out_specs,
        out_shape=out_shapes,
        compiler_params=_params("arbitrary", "arbitrary"),
        name="in_proj",
    )(x, mod, g.reshape(1, d), w_p)
    return {name: o for (name, _, _, _), o in zip(PROJ_OUTS, outs)}


def _compress_kernel(zk_ref, zv_ref, pos_ref, w1_ref, w2_ref, kc_ref, vc_ref):
    for j, (z_ref, o_ref) in enumerate(((zk_ref, kc_ref), (zv_ref, vc_ref))):
        zb = (z_ref[0] + pos_ref[j:j + 1, :]).astype(BF16)
        hid = _silu(_dot(zb, w1_ref[j])).astype(BF16)
        o_ref[0] = _dot(hid, w2_ref[j]).astype(BF16)


def _compress(zk, zv, pos, w1, w2p):
    b, nc, kd = zk.shape
    spec_z = pl.BlockSpec((1, nc, kd), lambda i: (i, 0, 0))
    spec_o = pl.BlockSpec((1, nc, 128), lambda i: (i, 0, 0))
    return pl.pallas_call(
        _compress_kernel,
        grid=(b,),
        in_specs=[
            spec_z, spec_z,
            pl.BlockSpec((2, kd), lambda i: (0, 0)),
            pl.BlockSpec((2, kd, HEAD_DIM), lambda i: (0, 0, 0)),
            pl.BlockSpec((2, HEAD_DIM, 128), lambda i: (0, 0, 0)),
        ],
        out_specs=[spec_o, spec_o],
        out_shape=[jax.ShapeDtypeStruct((b, nc, 128), BF16)] * 2,
        compiler_params=_params("arbitrary"),
        name="nsa_compress",
    )(zk, zv, pos, w1, w2p)


def _flash_update(s, vt, m_ref, l_ref, acc_ref):
    m_old = m_ref[...]
    m_new = jnp.maximum(m_old, jnp.max(s, axis=-1, keepdims=True))
    alpha = jnp.exp(m_old - m_new)
    e = jnp.exp(s - m_new)
    l_ref[...] = alpha * l_ref[...] + jnp.sum(e, axis=-1, keepdims=True)
    acc_ref[...] = alpha * acc_ref[...] + _dot(e.astype(BF16), vt)
    m_ref[...] = m_new


def _flash_reset(m_ref, l_ref, acc_ref):
    m_ref[...] = jnp.full(m_ref.shape, NEG_INF, F32)
    l_ref[...] = jnp.zeros(l_ref.shape, F32)
    acc_ref[...] = jnp.zeros(acc_ref.shape, F32)


def _nsa_kernel(aq_ref, ag_ref, kc_ref, vc_ref, ksel_ref, vsel_ref, kwin_ref, vwin_ref, ya_ref,
                qaug_ref, m_ref, l_ref, acc_ref, osel_ref, *, tq, seq):
    qi = pl.program_id(1)
    qstart = qi * tq
    nc = seq // CMP_BLOCK
    half = nc // 2
    ns = seq // SEL_BLOCK
    t_col = lax.broadcasted_iota(jnp.int32, (tq, 1), 0) + qstart

    lane_c = lax.broadcasted_iota(jnp.int32, (1, nc), 1)
    blk_c = jnp.where(lane_c < half, 2 * lane_c, 2 * (lane_c - half) + 1)
    dist = t_col - (blk_c * CMP_BLOCK + (CMP_BLOCK - 1))
    dist_f = dist.astype(F32)
    mask_c = dist >= 0
    lane128 = lax.broadcasted_iota(jnp.int32, (1, 128), 1)
    kc = kc_ref[0]
    vc = vc_ref[0]
    psum = jnp.zeros((tq, nc), F32)
    ocmp = []
    for h in range(N_HEADS):
        grp = aq_ref[0, :, 128 * (h // 2):128 * (h // 2) + 128].astype(F32)
        if h % 2 == 1:
            grp = pltpu.roll(grp, 64, 1)
        left = jnp.where(lane128 < HEAD_DIM, grp * QK_SCALE,
                         jnp.where(lane128 < HEAD_DIM + 2, SLOPES[h], 0.0)).astype(BF16)
        qaug_ref[h * tq:(h + 1) * tq, 0:128] = left
        s = _dot_t(left, kc) - SLOPES[h] * dist_f
        s = jnp.where(mask_c, s, NEG_INF)
        m = jnp.max(s, axis=-1, keepdims=True)
        e = jnp.where(mask_c, jnp.exp(s - m), 0.0)
        den = jnp.sum(e, axis=-1, keepdims=True)
        p = e / jnp.maximum(den, TINY)
        psum = psum + p
        ocmp.append(_dot(p.astype(BF16), vc))

    imp = psum[:, :half] + psum[:, half:]
    blk = lax.broadcasted_iota(jnp.int32, (1, ns), 1)
    cur = t_col >> 6
    future = blk * SEL_BLOCK > t_col
    forced = (blk == 0) | (blk == cur) | (blk == cur - 1)
    score = jnp.where(forced, FORCED_SCORE, jnp.where(future, -1.0, imp))
    blk_f = blk.astype(F32)
    sel = jnp.zeros((tq, ns), jnp.bool_)
    for _ in range(min(SEL_TOPK, ns)):
        mx = jnp.max(score, axis=-1, keepdims=True)
        first = jnp.min(jnp.where(score == mx, blk_f, float(ns)), axis=-1, keepdims=True)
        pick = blk_f == first
        sel = sel | pick
        score = jnp.where(pick, -jnp.inf, score)
    negmask = jnp.where(sel, 0.0, NEG_INF).astype(BF16)
    if ns < 128:
        negmask = jnp.concatenate([negmask, jnp.zeros((tq, 128 - ns), BF16)], axis=1)
    for h in range(N_HEADS):
        qaug_ref[h * tq:(h + 1) * tq, 128:256] = negmask

    row_in_tile = lax.broadcasted_iota(jnp.int32, (N_HEADS * tq, 1), 0) & (tq - 1)
    col = lax.broadcasted_iota(jnp.int32, (1, tq), 1)
    causal = col <= row_in_tile

    _flash_reset(m_ref, l_ref, acc_ref)
    qa = qaug_ref[...]
    s = _dot_t(qa, ksel_ref[0, pl.ds(qstart, tq), :])
    _flash_update(jnp.where(causal, s, NEG_INF), vsel_ref[0, pl.ds(qstart, tq), :], m_ref, l_ref, acc_ref)

    def sel_body(j, carry):
        k0 = pl.multiple_of(j * tq, tq)
        sj = _dot_t(qaug_ref[...], ksel_ref[0, pl.ds(k0, tq), :])
        _flash_update(sj, vsel_ref[0, pl.ds(k0, tq), :], m_ref, l_ref, acc_ref)
        return carry

    lax.fori_loop(0, qi, sel_body, 0)
    osel_ref[...] = acc_ref[...] / l_ref[...]

    _flash_reset(m_ref, l_ref, acc_ref)
    qw = qaug_ref[:, 0:128]
    s = _dot_t(qw, kwin_ref[0, pl.ds(qstart, tq), :])
    _flash_update(jnp.where(causal, s, NEG_INF), vwin_ref[0, pl.ds(qstart, tq), :], m_ref, l_ref, acc_ref)
    n_prev = -(-(NSA_WINDOW - 1) // tq)

    def win_body(j, carry):
        k0 = pl.multiple_of(j * tq, tq)
        sj = _dot_t(qaug_ref[:, 0:128], kwin_ref[0, pl.ds(k0, tq), :])
        inside = (row_in_tile + qstart) - (col + k0) < NSA_WINDOW
        _flash_update(jnp.where(inside, sj, NEG_INF), vwin_ref[0, pl.ds(k0, tq), :], m_ref, l_ref, acc_ref)
        return carry

    lax.fori_loop(jnp.maximum(qi - n_prev, 0), qi, win_body, 0)
    owin = acc_ref[...] / l_ref[...]

    gate = _sigmoid(ag_ref[0])
    comb = []
    for h in range(N_HEADS):
        rows = slice(h * tq, (h + 1) * tq)
        comb.append(gate[:, 3 * h:3 * h + 1] * ocmp[h]
                    + gate[:, 3 * h + 1:3 * h + 2] * osel_ref[rows, :]
                    + gate[:, 3 * h + 2:3 * h + 3] * owin[rows, :])
    for g in range(2):
        ya_ref[0, :, 128 * g:128 * (g + 1)] = jnp.where(
            lane128 < HEAD_DIM, pltpu.roll(comb[2 * g], 64, 1), comb[2 * g + 1])


def _nsa(aq, ag, kc, vc, ksel, akv, kwin, tq):
    b, s, _ = aq.shape
    nc = kc.shape[1]
    rows = N_HEADS * tq
    whole = lambda w, blk: pl.BlockSpec((1, s, w), lambda i, j, blk=blk: (i, 0, blk))
    return pl.pallas_call(
        functools.partial(_nsa_kernel, tq=tq, seq=s),
        grid=(b, s // tq),
        in_specs=[
            pl.BlockSpec((1, tq, 256), lambda i, j: (i, j, 0)),
            pl.BlockSpec((1, tq, 128), lambda i, j: (i, j, 0)),
            pl.BlockSpec((1, nc, 128), lambda i, j: (i, 0, 0)),
            pl.BlockSpec((1, nc, 128), lambda i, j: (i, 0, 0)),
            whole(256, 0),
            whole(128, 0),
            whole(128, 0),
            whole(128, 1),
        ],
        out_specs=pl.BlockSpec((1, tq, 256), lambda i, j: (i, j, 0)),
        out_shape=jax.ShapeDtypeStruct((b, s, 256), F32),
        scratch_shapes=[
            pltpu.VMEM((rows, 256), BF16),
            pltpu.VMEM((rows, 1), F32),
            pltpu.VMEM((rows, 1), F32),
            pltpu.VMEM((rows, 128), F32),
            pltpu.VMEM((rows, 128), F32),
        ],
        compiler_params=_params("arbitrary", "arbitrary"),
        name="nsa_attention",
    )(aq, ag, kc, vc, ksel, akv, kwin, akv)


def _swa_kernel(sink_ref, bq_ref, bkv_ref, yb_ref, *, tq, seq):
    qi = pl.program_id(1)
    qstart = qi * tq
    nk = 2 * tq
    k0 = pl.multiple_of(jnp.maximum(qstart - tq, 0), tq)
    kk = bkv_ref[0, pl.ds(k0, nk), 0:128]
    vv = bkv_ref[0, pl.ds(k0, nk), 128:256]
    t_col = lax.broadcasted_iota(jnp.int32, (tq, 1), 0) + qstart
    s_row = lax.broadcasted_iota(jnp.int32, (1, nk), 1) + k0
    rel = t_col - s_row
    rel_f = rel.astype(F32)
    mask = (rel >= 0) & (rel < SWA_WINDOW)
    lane128 = lax.broadcasted_iota(jnp.int32, (1, 128), 1)
    for g in range(2):
        qg = bq_ref[0, :, 128 * g:128 * (g + 1)]
        outs = []
        for p in range(2):
            h = SWA_HEAD_ORDER[2 * g + p]
            in_half = (lane128 >= HEAD_DIM * p) & (lane128 < HEAD_DIM * (p + 1))
            qh = jnp.where(in_half, qg, jnp.zeros_like(qg))
            s = _dot_t(qh, kk) * QK_SCALE - SLOPES[h] * rel_f
            s = jnp.where(mask, s, NEG_INF)
            sink = sink_ref[h]
            m = jnp.maximum(jnp.max(s, axis=-1, keepdims=True), sink)
            e = jnp.where(mask, jnp.exp(s - m), 0.0)
            den = jnp.sum(e, axis=-1, keepdims=True) + jnp.exp(sink - m)
            pr = e / jnp.maximum(den, TINY)
            outs.append(_dot(pr.astype(BF16), vv))
        yb_ref[0, :, 128 * g:128 * (g + 1)] = jnp.where(lane128 < HEAD_DIM, outs[0], outs[1])


def _swa(sink, bq, bkv, tq):
    b, s, _ = bq.shape
    return pl.pallas_call(
        functools.partial(_swa_kernel, tq=tq, seq=s),
        grid=(b, s // tq),
        in_specs=[
            pl.BlockSpec(memory_space=pltpu.SMEM),
            pl.BlockSpec((1, tq, 256), lambda i, j: (i, j, 0)),
            pl.BlockSpec((1, s, 256), lambda i, j: (i, 0, 0)),
        ],
        out_specs=pl.BlockSpec((1, tq, 256), lambda i, j: (i, j, 0)),
        out_shape=jax.ShapeDtypeStruct((b, s, 256), F32),
        compiler_params=_params("arbitrary", "arbitrary"),
        name="swa_attention",
    )(sink, bq, bkv)


def _stick_kernel(cq_ref, ck_ref, cv_ref, yc_ref, qm_ref, carry_ref, o_ref, *, tq, seq):
    qi = pl.program_id(1)
    qstart = qi * tq
    lane256 = lax.broadcasted_iota(jnp.int32, (1, 256), 1)
    head_of_lane = lane256 >> 6
    q = cq_ref[0].astype(F32) * QK_SCALE
    for h in range(N_HEADS):
        qm_ref[h] = jnp.where(head_of_lane == h, q, 0.0).astype(BF16)
    jj = lax.broadcasted_iota(jnp.int32, (tq, tq), 0)
    ss = lax.broadcasted_iota(jnp.int32, (tq, tq), 1)
    upper = (jj > ss).astype(BF16)
    strict = ss < jj

    def tile(k0, diagonal):
        kt = ck_ref[0, pl.ds(k0, tq), :]
        vt = cv_ref[0, pl.ds(k0, tq), :]
        acc = o_ref[...]
        worst = jnp.full((1, 1), -jnp.inf, F32)
        for h in range(N_HEADS):
            z = _dot_t(qm_ref[h], kt)
            soft = jnp.log1p(jnp.exp(-jnp.abs(z)))
            log_beta = jnp.minimum(z, 0.0) - soft
            log_1m = log_beta - z
            if diagonal:
                log_1m = jnp.where(strict, log_1m, 0.0)
            suffix = _dot_split(log_1m, upper)
            if not diagonal:
                suffix = suffix + carry_ref[h]
            a = jnp.exp(log_beta + suffix)
            if diagonal:
                a = jnp.where(strict, a, 0.0)
            pv = _dot(a.astype(BF16), vt)
            acc = acc + jnp.where(head_of_lane == h, pv, 0.0)
            total = jnp.sum(log_1m, axis=-1, keepdims=True)
            new_carry = total if diagonal else carry_ref[h] + total
            carry_ref[h] = new_carry
            worst = jnp.maximum(worst, jnp.max(new_carry, axis=0, keepdims=True))
        o_ref[...] = acc
        return worst[0, 0]

    o_ref[...] = jnp.zeros(o_ref.shape, F32)
    worst0 = tile(pl.multiple_of(qstart, tq), True)

    def cond(state):
        j, worst = state
        return (j >= 0) & (worst > STICK_EXIT)

    def body(state):
        j, _ = state
        return j - 1, tile(pl.multiple_of(j * tq, tq), False)

    lax.while_loop(cond, body, (qi - 1, worst0))
    yc_ref[0] = o_ref[...]


def _stick(cq, ck, cv, tq):
    b, s, _ = cq.shape
    return pl.pallas_call(
        functools.partial(_stick_kernel, tq=tq, seq=s),
        grid=(b, s // tq),
        in_specs=[
            pl.BlockSpec((1, tq, 256), lambda i, j: (i, j, 0)),
            pl.BlockSpec((1, s, 256), lambda i, j: (i, 0, 0)),
            pl.BlockSpec((1, s, 256), lambda i, j: (i, 0, 0)),
        ],
        out_specs=pl.BlockSpec((1, tq, 256), lambda i, j: (i, j, 0)),
        out_shape=jax.ShapeDtypeStruct((b, s, 256), F32),
        scratch_shapes=[
            pltpu.VMEM((N_HEADS, tq, 256), BF16),
            pltpu.VMEM((N_HEADS, tq, 1), F32),
            pltpu.VMEM((tq, 256), F32),
        ],
        compiler_params=_params("arbitrary", "arbitrary"),
        name="stick_breaking",
    )(cq, ck, cv)


def _ret_kernel(dq_ref, dk_ref, dv_ref, dmat_ref, xi_ref, zeta_ref, decay_ref, bd_ref, yd_ref, r_ref):
    n = pl.program_id(1)

    @pl.when(n == 0)
    def _():
        r_ref[...] = jnp.zeros(r_ref.shape, F32)

    c = RET_CHUNK
    lane256 = lax.broadcasted_iota(jnp.int32, (1, 256), 1)
    head_of_lane = lane256 >> 6
    q = dq_ref[0].astype(F32) * QK_SCALE
    k = dk_ref[0]
    v = dv_ref[0]
    o = _dot(q.astype(BF16), r_ref[...].astype(BF16)) * xi_ref[...]
    for h in range(N_HEADS):
        qh = jnp.where(head_of_lane == h, q, 0.0).astype(BF16)
        s = _dot_t(qh, k) * dmat_ref[h]
        o = o + jnp.where(head_of_lane == h, _dot(s.astype(BF16), v), 0.0)
    mu = jnp.zeros((c, 256), F32)
    for h in range(N_HEADS):
        in_h = head_of_lane == h
        mu = mu + jnp.where(in_h, jnp.sum(jnp.where(in_h, o, 0.0), axis=-1, keepdims=True), 0.0)
    cen = o - mu * (1.0 / HEAD_DIM)
    var = jnp.zeros((c, 256), F32)
    for h in range(N_HEADS):
        in_h = head_of_lane == h
        var = var + jnp.where(in_h, jnp.sum(jnp.where(in_h, cen * cen, 0.0), axis=-1, keepdims=True), 0.0)
    yd_ref[0] = cen * lax.rsqrt(var * (1.0 / HEAD_DIM) + LN_EPS)
    kz = (k.astype(F32) * zeta_ref[...]).astype(BF16)
    upd = lax.dot_general(kz, v, (((0,), (0,)), ((), ())), preferred_element_type=F32)
    r_ref[...] = r_ref[...] * decay_ref[...] + upd * bd_ref[...]


def _retention(dq, dk, dv, consts):
    b, s, _ = dq.shape
    c = RET_CHUNK
    dmat, xi, zeta, decay, bd = consts
    tile = pl.BlockSpec((1, c, 256), lambda i, j: (i, j, 0))
    return pl.pallas_call(
        _ret_kernel,
        grid=(b, s // c),
        in_specs=[
            tile, tile, tile,
            pl.BlockSpec((N_HEADS, c, c), lambda i, j: (0, 0, 0)),
            pl.BlockSpec((c, 256), lambda i, j: (0, 0)),
            pl.BlockSpec((c, 256), lambda i, j: (0, 0)),
            pl.BlockSpec((256, 256), lambda i, j: (0, 0)),
            pl.BlockSpec((256, 256), lambda i, j: (0, 0)),
        ],
        out_specs=tile,
        out_shape=jax.ShapeDtypeStruct((b, s, 256), F32),
        scratch_shapes=[pltpu.VMEM((256, 256), F32)],
        compiler_params=_params("arbitrary", "arbitrary"),
        name="retention",
    )(dq, dk, dv, dmat, xi, zeta, decay, bd)


def _retention_consts():
    c = RET_CHUNK
    log_g = jnp.log(1.0 - jnp.asarray(2.0 ** (-5.0 - np.arange(N_HEADS)), dtype=F32))
    i = jnp.arange(c)
    diff = (i[:, None] - i[None, :]).astype(F32)
    dmat = jnp.where(diff >= 0, jnp.exp(log_g[:, None, None] * jnp.maximum(diff, 0.0)), 0.0)
    zeta = jnp.exp(log_g[:, None] * (c - 1 - i)[None, :].astype(F32))
    xi = jnp.exp(log_g[:, None] * (i + 1)[None, :].astype(F32))
    g_chunk = jnp.exp(log_g * c)
    lanes = lambda hc: jnp.repeat(hc.T, HEAD_DIM, axis=1)
    head = np.arange(256) // HEAD_DIM
    bd = jnp.asarray(head[:, None] == head[None, :], F32)
    decay = jnp.repeat(g_chunk, HEAD_DIM)[:, None] * jnp.ones((1, 256), F32)
    return dmat, lanes(xi), lanes(zeta), decay, bd


def _merge_kernel(x_ref, mod_ref, g_ref, ya_ref, yb_ref, yc_ref, yd_ref, z_ref,
                  wm_ref, wbr_ref, wout_ref, o_ref, *, d):
    xf = x_ref[0]
    mod = mod_ref[0]
    h = _modulated_norm(xf, g_ref[...], mod, d).astype(BF16)
    merged = None
    for i, y_ref in enumerate((ya_ref, yb_ref, yc_ref, yd_ref)):
        yi = (y_ref[0] * _silu(z_ref[0, :, 256 * i:256 * (i + 1)])).astype(BF16)
        term = _sigmoid(_dot(h, wm_ref[i])) * _dot(yi, wbr_ref[i])
        merged = term if merged is None else merged + term
    o_ref[0] = xf + mod[:, 2 * d:3 * d] * _dot(merged.astype(BF16), wout_ref[...])


def _merge(x, mod, g, ys, z, wm, wbr, wout, tm):
    b, s, d = x.shape
    ytile = pl.BlockSpec((1, tm, 256), lambda i, j: (i, j, 0))
    return pl.pallas_call(
        functools.partial(_merge_kernel, d=d),
        grid=(b, s // tm),
        in_specs=[
            pl.BlockSpec((1, tm, d), lambda i, j: (i, j, 0)),
            pl.BlockSpec((1, 1, 3 * d), lambda i, j: (i, 0, 0)),
            pl.BlockSpec((1, d), lambda i, j: (0, 0)),
            ytile, ytile, ytile, ytile,
            pl.BlockSpec((1, tm, 4 * 256), lambda i, j: (i, j, 0)),
            pl.BlockSpec((4, d, d), lambda i, j: (0, 0, 0)),
            pl.BlockSpec((4, 256, d), lambda i, j: (0, 0, 0)),
            pl.BlockSpec((d, d), lambda i, j: (0, 0)),
        ],
        out_specs=pl.BlockSpec((1, tm, d), lambda i, j: (i, j, 0)),
        out_shape=jax.ShapeDtypeStruct((b, s, d), F32),
        compiler_params=_params("arbitrary", "arbitrary"),
        name="merge_out",
    )(x, mod, g.reshape(1, d), *ys, z, wm, wbr, wout)


def _final_norm_kernel(x_ref, g_ref, o_ref):
    xf = x_ref[0]
    ms = jnp.mean(xf * xf, axis=-1, keepdims=True)
    o_ref[0] = xf * lax.rsqrt(ms + RMS_EPS) * g_ref[...]


def _final_norm(x, g, tm):
    b, s, d = x.shape
    return pl.pallas_call(
        _final_norm_kernel,
        grid=(b, s // tm),
        in_specs=[pl.BlockSpec((1, tm, d), lambda i, j: (i, j, 0)),
                  pl.BlockSpec((1, d), lambda i, j: (0, 0))],
        out_specs=pl.BlockSpec((1, tm, d), lambda i, j: (i, j, 0)),
        out_shape=jax.ShapeDtypeStruct((b, s, d), F32),
        compiler_params=_params("arbitrary", "arbitrary"),
        name="final_norm",
    )(x, g.reshape(1, d))


def _proj_columns():
    a, bb, cc, dd = 0, 908, 1676, 2700
    perm = np.concatenate([np.arange(64) + 64 * h for h in SWA_HEAD_ORDER])
    pad = lambda n: np.full((n,), -1)
    cols = np.concatenate([
        np.arange(a, a + 256),
        np.arange(a + 256, a + 384),
        np.arange(a + 384, a + 640),
        np.arange(a + 640, a + 652), pad(116),
        np.arange(a + 652, a + 908),
        bb + 512 + perm,
        np.arange(cc + 768, cc + 1024),
        np.arange(dd + 768, dd + 1024),
        bb + perm,
        np.arange(bb + 256, bb + 512),
        np.arange(cc, cc + 768),
        np.arange(dd, dd + 768),
    ])
    assert cols.shape[0] == PROJ_W
    return cols, perm


def _position_features(s):
    pos = np.arange(s)
    feat = np.zeros((s, 64), np.float32)
    feat[:, 0] = SEL_BLOCK * (pos // SEL_BLOCK)
    feat[:, 1] = pos % SEL_BLOCK
    onehot = (pos[:, None] // SEL_BLOCK == np.arange(128)[None, :]).astype(np.float32)
    return jnp.asarray(feat, BF16), jnp.asarray(onehot, BF16)


def kernel(x, c, w_ada, b_ada, norm_g, w_in, cmp_pos, cmp_w1, cmp_w2, sink, w_merge, w_br, w_out, final_g):
    b, s, d = x.shape
    depth = w_ada.shape[0]
    in_w = w_in.shape[2]
    nc = s // CMP_BLOCK
    tm = min(512, s)
    tq = 128

    cols, perm = _proj_columns()
    w_in_z = jnp.concatenate([w_in, jnp.zeros((depth, d, 1), F32)], axis=2)
    w_in_p = jnp.take(w_in_z, jnp.asarray(np.where(cols < 0, in_w, cols)), axis=2).astype(BF16)
    wm_b = w_merge.astype(BF16)
    wbr_b = w_br.at[:, 1].set(w_br[:, 1][:, perm, :]).astype(BF16)
    wout_b = w_out.astype(BF16)
    w1_b = cmp_w1.astype(BF16)
    zeros64 = jnp.zeros((depth, HEAD_DIM, HEAD_DIM), F32)
    w2_p = jnp.stack([jnp.concatenate([cmp_w2[:, 0], zeros64], axis=2),
                      jnp.concatenate([zeros64, cmp_w2[:, 1]], axis=2)], axis=1).astype(BF16)
    pos_flat = cmp_pos.reshape(depth, 2, CMP_BLOCK * HEAD_DIM)
    feat, onehot = _position_features(s)
    feat_b = jnp.broadcast_to(feat[None], (b, s, 64))
    onehot_b = jnp.broadcast_to(onehot[None], (b, s, 128))
    ret_consts = _retention_consts()
    mods = _modulation(c, w_ada, b_ada)

    def even_odd_blocks(zc):
        zb = zc.reshape(b, nc // 2, 2, CMP_BLOCK * HEAD_DIM)
        return zb.swapaxes(1, 2).reshape(b, nc, CMP_BLOCK * HEAD_DIM)

    for l in range(depth):
        p = _project(x, mods[l], norm_g[l], w_in_p[l], tm)
        kc, vc = _compress(even_odd_blocks(p["acmp"][..., :64]), even_odd_blocks(p["acmp"][..., 64:]),
                           pos_flat[l], w1_b[l], w2_p[l])
        akv = p["akv"]
        ksel = jnp.concatenate([akv[..., 0:64], feat_b, onehot_b], axis=-1)
        kwin = jnp.concatenate([akv[..., 128:192], feat_b], axis=-1)
        y_a = _nsa(p["aq"], p["ag"], kc, vc, ksel, akv, kwin, tq)
        y_b = _swa(sink[l], p["bq"], p["bkv"], tq)
        y_c = _stick(p["cq"], p["ck"], p["cv"], tq)
        y_d = _retention(p["dq"], p["dk"], p["dv"], ret_consts)
        x = _merge(x, mods[l], norm_g[l], (y_a, y_b, y_c, y_d), p["z"], wm_b[l], wbr_b[l], wout_b[l], tm)
    return _final_norm(x, final_g, tm)
```

```python
import functools

import numpy as np
import jax
import jax.numpy as jnp
from jax import lax
from jax.experimental import pallas as pl
from jax.experimental.pallas import tpu as pltpu

F32 = jnp.float32
BF16 = jnp.bfloat16

HEAD_DIM = 64
N_HEADS = 4
BRANCH_W = N_HEADS * HEAD_DIM
CMP_BLOCK = 32
SEL_BLOCK = 64
SEL_TOPK = 16
NSA_WINDOW = 512
NSA_TK = 512
SWA_WINDOW = 128
RET_CHUNK = 128
RMS_EPS = 1e-6
LN_EPS = 1e-5
NEG_INF = -1e30
TINY = 1e-30
FORCED_SCORE = 1e4
QK_SCALE = HEAD_DIM ** -0.5
STICK_EXIT = -110.0

VMEM_LIMIT = 56 * 1024 * 1024
SLOPES = tuple(float(2.0 ** (-8.0 * (h + 1) / N_HEADS)) for h in range(N_HEADS))

PROJ_OUTS = (
    ("aq", 0, 256, BF16),
    ("acmp", 256, 128, F32),
    ("akv", 384, 256, BF16),
    ("ag", 640, 128, F32),
    ("z", 768, 1024, F32),
    ("bq", 1792, 256, BF16),
    ("bkv", 2048, 256, BF16),
    ("cq", 2304, 256, BF16),
    ("ck", 2560, 256, BF16),
    ("cv", 2816, 256, BF16),
    ("dq", 3072, 256, BF16),
    ("dk", 3328, 256, BF16),
    ("dv", 3584, 256, BF16),
)
PROJ_W = 3840
SWA_HEAD_ORDER = (0, 2, 1, 3)


def _dot(a, b):
    return jnp.dot(a, b, preferred_element_type=F32)


def _dot_t(a, b):
    return lax.dot_general(a, b, (((1,), (1,)), ((), ())), preferred_element_type=F32)


def _dot_split(a, b):
    hi = a.astype(BF16)
    lo = (a - hi.astype(F32)).astype(BF16)
    return _dot(hi, b) + _dot(lo, b)


def _sigmoid(v):
    return 1.0 / (1.0 + jnp.exp(-v))


def _silu(v):
    return v * _sigmoid(v)


def _params(*sem):
    return pltpu.CompilerParams(dimension_semantics=sem, vmem_limit_bytes=VMEM_LIMIT)


def _mod_kernel(c_ref, w_ref, b_ref, o_ref):
    cc = c_ref[...]
    o_ref[0] = _dot(_silu(cc).astype(BF16), w_ref[0].astype(BF16)) + b_ref[0]


def _modulation(c, w_ada, b_ada):
    depth, d, n = w_ada.shape
    b = c.shape[0]
    rows = 8
    cp = jnp.zeros((rows, d), F32).at[:b].set(c)
    tn = 512
    out = pl.pallas_call(
        _mod_kernel,
        grid=(depth, n // tn),
        in_specs=[
            pl.BlockSpec((rows, d), lambda l, j: (0, 0)),
            pl.BlockSpec((1, d, tn), lambda l, j: (l, 0, j)),
            pl.BlockSpec((1, 1, tn), lambda l, j: (l, 0, j)),
        ],
        out_specs=pl.BlockSpec((1, rows, tn), lambda l, j: (l, 0, j)),
        out_shape=jax.ShapeDtypeStruct((depth, rows, n), F32),
        compiler_params=_params("arbitrary", "arbitrary"),
        name="adaln_mod",
    )(cp, w_ada, b_ada.reshape(depth, 1, n))
    return out[:, :b].reshape(depth, b, 1, n)


def _modulated_norm(xf, g, mod, d):
    ms = jnp.mean(xf * xf, axis=-1, keepdims=True)
    y = xf * lax.rsqrt(ms + RMS_EPS) * g
    return y * (1.0 + mod[:, d:2 * d]) + mod[:, 0:d]


def _proj_kernel(x_ref, mod_ref, g_ref, w_ref, *out_refs, d):
    h = _modulated_norm(x_ref[0], g_ref[...], mod_ref[0], d).astype(BF16)
    for (_, start, width, dtype), o_ref in zip(PROJ_OUTS, out_refs):
        for c0 in range(0, width, 256):
            cw = min(256, width - c0)
            o_ref[0, :, c0:c0 + cw] = _dot(h, w_ref[:, start + c0:start + c0 + cw]).astype(dtype)


def _project(x, mod, g, w_p, tm):
    b, s, d = x.shape
    out_shapes = [jax.ShapeDtypeStruct((b, s, wd), dt) for (_, _, wd, dt) in PROJ_OUTS]
    out_specs = [pl.BlockSpec((1, tm, wd), lambda i, j: (i, j, 0)) for (_, _, wd, _) in PROJ_OUTS]
    outs = pl.pallas_call(
        functools.partial(_proj_kernel, d=d),
        grid=(b, s // tm),
        in_specs=[
            pl.BlockSpec((1, tm, d), lambda i, j: (i, j, 0)),
            pl.BlockSpec((1, 1, 3 * d), lambda i, j: (i, 0, 0)),
            pl.BlockSpec((1, d), lambda i, j: (0, 0)),
            pl.BlockSpec((d, PROJ_W), lambda i, j: (0, 0)),
        ],
        out_specs=out_specs,
        out_shape=out_shapes,
        compiler_params=_params("arbitrary", "arbitrary"),
        name="in_proj",
    )(x, mod, g.reshape(1, d), w_p)
    return {name: o for (name, _, _, _), o in zip(PROJ_OUTS, outs)}


def _compress_kernel(zk_ref, zv_ref, pos_ref, w1_ref, w2_ref, kc_ref, vc_ref):
    for j, (z_ref, o_ref) in enumerate(((zk_ref, kc_ref), (zv_ref, vc_ref))):
        zb = (z_ref[0] + pos_ref[j:j + 1, :]).astype(BF16)
        hid = _silu(_dot(zb, w1_ref[j])).astype(BF16)
        o_ref[0] = _dot(hid, w2_ref[j]).astype(BF16)


def _compress(zk, zv, pos, w1, w2p):
    b, nc, kd = zk.shape
    spec_z = pl.BlockSpec((1, nc, kd), lambda i: (i, 0, 0))
    spec_o = pl.BlockSpec((1, nc, 128), lambda i: (i, 0, 0))
    return pl.pallas_call(
        _compress_kernel,
        grid=(b,),
        in_specs=[
            spec_z, spec_z,
            pl.BlockSpec((2, kd), lambda i: (0, 0)),
            pl.BlockSpec((2, kd, HEAD_DIM), lambda i: (0, 0, 0)),
            pl.BlockSpec((2, HEAD_DIM, 128), lambda i: (0, 0, 0)),
        ],
        out_specs=[spec_o, spec_o],
        out_shape=[jax.ShapeDtypeStruct((b, nc, 128), BF16)] * 2,
        compiler_params=_params("arbitrary"),
        name="nsa_compress",
    )(zk, zv, pos, w1, w2p)


def _nsa_kernel(aq_ref, ag_ref, kc_ref, vc_ref, ksel_ref, vsel_ref, kwin_ref, vwin_ref, ya_ref,
                qaug_ref, mx_ref, sm_ref, acc_ref, owin_ref, *, tq, seq):
    qi = pl.program_id(1)
    qstart = qi * tq
    nc = seq // CMP_BLOCK
    half = nc // 2
    ns = seq // SEL_BLOCK
    t_lane = lax.broadcasted_iota(jnp.int32, (1, tq), 1) + qstart

    row_c = lax.broadcasted_iota(jnp.int32, (nc, 1), 0)
    blk_c = jnp.where(row_c < half, 2 * row_c, 2 * (row_c - half) + 1)
    dist = t_lane - (blk_c * CMP_BLOCK + (CMP_BLOCK - 1))
    dist_f = dist.astype(F32)
    mask_c = dist >= 0
    lane128 = lax.broadcasted_iota(jnp.int32, (1, 128), 1)
    kc = kc_ref[0]
    vc = vc_ref[0]
    psum = jnp.zeros((nc, tq), F32)
    ocmp = []
    for h in range(N_HEADS):
        grp = aq_ref[0, :, 128 * (h // 2):128 * (h // 2) + 128].astype(F32)
        if h % 2 == 1:
            grp = pltpu.roll(grp, 64, 1)
        left = jnp.where(lane128 < HEAD_DIM, grp * QK_SCALE,
                         jnp.where(lane128 < HEAD_DIM + 2, SLOPES[h], 0.0)).astype(BF16)
        qaug_ref[h * tq:(h + 1) * tq, 0:128] = left
        s = _dot_t(kc, left) - SLOPES[h] * dist_f
        s = jnp.where(mask_c, s, NEG_INF)
        m = jnp.max(s, axis=0, keepdims=True)
        e = jnp.where(mask_c, jnp.exp(s - m), 0.0)
        den = jnp.sum(e, axis=0, keepdims=True)
        p = e * (1.0 / jnp.maximum(den, TINY))
        psum = psum + p
        ocmp.append(lax.dot_general(p.astype(BF16), vc, (((0,), (0,)), ((), ())),
                                    preferred_element_type=F32))

    rows = N_HEADS * tq
    t_row = (lax.broadcasted_iota(jnp.int32, (rows, 1), 0) & (tq - 1)) + qstart
    n_win = NSA_WINDOW + tq
    w0 = pl.multiple_of(jnp.clip(qstart - NSA_WINDOW, 0, seq - n_win), tq)
    rel_w = t_row - (lax.broadcasted_iota(jnp.int32, (1, n_win), 1) + w0)
    sw = _dot_t(qaug_ref[:, 0:128], kwin_ref[0, pl.ds(w0, n_win), :])
    sw = jnp.where((rel_w >= 0) & (rel_w < NSA_WINDOW), sw, NEG_INF)
    ew = jnp.exp(sw - jnp.max(sw, axis=-1, keepdims=True))
    owin_ref[...] = (_dot(ew.astype(BF16), vwin_ref[0, pl.ds(w0, n_win), :])
                     * (1.0 / jnp.sum(ew, axis=-1, keepdims=True)))

    imp = psum[:half] + psum[half:]
    blk = lax.broadcasted_iota(jnp.int32, (ns, 1), 0)
    cur = t_lane >> 6
    future = blk * SEL_BLOCK > t_lane
    forced = (blk == 0) | (blk == cur) | (blk == cur - 1)
    score = jnp.where(forced, FORCED_SCORE, jnp.where(future, -1.0, imp))
    blk_f = blk.astype(F32)
    sel_t = jnp.zeros((ns, tq), F32)
    for _ in range(min(SEL_TOPK, ns)):
        mx = jnp.max(score, axis=0, keepdims=True)
        first = jnp.min(jnp.where(score == mx, blk_f, float(ns)), axis=0, keepdims=True)
        pick = blk_f == first
        sel_t = jnp.where(pick, 1.0, sel_t)
        score = jnp.where(pick, -jnp.inf, score)
    sel_f = jnp.transpose(sel_t)
    negmask = jnp.where(sel_f > 0.5, 0.0, NEG_INF).astype(BF16)
    if ns < 128:
        negmask = jnp.concatenate([negmask, jnp.zeros((tq, 128 - ns), BF16)], axis=1)
    for h in range(N_HEADS):
        qaug_ref[h * tq:(h + 1) * tq, 128:256] = negmask

    if ns < 128:
        sel_f = jnp.concatenate([sel_f, jnp.zeros((tq, 128 - ns), F32)], axis=1)
    any_blk = jnp.max(sel_f, axis=0, keepdims=True)
    blocks_per_tile = NSA_TK // SEL_BLOCK
    shift = 1
    while shift < blocks_per_tile:
        any_blk = jnp.maximum(any_blk, pltpu.roll(any_blk, 128 - shift, 1))
        shift *= 2
    tile_of_lane = lane128 >> (blocks_per_tile.bit_length() - 1)
    first_of_tile = (lane128 & (blocks_per_tile - 1)) == 0
    pow2 = lax.bitcast_convert_type(((tile_of_lane & 15) + 127) << 23, F32)
    weighted = jnp.where(first_of_tile, any_blk * pow2, 0.0)
    bits_lo = jnp.sum(jnp.where(tile_of_lane < 16, weighted, 0.0), axis=1, keepdims=True).astype(jnp.int32)[0, 0]
    bits_hi = jnp.sum(jnp.where(tile_of_lane >= 16, weighted, 0.0), axis=1, keepdims=True).astype(jnp.int32)[0, 0]

    def tile_selected(t):
        return ((jnp.where(t < 16, bits_lo, bits_hi) >> (t & 15)) & 1) == 1

    col_k = lax.broadcasted_iota(jnp.int32, (1, NSA_TK), 1)
    t_last = qi // (NSA_TK // tq)
    groups = NSA_TK // 128

    def scores(t, diagonal):
        k0 = pl.multiple_of(t * NSA_TK, NSA_TK)
        sc = _dot_t(qaug_ref[...], ksel_ref[0, pl.ds(k0, NSA_TK), :])
        if diagonal:
            sc = jnp.where(col_k + k0 <= t_row, sc, NEG_INF)
        return sc

    def fold_max(t, diagonal):
        sc = scores(t, diagonal)
        m = mx_ref[...]
        for g in range(groups):
            m = jnp.maximum(m, sc[:, 128 * g:128 * (g + 1)])
        mx_ref[...] = m

    def accumulate(t, diagonal):
        sc = scores(t, diagonal)
        m = mx_ref[...]
        e = jnp.exp(sc - jnp.concatenate([m] * groups, axis=1))
        part = sm_ref[...]
        for g in range(groups):
            part = part + e[:, 128 * g:128 * (g + 1)]
        sm_ref[...] = part
        k0 = pl.multiple_of(t * NSA_TK, NSA_TK)
        acc_ref[...] += _dot(e.astype(BF16), vsel_ref[0, pl.ds(k0, NSA_TK), :])

    def over_tiles(fn):
        def body(t, carry):
            pl.when(tile_selected(t))(lambda: fn(t, False))
            return carry
        lax.fori_loop(0, t_last, body, 0)
        fn(t_last, True)

    mx_ref[...] = jnp.full(mx_ref.shape, NEG_INF, F32)
    over_tiles(fold_max)
    mx_ref[...] = jnp.broadcast_to(jnp.max(mx_ref[...], axis=-1, keepdims=True), mx_ref.shape)
    sm_ref[...] = jnp.zeros(sm_ref.shape, F32)
    acc_ref[...] = jnp.zeros(acc_ref.shape, F32)
    over_tiles(accumulate)
    osel = acc_ref[...] * (1.0 / jnp.sum(sm_ref[...], axis=-1, keepdims=True))

    gate = _sigmoid(ag_ref[0])
    comb = []
    for h in range(N_HEADS):
        head_rows = slice(h * tq, (h + 1) * tq)
        comb.append(gate[:, 3 * h:3 * h + 1] * ocmp[h]
                    + gate[:, 3 * h + 1:3 * h + 2] * osel[head_rows, :]
                    + gate[:, 3 * h + 2:3 * h + 3] * owin_ref[head_rows, :])
    for g in range(2):
        ya_ref[0, :, 128 * g:128 * (g + 1)] = jnp.where(
            lane128 < HEAD_DIM, pltpu.roll(comb[2 * g], 64, 1), comb[2 * g + 1])


def _nsa(aq, ag, kc, vc, ksel, akv, kwin, tq):
    b, s, _ = aq.shape
    nc = kc.shape[1]
    rows = N_HEADS * tq
    whole = lambda w, blk: pl.BlockSpec((1, s, w), lambda i, j, blk=blk: (i, 0, blk))
    return pl.pallas_call(
        functools.partial(_nsa_kernel, tq=tq, seq=s),
        grid=(b, s // tq),
        in_specs=[
            pl.BlockSpec((1, tq, 256), lambda i, j: (i, j, 0)),
            pl.BlockSpec((1, tq, 128), lambda i, j: (i, j, 0)),
            pl.BlockSpec((1, nc, 128), lambda i, j: (i, 0, 0)),
            pl.BlockSpec((1, nc, 128), lambda i, j: (i, 0, 0)),
            whole(256, 0),
            whole(128, 0),
            whole(128, 0),
            whole(128, 1),
        ],
        out_specs=pl.BlockSpec((1, tq, 256), lambda i, j: (i, j, 0)),
        out_shape=jax.ShapeDtypeStruct((b, s, 256), F32),
        scratch_shapes=[
            pltpu.VMEM((rows, 256), BF16),
            pltpu.VMEM((rows, 128), F32),
            pltpu.VMEM((rows, 128), F32),
            pltpu.VMEM((rows, 128), F32),
            pltpu.VMEM((rows, 128), F32),
        ],
        compiler_params=_params("arbitrary", "arbitrary"),
        name="nsa_attention",
    )(aq, ag, kc, vc, ksel, akv, kwin, akv)


def _swa_kernel(sink_ref, bq_ref, bkv_ref, yb_ref, *, tq, seq):
    qi = pl.program_id(1)
    qstart = qi * tq
    nk = 2 * tq
    k0 = pl.multiple_of(jnp.maximum(qstart - tq, 0), tq)
    kk = bkv_ref[0, pl.ds(k0, nk), 0:128]
    vv = bkv_ref[0, pl.ds(k0, nk), 128:256]
    t_col = lax.broadcasted_iota(jnp.int32, (tq, 1), 0) + qstart
    s_row = lax.broadcasted_iota(jnp.int32, (1, nk), 1) + k0
    rel = t_col - s_row
    rel_f = rel.astype(F32)
    mask = (rel >= 0) & (rel < SWA_WINDOW)
    lane128 = lax.broadcasted_iota(jnp.int32, (1, 128), 1)
    for g in range(2):
        qg = bq_ref[0, :, 128 * g:128 * (g + 1)]
        outs = []
        for p in range(2):
            h = SWA_HEAD_ORDER[2 * g + p]
            in_half = (lane128 >= HEAD_DIM * p) & (lane128 < HEAD_DIM * (p + 1))
            qh = jnp.where(in_half, qg, jnp.zeros_like(qg))
            s = _dot_t(qh, kk) * QK_SCALE - SLOPES[h] * rel_f
            s = jnp.where(mask, s, NEG_INF)
            sink = sink_ref[h]
            m = jnp.maximum(jnp.max(s, axis=-1, keepdims=True), sink)
            e = jnp.where(mask, jnp.exp(s - m), 0.0)
            den = jnp.sum(e, axis=-1, keepdims=True) + jnp.exp(sink - m)
            pr = e / jnp.maximum(den, TINY)
            outs.append(_dot(pr.astype(BF16), vv))
        yb_ref[0, :, 128 * g:128 * (g + 1)] = jnp.where(lane128 < HEAD_DIM, outs[0], outs[1])


def _swa(sink, bq, bkv, tq):
    b, s, _ = bq.shape
    return pl.pallas_call(
        functools.partial(_swa_kernel, tq=tq, seq=s),
        grid=(b, s // tq),
        in_specs=[
            pl.BlockSpec(memory_space=pltpu.SMEM),
            pl.BlockSpec((1, tq, 256), lambda i, j: (i, j, 0)),
            pl.BlockSpec((1, s, 256), lambda i, j: (i, 0, 0)),
        ],
        out_specs=pl.BlockSpec((1, tq, 256), lambda i, j: (i, j, 0)),
        out_shape=jax.ShapeDtypeStruct((b, s, 256), F32),
        compiler_params=_params("arbitrary", "arbitrary"),
        name="swa_attention",
    )(sink, bq, bkv)


def _stick_kernel(cq_ref, ck_ref, cv_ref, yc_ref, qm_ref, carry_ref, o_ref, *, tq, seq):
    qi = pl.program_id(1)
    qstart = qi * tq
    lane256 = lax.broadcasted_iota(jnp.int32, (1, 256), 1)
    head_of_lane = lane256 >> 6
    rows = N_HEADS * tq
    q = cq_ref[0].astype(F32) * QK_SCALE
    for h in range(N_HEADS):
        qm_ref[h * tq:(h + 1) * tq, :] = jnp.where(head_of_lane == h, q, 0.0).astype(BF16)
    jj = lax.broadcasted_iota(jnp.int32, (tq, 2 * tq), 0)
    ss = lax.broadcasted_iota(jnp.int32, (tq, 2 * tq), 1)
    upper = ((jj > ss) | (ss >= tq)).astype(BF16)
    t_in = lax.broadcasted_iota(jnp.int32, (rows, tq), 0) & (tq - 1)
    strict = lax.broadcasted_iota(jnp.int32, (rows, tq), 1) < t_in

    def tile(k0, diagonal):
        kt = ck_ref[0, pl.ds(k0, tq), :]
        vt = cv_ref[0, pl.ds(k0, tq), :]
        z = _dot_t(qm_ref[...], kt)
        soft = jnp.log1p(jnp.exp(-jnp.abs(z)))
        log_beta = jnp.minimum(z, 0.0) - soft
        log_1m = log_beta - z
        if diagonal:
            log_1m = jnp.where(strict, log_1m, 0.0)
        sums = _dot_split(log_1m, upper)
        suffix = sums[:, :tq]
        if not diagonal:
            suffix = suffix + carry_ref[...]
        a = jnp.exp(log_beta + suffix)
        if diagonal:
            a = jnp.where(strict, a, 0.0)
        pv = _dot(a.astype(BF16), vt)
        acc = o_ref[...]
        for h in range(N_HEADS):
            acc = acc + jnp.where(head_of_lane == h, pv[h * tq:(h + 1) * tq, :], 0.0)
        o_ref[...] = acc
        new_carry = sums[:, tq:] if diagonal else carry_ref[...] + sums[:, tq:]
        carry_ref[...] = new_carry
        return jnp.max(new_carry, axis=0, keepdims=True)[0, 0]

    o_ref[...] = jnp.zeros(o_ref.shape, F32)
    worst0 = tile(pl.multiple_of(qstart, tq), True)

    def cond(state):
        j, worst = state
        return (j >= 0) & (worst > STICK_EXIT)

    def body(state):
        j, _ = state
        return j - 1, tile(pl.multiple_of(j * tq, tq), False)

    lax.while_loop(cond, body, (qi - 1, worst0))
    yc_ref[0] = o_ref[...]


def _stick(cq, ck, cv, tq):
    b, s, _ = cq.shape
    return pl.pallas_call(
        functools.partial(_stick_kernel, tq=tq, seq=s),
        grid=(b, s // tq),
        in_specs=[
            pl.BlockSpec((1, tq, 256), lambda i, j: (i, j, 0)),
            pl.BlockSpec((1, s, 256), lambda i, j: (i, 0, 0)),
            pl.BlockSpec((1, s, 256), lambda i, j: (i, 0, 0)),
        ],
        out_specs=pl.BlockSpec((1, tq, 256), lambda i, j: (i, j, 0)),
        out_shape=jax.ShapeDtypeStruct((b, s, 256), F32),
        scratch_shapes=[
            pltpu.VMEM((N_HEADS * tq, 256), BF16),
            pltpu.VMEM((N_HEADS * tq, tq), F32),
            pltpu.VMEM((tq, 256), F32),
        ],
        compiler_params=_params("arbitrary", "arbitrary"),
        name="stick_breaking",
    )(cq, ck, cv)


def _ret_kernel(dq_ref, dk_ref, dv_ref, dmat_ref, xi_ref, zeta_ref, decay_ref, bd_ref, yd_ref, r_ref):
    n = pl.program_id(1)

    @pl.when(n == 0)
    def _():
        r_ref[...] = jnp.zeros(r_ref.shape, F32)

    c = RET_CHUNK
    lane256 = lax.broadcasted_iota(jnp.int32, (1, 256), 1)
    head_of_lane = lane256 >> 6
    q = dq_ref[0].astype(F32) * QK_SCALE
    k = dk_ref[0]
    v = dv_ref[0]
    o = _dot(q.astype(BF16), r_ref[...].astype(BF16)) * xi_ref[...]
    for h in range(N_HEADS):
        qh = jnp.where(head_of_lane == h, q, 0.0).astype(BF16)
        s = _dot_t(qh, k) * dmat_ref[h]
        o = o + jnp.where(head_of_lane == h, _dot(s.astype(BF16), v), 0.0)
    mu = jnp.zeros((c, 256), F32)
    for h in range(N_HEADS):
        in_h = head_of_lane == h
        mu = mu + jnp.where(in_h, jnp.sum(jnp.where(in_h, o, 0.0), axis=-1, keepdims=True), 0.0)
    cen = o - mu * (1.0 / HEAD_DIM)
    var = jnp.zeros((c, 256), F32)
    for h in range(N_HEADS):
        in_h = head_of_lane == h
        var = var + jnp.where(in_h, jnp.sum(jnp.where(in_h, cen * cen, 0.0), axis=-1, keepdims=True), 0.0)
    yd_ref[0] = cen * lax.rsqrt(var * (1.0 / HEAD_DIM) + LN_EPS)
    kz = (k.astype(F32) * zeta_ref[...]).astype(BF16)
    upd = lax.dot_general(kz, v, (((0,), (0,)), ((), ())), preferred_element_type=F32)
    r_ref[...] = r_ref[...] * decay_ref[...] + upd * bd_ref[...]


def _retention(dq, dk, dv, consts):
    b, s, _ = dq.shape
    c = RET_CHUNK
    dmat, xi, zeta, decay, bd = consts
    tile = pl.BlockSpec((1, c, 256), lambda i, j: (i, j, 0))
    return pl.pallas_call(
        _ret_kernel,
        grid=(b, s // c),
        in_specs=[
            tile, tile, tile,
            pl.BlockSpec((N_HEADS, c, c), lambda i, j: (0, 0, 0)),
            pl.BlockSpec((c, 256), lambda i, j: (0, 0)),
            pl.BlockSpec((c, 256), lambda i, j: (0, 0)),
            pl.BlockSpec((256, 256), lambda i, j: (0, 0)),
            pl.BlockSpec((256, 256), lambda i, j: (0, 0)),
        ],
        out_specs=tile,
        out_shape=jax.ShapeDtypeStruct((b, s, 256), F32),
        scratch_shapes=[pltpu.VMEM((256, 256), F32)],
        compiler_params=_params("arbitrary", "arbitrary"),
        name="retention",
    )(dq, dk, dv, dmat, xi, zeta, decay, bd)


def _retention_consts():
    c = RET_CHUNK
    log_g = jnp.log(1.0 - jnp.asarray(2.0 ** (-5.0 - np.arange(N_HEADS)), dtype=F32))
    i = jnp.arange(c)
    diff = (i[:, None] - i[None, :]).astype(F32)
    dmat = jnp.where(diff >= 0, jnp.exp(log_g[:, None, None] * jnp.maximum(diff, 0.0)), 0.0)
    zeta = jnp.exp(log_g[:, None] * (c - 1 - i)[None, :].astype(F32))
    xi = jnp.exp(log_g[:, None] * (i + 1)[None, :].astype(F32))
    g_chunk = jnp.exp(log_g * c)
    lanes = lambda hc: jnp.repeat(hc.T, HEAD_DIM, axis=1)
    head = np.arange(256) // HEAD_DIM
    bd = jnp.asarray(head[:, None] == head[None, :], F32)
    decay = jnp.repeat(g_chunk, HEAD_DIM)[:, None] * jnp.ones((1, 256), F32)
    return dmat, lanes(xi), lanes(zeta), decay, bd


def _merge_kernel(x_ref, mod_ref, g_ref, ya_ref, yb_ref, yc_ref, yd_ref, z_ref,
                  wm_ref, wbr_ref, wout_ref, o_ref, *, d):
    xf = x_ref[0]
    mod = mod_ref[0]
    h = _modulated_norm(xf, g_ref[...], mod, d).astype(BF16)
    merged = None
    for i, y_ref in enumerate((ya_ref, yb_ref, yc_ref, yd_ref)):
        yi = (y_ref[0] * _silu(z_ref[0, :, 256 * i:256 * (i + 1)])).astype(BF16)
        term = _sigmoid(_dot(h, wm_ref[i])) * _dot(yi, wbr_ref[i])
        merged = term if merged is None else merged + term
    o_ref[0] = xf + mod[:, 2 * d:3 * d] * _dot(merged.astype(BF16), wout_ref[...])


def _merge(x, mod, g, ys, z, wm, wbr, wout, tm):
    b, s, d = x.shape
    ytile = pl.BlockSpec((1, tm, 256), lambda i, j: (i, j, 0))
    return pl.pallas_call(
        functools.partial(_merge_kernel, d=d),
        grid=(b, s // tm),
        in_specs=[
            pl.BlockSpec((1, tm, d), lambda i, j: (i, j, 0)),
            pl.BlockSpec((1, 1, 3 * d), lambda i, j: (i, 0, 0)),
            pl.BlockSpec((1, d), lambda i, j: (0, 0)),
            ytile, ytile, ytile, ytile,
            pl.BlockSpec((1, tm, 4 * 256), lambda i, j: (i, j, 0)),
            pl.BlockSpec((4, d, d), lambda i, j: (0, 0, 0)),
            pl.BlockSpec((4, 256, d), lambda i, j: (0, 0, 0)),
            pl.BlockSpec((d, d), lambda i, j: (0, 0)),
        ],
        out_specs=pl.BlockSpec((1, tm, d), lambda i, j: (i, j, 0)),
        out_shape=jax.ShapeDtypeStruct((b, s, d), F32),
        compiler_params=_params("arbitrary", "arbitrary"),
        name="merge_out",
    )(x, mod, g.reshape(1, d), *ys, z, wm, wbr, wout)


def _final_norm_kernel(x_ref, g_ref, o_ref):
    xf = x_ref[0]
    ms = jnp.mean(xf * xf, axis=-1, keepdims=True)
    o_ref[0] = xf * lax.rsqrt(ms + RMS_EPS) * g_ref[...]


def _final_norm(x, g, tm):
    b, s, d = x.shape
    return pl.pallas_call(
        _final_norm_kernel,
        grid=(b, s // tm),
        in_specs=[pl.BlockSpec((1, tm, d), lambda i, j: (i, j, 0)),
                  pl.BlockSpec((1, d), lambda i, j: (0, 0))],
        out_specs=pl.BlockSpec((1, tm, d), lambda i, j: (i, j, 0)),
        out_shape=jax.ShapeDtypeStruct((b, s, d), F32),
        compiler_params=_params("arbitrary", "arbitrary"),
        name="final_norm",
    )(x, g.reshape(1, d))


def _proj_columns():
    a, bb, cc, dd = 0, 908, 1676, 2700
    perm = np.concatenate([np.arange(64) + 64 * h for h in SWA_HEAD_ORDER])
    pad = lambda n: np.full((n,), -1)
    cols = np.concatenate([
        np.arange(a, a + 256),
        np.arange(a + 256, a + 384),
        np.arange(a + 384, a + 640),
        np.arange(a + 640, a + 652), pad(116),
        np.arange(a + 652, a + 908),
        bb + 512 + perm,
        np.arange(cc + 768, cc + 1024),
        np.arange(dd + 768, dd + 1024),
        bb + perm,
        np.arange(bb + 256, bb + 512),
        np.arange(cc, cc + 768),
        np.arange(dd, dd + 768),
    ])
    assert cols.shape[0] == PROJ_W
    return cols, perm


def _position_features(s):
    pos = np.arange(s)
    feat = np.zeros((s, 64), np.float32)
    feat[:, 0] = SEL_BLOCK * (pos // SEL_BLOCK)
    feat[:, 1] = pos % SEL_BLOCK
    onehot = (pos[:, None] // SEL_BLOCK == np.arange(128)[None, :]).astype(np.float32)
    return jnp.asarray(feat, BF16), jnp.asarray(onehot, BF16)


def kernel(x, c, w_ada, b_ada, norm_g, w_in, cmp_pos, cmp_w1, cmp_w2, sink, w_merge, w_br, w_out, final_g):
    b, s, d = x.shape
    depth = w_ada.shape[0]
    in_w = w_in.shape[2]
    nc = s // CMP_BLOCK
    tm = min(512, s)
    tq = 128

    cols, perm = _proj_columns()
    w_in_z = jnp.concatenate([w_in, jnp.zeros((depth, d, 1), F32)], axis=2)
    w_in_p = jnp.take(w_in_z, jnp.asarray(np.where(cols < 0, in_w, cols)), axis=2).astype(BF16)
    wm_b = w_merge.astype(BF16)
    wbr_b = w_br.at[:, 1].set(w_br[:, 1][:, perm, :]).astype(BF16)
    wout_b = w_out.astype(BF16)
    w1_b = cmp_w1.astype(BF16)
    zeros64 = jnp.zeros((depth, HEAD_DIM, HEAD_DIM), F32)
    w2_p = jnp.stack([jnp.concatenate([cmp_w2[:, 0], zeros64], axis=2),
                      jnp.concatenate([zeros64, cmp_w2[:, 1]], axis=2)], axis=1).astype(BF16)
    pos_flat = cmp_pos.reshape(depth, 2, CMP_BLOCK * HEAD_DIM)
    feat, onehot = _position_features(s)
    feat_b = jnp.broadcast_to(feat[None], (b, s, 64))
    onehot_b = jnp.broadcast_to(onehot[None], (b, s, 128))
    ret_consts = _retention_consts()
    mods = _modulation(c, w_ada, b_ada)

    def even_odd_blocks(zc):
        zb = zc.reshape(b, nc // 2, 2, CMP_BLOCK * HEAD_DIM)
        return zb.swapaxes(1, 2).reshape(b, nc, CMP_BLOCK * HEAD_DIM)

    for l in range(depth):
        p = _project(x, mods[l], norm_g[l], w_in_p[l], tm)
        kc, vc = _compress(even_odd_blocks(p["acmp"][..., :64]), even_odd_blocks(p["acmp"][..., 64:]),
                           pos_flat[l], w1_b[l], w2_p[l])
        akv = p["akv"]
        ksel = jnp.concatenate([akv[..., 0:64], feat_b, onehot_b], axis=-1)
        kwin = jnp.concatenate([akv[..., 128:192], feat_b], axis=-1)
        y_a = _nsa(p["aq"], p["ag"], kc, vc, ksel, akv, kwin, tq)
        y_b = _swa(sink[l], p["bq"], p["bkv"], tq)
        y_c = _stick(p["cq"], p["ck"], p["cv"], tq)
        y_d = _retention(p["dq"], p["dk"], p["dv"], ret_consts)
        x = _merge(x, mods[l], norm_g[l], (y_a, y_b, y_c, y_d), p["z"], wm_b[l], wbr_b[l], wout_b[l], tm)
    return _final_norm(x, final_g, tm)
```

```python
import functools

import numpy as np
import jax
import jax.numpy as jnp
from jax import lax
from jax.experimental import pallas as pl
from jax.experimental.pallas import tpu as pltpu

F32 = jnp.float32
BF16 = jnp.bfloat16

HEAD_DIM = 64
N_HEADS = 4
BRANCH_W = N_HEADS * HEAD_DIM
CMP_BLOCK = 32
SEL_BLOCK = 64
SEL_TOPK = 16
NSA_WINDOW = 512
NSA_TK = 512
SWA_WINDOW = 128
RET_CHUNK = 128
RMS_EPS = 1e-6
LN_EPS = 1e-5
NEG_INF = -1e30
TINY = 1e-30
FORCED_SCORE = 1e4
QK_SCALE = HEAD_DIM ** -0.5
STICK_EXIT = -110.0
STICK_EAGER_TILES = 2
MAX_BOUND_GAP = 60.0
BOUND_SLACK = 1.001

ROW_TILE = 512
NSA_TQ = 256
SWA_TQ = 128
STICK_TQ = 128

VMEM_LIMIT = 56 * 1024 * 1024
SLOPES = tuple(float(2.0 ** (-8.0 * (h + 1) / N_HEADS)) for h in range(N_HEADS))

PROJ_OUTS = (
    ("aq", 0, 256, BF16),
    ("acmp", 256, 128, F32),
    ("akv", 384, 256, BF16),
    ("ag", 640, 128, F32),
    ("z", 768, 1024, F32),
    ("bq", 1792, 256, BF16),
    ("bkv", 2048, 256, BF16),
    ("cq", 2304, 256, BF16),
    ("ck", 2560, 256, BF16),
    ("cv", 2816, 256, BF16),
    ("dq", 3072, 256, BF16),
    ("dk", 3328, 256, BF16),
    ("dv", 3584, 256, BF16),
)
PROJ_W = 3840
SWA_HEAD_ORDER = (0, 2, 1, 3)


def _dot(a, b):
    return jnp.dot(a, b, preferred_element_type=F32)


def _dot_t(a, b):
    return lax.dot_general(a, b, (((1,), (1,)), ((), ())), preferred_element_type=F32)


def _dot_split(a, b):
    hi = a.astype(BF16)
    lo = (a - hi.astype(F32)).astype(BF16)
    return _dot(hi, b) + _dot(lo, b)


def _sigmoid(v):
    return 1.0 / (1.0 + jnp.exp(-v))


def _silu(v):
    return v * _sigmoid(v)


def _params(*sem):
    return pltpu.CompilerParams(dimension_semantics=sem, vmem_limit_bytes=VMEM_LIMIT)


def _mod_kernel(c_ref, w_ref, b_ref, o_ref):
    cc = c_ref[...]
    o_ref[0] = _dot(_silu(cc).astype(BF16), w_ref[0].astype(BF16)) + b_ref[0]


def _modulation(c, w_ada, b_ada):
    depth, d, n = w_ada.shape
    b = c.shape[0]
    rows = 8
    cp = jnp.zeros((rows, d), F32).at[:b].set(c)
    tn = 512
    out = pl.pallas_call(
        _mod_kernel,
        grid=(depth, n // tn),
        in_specs=[
            pl.BlockSpec((rows, d), lambda l, j: (0, 0)),
            pl.BlockSpec((1, d, tn), lambda l, j: (l, 0, j)),
            pl.BlockSpec((1, 1, tn), lambda l, j: (l, 0, j)),
        ],
        out_specs=pl.BlockSpec((1, rows, tn), lambda l, j: (l, 0, j)),
        out_shape=jax.ShapeDtypeStruct((depth, rows, n), F32),
        compiler_params=_params("arbitrary", "arbitrary"),
        name="adaln_mod",
    )(cp, w_ada, b_ada.reshape(depth, 1, n))
    return out[:, :b].reshape(depth, b, 1, n)


def _modulated_norm(xf, g, mod, d):
    ms = jnp.mean(xf * xf, axis=-1, keepdims=True)
    y = xf * lax.rsqrt(ms + RMS_EPS) * g
    return y * (1.0 + mod[:, d:2 * d]) + mod[:, 0:d]


def _proj_kernel(x_ref, mod_ref, g_ref, w_ref, *out_refs, d):
    h = _modulated_norm(x_ref[0], g_ref[...], mod_ref[0], d).astype(BF16)
    for (_, start, width, dtype), o_ref in zip(PROJ_OUTS, out_refs):
        for c0 in range(0, width, 256):
            cw = min(256, width - c0)
            o_ref[0, :, c0:c0 + cw] = _dot(h, w_ref[:, start + c0:start + c0 + cw]).astype(dtype)


def _project(x, mod, g, w_p, tm):
    b, s, d = x.shape
    out_shapes = [jax.ShapeDtypeStruct((b, s, wd), dt) for (_, _, wd, dt) in PROJ_OUTS]
    out_specs = [pl.BlockSpec((1, tm, wd), lambda i, j: (i, j, 0)) for (_, _, wd, _) in PROJ_OUTS]
    outs = pl.pallas_call(
        functools.partial(_proj_kernel, d=d),
        grid=(b, s // tm),
        in_specs=[
            pl.BlockSpec((1, tm, d), lambda i, j: (i, j, 0)),
            pl.BlockSpec((1, 1, 3 * d), lambda i, j: (i, 0, 0)),
            pl.BlockSpec((1, d), lambda i, j: (0, 0)),
            pl.BlockSpec((d, PROJ_W), lambda i, j: (0, 0)),
        ],
        out_specs=out_specs,
        out_shape=out_shapes,
        compiler_params=_params("arbitrary", "arbitrary"),
        name="in_proj",
    )(x, mod, g.reshape(1, d), w_p)
    return {name: o for (name, _, _, _), o in zip(PROJ_OUTS, outs)}


def _compress_kernel(zk_ref, zv_ref, pos_ref, w1_ref, w2_ref, kc_ref, vc_ref):
    for j, (z_ref, o_ref) in enumerate(((zk_ref, kc_ref), (zv_ref, vc_ref))):
        zb = (z_ref[0] + pos_ref[j:j + 1, :]).astype(BF16)
        hid = _silu(_dot(zb, w1_ref[j])).astype(BF16)
        o_ref[0] = _dot(hid, w2_ref[j]).astype(BF16)


def _compress(zk, zv, pos, w1, w2p):
    b, nc, kd = zk.shape
    spec_z = pl.BlockSpec((1, nc, kd), lambda i: (i, 0, 0))
    spec_o = pl.BlockSpec((1, nc, 128), lambda i: (i, 0, 0))
    return pl.pallas_call(
        _compress_kernel,
        grid=(b,),
        in_specs=[
            spec_z, spec_z,
            pl.BlockSpec((2, kd), lambda i: (0, 0)),
            pl.BlockSpec((2, kd, HEAD_DIM), lambda i: (0, 0, 0)),
            pl.BlockSpec((2, HEAD_DIM, 128), lambda i: (0, 0, 0)),
        ],
        out_specs=[spec_o, spec_o],
        out_shape=[jax.ShapeDtypeStruct((b, nc, 128), BF16)] * 2,
        compiler_params=_params("arbitrary"),
        name="nsa_compress",
    )(zk, zv, pos, w1, w2p)


def _nsa_kernel(aq_ref, ag_ref, kc_ref, vc_ref, ksel_ref, vsel_ref, kwin_ref, vwin_ref, ya_ref,
                qaug_ref, mx_ref, sm_ref, acc_ref, owin_ref, kmax_ref, *, tq, seq):
    qi = pl.program_id(1)
    qstart = qi * tq
    nc = seq // CMP_BLOCK
    half = nc // 2
    ns = seq // SEL_BLOCK
    t_lane = lax.broadcasted_iota(jnp.int32, (1, tq), 1) + qstart
    lane128 = lax.broadcasted_iota(jnp.int32, (1, 128), 1)

    @pl.when(qi == 0)
    def _():
        def chunk(c, best):
            kk = ksel_ref[0, pl.ds(pl.multiple_of(c * NSA_TK, NSA_TK), NSA_TK), 0:128].astype(F32)
            sq = jnp.sum(jnp.where(lane128 < HEAD_DIM, kk * kk, 0.0), axis=-1, keepdims=True)
            return jnp.maximum(best, jnp.max(sq, axis=0, keepdims=True))
        best = lax.fori_loop(0, seq // NSA_TK, chunk, jnp.zeros((1, 1), F32))
        kmax_ref[0] = jnp.sqrt(best)[0, 0]

    row_c = lax.broadcasted_iota(jnp.int32, (nc, 1), 0)
    blk_c = jnp.where(row_c < half, 2 * row_c, 2 * (row_c - half) + 1)
    dist = t_lane - (blk_c * CMP_BLOCK + (CMP_BLOCK - 1))
    dist_f = dist.astype(F32)
    mask_c = dist >= 0
    kc = kc_ref[0]
    vc = vc_ref[0]
    psum = jnp.zeros((nc, tq), F32)
    ocmp = []
    for h in range(N_HEADS):
        grp = aq_ref[0, :, 128 * (h // 2):128 * (h // 2) + 128].astype(F32)
        if h % 2 == 1:
            grp = pltpu.roll(grp, 64, 1)
        left = jnp.where(lane128 < HEAD_DIM, grp * QK_SCALE,
                         jnp.where(lane128 < HEAD_DIM + 2, SLOPES[h], 0.0)).astype(BF16)
        qaug_ref[h * tq:(h + 1) * tq, 0:128] = left
        s = _dot_t(kc, left) - SLOPES[h] * dist_f
        s = jnp.where(mask_c, s, NEG_INF)
        m = jnp.max(s, axis=0, keepdims=True)
        e = jnp.where(mask_c, jnp.exp(s - m), 0.0)
        den = jnp.sum(e, axis=0, keepdims=True)
        p = e * (1.0 / jnp.maximum(den, TINY))
        psum = psum + p
        ocmp.append(lax.dot_general(p.astype(BF16), vc, (((0,), (0,)), ((), ())),
                                    preferred_element_type=F32))

    rows = N_HEADS * tq
    t_row = (lax.broadcasted_iota(jnp.int32, (rows, 1), 0) & (tq - 1)) + qstart
    n_win = NSA_WINDOW + tq
    w0 = pl.multiple_of(jnp.clip(qstart - NSA_WINDOW, 0, seq - n_win), tq)
    rel_w = t_row - (lax.broadcasted_iota(jnp.int32, (1, n_win), 1) + w0)
    sw = _dot_t(qaug_ref[:, 0:128], kwin_ref[0, pl.ds(w0, n_win), :])
    sw = jnp.where((rel_w >= 0) & (rel_w < NSA_WINDOW), sw, NEG_INF)
    ew = jnp.exp(sw - jnp.max(sw, axis=-1, keepdims=True))
    owin_ref[...] = (_dot(ew.astype(BF16), vwin_ref[0, pl.ds(w0, n_win), :])
                     * (1.0 / jnp.sum(ew, axis=-1, keepdims=True)))

    imp = psum[:half] + psum[half:]
    blk = lax.broadcasted_iota(jnp.int32, (ns, 1), 0)
    cur = t_lane >> 6
    future = blk * SEL_BLOCK > t_lane
    forced = (blk == 0) | (blk == cur) | (blk == cur - 1)
    score = jnp.where(forced, FORCED_SCORE, jnp.where(future, -1.0, imp))
    blk_f = blk.astype(F32)
    sel_t = jnp.zeros((ns, tq), F32)
    for _ in range(min(SEL_TOPK, ns)):
        mx = jnp.max(score, axis=0, keepdims=True)
        first = jnp.min(jnp.where(score == mx, blk_f, float(ns)), axis=0, keepdims=True)
        pick = blk_f == first
        sel_t = jnp.where(pick, 1.0, sel_t)
        score = jnp.where(pick, -jnp.inf, score)
    sel_f = jnp.transpose(sel_t)
    negmask = jnp.where(sel_f > 0.5, 0.0, NEG_INF).astype(BF16)
    if ns < 128:
        negmask = jnp.concatenate([negmask, jnp.zeros((tq, 128 - ns), BF16)], axis=1)
    for h in range(N_HEADS):
        qaug_ref[h * tq:(h + 1) * tq, 128:256] = negmask

    if ns < 128:
        sel_f = jnp.concatenate([sel_f, jnp.zeros((tq, 128 - ns), F32)], axis=1)
    any_blk = jnp.max(sel_f, axis=0, keepdims=True)
    blocks_per_tile = NSA_TK // SEL_BLOCK
    shift = 1
    while shift < blocks_per_tile:
        any_blk = jnp.maximum(any_blk, pltpu.roll(any_blk, 128 - shift, 1))
        shift *= 2
    tile_of_lane = lane128 >> (blocks_per_tile.bit_length() - 1)
    first_of_tile = (lane128 & (blocks_per_tile - 1)) == 0
    pow2 = lax.bitcast_convert_type(((tile_of_lane & 15) + 127) << 23, F32)
    weighted = jnp.where(first_of_tile, any_blk * pow2, 0.0)
    bits_lo = jnp.sum(jnp.where(tile_of_lane < 16, weighted, 0.0), axis=1, keepdims=True).astype(jnp.int32)[0, 0]
    bits_hi = jnp.sum(jnp.where(tile_of_lane >= 16, weighted, 0.0), axis=1, keepdims=True).astype(jnp.int32)[0, 0]

    def tile_selected(t):
        return ((jnp.where(t < 16, bits_lo, bits_hi) >> (t & 15)) & 1) == 1

    col_k = lax.broadcasted_iota(jnp.int32, (1, NSA_TK), 1)
    t_last = qi // (NSA_TK // tq)
    groups = NSA_TK // 128

    def scores(t, diagonal):
        k0 = pl.multiple_of(t * NSA_TK, NSA_TK)
        sc = _dot_t(qaug_ref[...], ksel_ref[0, pl.ds(k0, NSA_TK), :])
        if diagonal:
            sc = jnp.where(col_k + k0 <= t_row, sc, NEG_INF)
        return sc

    def fold_max(t, diagonal):
        sc = scores(t, diagonal)
        m = mx_ref[...]
        for g in range(groups):
            m = jnp.maximum(m, sc[:, 128 * g:128 * (g + 1)])
        mx_ref[...] = m

    def accumulate(t, diagonal):
        sc = scores(t, diagonal)
        m = mx_ref[...]
        e = jnp.exp(sc - jnp.concatenate([m] * groups, axis=1))
        part = sm_ref[...]
        for g in range(groups):
            part = part + e[:, 128 * g:128 * (g + 1)]
        sm_ref[...] = part
        k0 = pl.multiple_of(t * NSA_TK, NSA_TK)
        acc_ref[...] += _dot(e.astype(BF16), vsel_ref[0, pl.ds(k0, NSA_TK), :])

    def over_tiles(fn):
        def body(t, carry):
            pl.when(tile_selected(t))(lambda: fn(t, False))
            return carry
        lax.fori_loop(0, t_last, body, 0)
        fn(t_last, True)

    qpart = qaug_ref[:, 0:128].astype(F32)
    qnorm = jnp.sqrt(jnp.sum(jnp.where(lane128 < HEAD_DIM, qpart * qpart, 0.0), axis=-1, keepdims=True))
    reach = qnorm * (kmax_ref[0] * BOUND_SLACK)
    bound_ok = jnp.max(reach, axis=0, keepdims=True)[0, 0] * 2.0 <= MAX_BOUND_GAP

    @pl.when(bound_ok)
    def _():
        head_of_row = lax.broadcasted_iota(jnp.int32, (rows, 1), 0) >> (tq.bit_length() - 1)
        slope = jnp.full((rows, 1), SLOPES[0], F32)
        for h in range(1, N_HEADS):
            slope = jnp.where(head_of_row == h, SLOPES[h], slope)
        mx_ref[...] = jnp.broadcast_to(reach + slope * t_row.astype(F32), mx_ref.shape)

    @pl.when(jnp.logical_not(bound_ok))
    def _():
        mx_ref[...] = jnp.full(mx_ref.shape, NEG_INF, F32)
        over_tiles(fold_max)
        mx_ref[...] = jnp.broadcast_to(jnp.max(mx_ref[...], axis=-1, keepdims=True), mx_ref.shape)

    sm_ref[...] = jnp.zeros(sm_ref.shape, F32)
    acc_ref[...] = jnp.zeros(acc_ref.shape, F32)
    over_tiles(accumulate)
    osel = acc_ref[...] * (1.0 / jnp.sum(sm_ref[...], axis=-1, keepdims=True))

    gate = _sigmoid(ag_ref[0])
    comb = []
    for h in range(N_HEADS):
        head_rows = slice(h * tq, (h + 1) * tq)
        comb.append(gate[:, 3 * h:3 * h + 1] * ocmp[h]
                    + gate[:, 3 * h + 1:3 * h + 2] * osel[head_rows, :]
                    + gate[:, 3 * h + 2:3 * h + 3] * owin_ref[head_rows, :])
    for g in range(2):
        ya_ref[0, :, 128 * g:128 * (g + 1)] = jnp.where(
            lane128 < HEAD_DIM, pltpu.roll(comb[2 * g], 64, 1), comb[2 * g + 1])


def _nsa(aq, ag, kc, vc, ksel, akv, kwin, tq):
    b, s, _ = aq.shape
    nc = kc.shape[1]
    rows = N_HEADS * tq
    whole = lambda w, blk: pl.BlockSpec((1, s, w), lambda i, j, blk=blk: (i, 0, blk))
    return pl.pallas_call(
        functools.partial(_nsa_kernel, tq=tq, seq=s),
        grid=(b, s // tq),
        in_specs=[
            pl.BlockSpec((1, tq, 256), lambda i, j: (i, j, 0)),
            pl.BlockSpec((1, tq, 128), lambda i, j: (i, j, 0)),
            pl.BlockSpec((1, nc, 128), lambda i, j: (i, 0, 0)),
            pl.BlockSpec((1, nc, 128), lambda i, j: (i, 0, 0)),
            whole(256, 0),
            whole(128, 0),
            whole(128, 0),
            whole(128, 1),
        ],
        out_specs=pl.BlockSpec((1, tq, 256), lambda i, j: (i, j, 0)),
        out_shape=jax.ShapeDtypeStruct((b, s, 256), F32),
        scratch_shapes=[
            pltpu.VMEM((rows, 256), BF16),
            pltpu.VMEM((rows, 128), F32),
            pltpu.VMEM((rows, 128), F32),
            pltpu.VMEM((rows, 128), F32),
            pltpu.VMEM((rows, 128), F32),
            pltpu.SMEM((1,), F32),
        ],
        compiler_params=_params("arbitrary", "arbitrary"),
        name="nsa_attention",
    )(aq, ag, kc, vc, ksel, akv, kwin, akv)


def _swa_kernel(sink_ref, bq_ref, bkv_ref, yb_ref, *, tq, seq):
    qi = pl.program_id(1)
    qstart = qi * tq
    nk = 2 * tq
    k0 = pl.multiple_of(jnp.maximum(qstart - tq, 0), tq)
    kk = bkv_ref[0, pl.ds(k0, nk), 0:128]
    vv = bkv_ref[0, pl.ds(k0, nk), 128:256]
    t_col = lax.broadcasted_iota(jnp.int32, (tq, 1), 0) + qstart
    s_row = lax.broadcasted_iota(jnp.int32, (1, nk), 1) + k0
    rel = t_col - s_row
    rel_f = rel.astype(F32)
    mask = (rel >= 0) & (rel < SWA_WINDOW)
    lane128 = lax.broadcasted_iota(jnp.int32, (1, 128), 1)
    for g in range(2):
        qg = bq_ref[0, :, 128 * g:128 * (g + 1)]
        outs = []
        for p in range(2):
            h = SWA_HEAD_ORDER[2 * g + p]
            in_half = (lane128 >= HEAD_DIM * p) & (lane128 < HEAD_DIM * (p + 1))
            qh = jnp.where(in_half, qg, jnp.zeros_like(qg))
            s = _dot_t(qh, kk) * QK_SCALE - SLOPES[h] * rel_f
            s = jnp.where(mask, s, NEG_INF)
            sink = sink_ref[h]
            m = jnp.maximum(jnp.max(s, axis=-1, keepdims=True), sink)
            e = jnp.where(mask, jnp.exp(s - m), 0.0)
            den = jnp.sum(e, axis=-1, keepdims=True) + jnp.exp(sink - m)
            pr = e / jnp.maximum(den, TINY)
            outs.append(_dot(pr.astype(BF16), vv))
        yb_ref[0, :, 128 * g:128 * (g + 1)] = jnp.where(lane128 < HEAD_DIM, outs[0], outs[1])


def _swa(sink, bq, bkv, tq):
    b, s, _ = bq.shape
    return pl.pallas_call(
        functools.partial(_swa_kernel, tq=tq, seq=s),
        grid=(b, s // tq),
        in_specs=[
            pl.BlockSpec(memory_space=pltpu.SMEM),
            pl.BlockSpec((1, tq, 256), lambda i, j: (i, j, 0)),
            pl.BlockSpec((1, s, 256), lambda i, j: (i, 0, 0)),
        ],
        out_specs=pl.BlockSpec((1, tq, 256), lambda i, j: (i, j, 0)),
        out_shape=jax.ShapeDtypeStruct((b, s, 256), F32),
        compiler_params=_params("arbitrary", "arbitrary"),
        name="swa_attention",
    )(sink, bq, bkv)


def _stick_kernel(cq_ref, ck_ref, cv_ref, yc_ref, qm_ref, carry_ref, o_ref, *, tq, seq):
    qi = pl.program_id(1)
    qstart = qi * tq
    lane256 = lax.broadcasted_iota(jnp.int32, (1, 256), 1)
    head_of_lane = lane256 >> 6
    rows = N_HEADS * tq
    q = cq_ref[0].astype(F32) * QK_SCALE
    for h in range(N_HEADS):
        qm_ref[h * tq:(h + 1) * tq, :] = jnp.where(head_of_lane == h, q, 0.0).astype(BF16)
    jj = lax.broadcasted_iota(jnp.int32, (tq, 2 * tq), 0)
    ss = lax.broadcasted_iota(jnp.int32, (tq, 2 * tq), 1)
    upper = ((jj > ss) | (ss >= tq)).astype(BF16)
    t_in = lax.broadcasted_iota(jnp.int32, (rows, tq), 0) & (tq - 1)
    strict = lax.broadcasted_iota(jnp.int32, (rows, tq), 1) < t_in

    def tile(k0, carry, acc, valid):
        diagonal = carry is None
        kt = ck_ref[0, pl.ds(k0, tq), :]
        vt = cv_ref[0, pl.ds(k0, tq), :]
        z = _dot_t(qm_ref[...], kt)
        soft = jnp.log1p(jnp.exp(-jnp.abs(z)))
        log_beta = jnp.minimum(z, 0.0) - soft
        log_1m = log_beta - z
        if diagonal:
            log_1m = jnp.where(strict, log_1m, 0.0)
        sums = _dot_split(log_1m, upper)
        suffix = sums[:, :tq] if diagonal else sums[:, :tq] + carry
        a = jnp.exp(log_beta + suffix)
        if diagonal:
            a = jnp.where(strict, a, 0.0)
        if valid is not None:
            a = jnp.where(valid, a, 0.0)
        pv = _dot(a.astype(BF16), vt)
        for h in range(N_HEADS):
            acc = acc + jnp.where(head_of_lane == h, pv[h * tq:(h + 1) * tq, :], 0.0)
        return (sums[:, tq:] if diagonal else carry + sums[:, tq:]), acc

    carry, acc = tile(pl.multiple_of(qstart, tq), None, jnp.zeros((tq, 256), F32), None)
    for d in range(1, STICK_EAGER_TILES + 1):
        j = qi - d
        carry, acc = tile(pl.multiple_of(jnp.maximum(j, 0) * tq, tq), carry, acc, j >= 0)
    carry_ref[...] = carry
    o_ref[...] = acc

    def cond(state):
        j, worst = state
        return (j >= 0) & (worst > STICK_EXIT)

    def body(state):
        j, _ = state
        new_carry, new_acc = tile(pl.multiple_of(j * tq, tq), carry_ref[...], o_ref[...], None)
        carry_ref[...] = new_carry
        o_ref[...] = new_acc
        return j - 1, jnp.max(new_carry, axis=0, keepdims=True)[0, 0]

    lax.while_loop(cond, body, (qi - 1 - STICK_EAGER_TILES, jnp.max(carry, axis=0, keepdims=True)[0, 0]))
    yc_ref[0] = o_ref[...]


def _stick(cq, ck, cv, tq):
    b, s, _ = cq.shape
    return pl.pallas_call(
        functools.partial(_stick_kernel, tq=tq, seq=s),
        grid=(b, s // tq),
        in_specs=[
            pl.BlockSpec((1, tq, 256), lambda i, j: (i, j, 0)),
            pl.BlockSpec((1, s, 256), lambda i, j: (i, 0, 0)),
            pl.BlockSpec((1, s, 256), lambda i, j: (i, 0, 0)),
        ],
        out_specs=pl.BlockSpec((1, tq, 256), lambda i, j: (i, j, 0)),
        out_shape=jax.ShapeDtypeStruct((b, s, 256), F32),
        scratch_shapes=[
            pltpu.VMEM((N_HEADS * tq, 256), BF16),
            pltpu.VMEM((N_HEADS * tq, tq), F32),
            pltpu.VMEM((tq, 256), F32),
        ],
        compiler_params=_params("arbitrary", "arbitrary"),
        name="stick_breaking",
    )(cq, ck, cv)


def _ret_kernel(dq_ref, dk_ref, dv_ref, dmat_ref, xi_ref, zeta_ref, decay_ref, bd_ref, yd_ref, r_ref):
    n = pl.program_id(1)

    @pl.when(n == 0)
    def _():
        r_ref[...] = jnp.zeros(r_ref.shape, F32)

    c = RET_CHUNK
    lane256 = lax.broadcasted_iota(jnp.int32, (1, 256), 1)
    head_of_lane = lane256 >> 6
    q = dq_ref[0].astype(F32) * QK_SCALE
    k = dk_ref[0]
    v = dv_ref[0]
    o = _dot(q.astype(BF16), r_ref[...].astype(BF16)) * xi_ref[...]
    for h in range(N_HEADS):
        qh = jnp.where(head_of_lane == h, q, 0.0).astype(BF16)
        s = _dot_t(qh, k) * dmat_ref[h]
        o = o + jnp.where(head_of_lane == h, _dot(s.astype(BF16), v), 0.0)
    mu = jnp.zeros((c, 256), F32)
    for h in range(N_HEADS):
        in_h = head_of_lane == h
        mu = mu + jnp.where(in_h, jnp.sum(jnp.where(in_h, o, 0.0), axis=-1, keepdims=True), 0.0)
    cen = o - mu * (1.0 / HEAD_DIM)
    var = jnp.zeros((c, 256), F32)
    for h in range(N_HEADS):
        in_h = head_of_lane == h
        var = var + jnp.where(in_h, jnp.sum(jnp.where(in_h, cen * cen, 0.0), axis=-1, keepdims=True), 0.0)
    yd_ref[0] = cen * lax.rsqrt(var * (1.0 / HEAD_DIM) + LN_EPS)
    kz = (k.astype(F32) * zeta_ref[...]).astype(BF16)
    upd = lax.dot_general(kz, v, (((0,), (0,)), ((), ())), preferred_element_type=F32)
    r_ref[...] = r_ref[...] * decay_ref[...] + upd * bd_ref[...]


def _retention(dq, dk, dv, consts):
    b, s, _ = dq.shape
    c = RET_CHUNK
    dmat, xi, zeta, decay, bd = consts
    tile = pl.BlockSpec((1, c, 256), lambda i, j: (i, j, 0))
    return pl.pallas_call(
        _ret_kernel,
        grid=(b, s // c),
        in_specs=[
            tile, tile, tile,
            pl.BlockSpec((N_HEADS, c, c), lambda i, j: (0, 0, 0)),
            pl.BlockSpec((c, 256), lambda i, j: (0, 0)),
            pl.BlockSpec((c, 256), lambda i, j: (0, 0)),
            pl.BlockSpec((256, 256), lambda i, j: (0, 0)),
            pl.BlockSpec((256, 256), lambda i, j: (0, 0)),
        ],
        out_specs=tile,
        out_shape=jax.ShapeDtypeStruct((b, s, 256), F32),
        scratch_shapes=[pltpu.VMEM((256, 256), F32)],
        compiler_params=_params("arbitrary", "arbitrary"),
        name="retention",
    )(dq, dk, dv, dmat, xi, zeta, decay, bd)


def _retention_consts():
    c = RET_CHUNK
    log_g = jnp.log(1.0 - jnp.asarray(2.0 ** (-5.0 - np.arange(N_HEADS)), dtype=F32))
    i = jnp.arange(c)
    diff = (i[:, None] - i[None, :]).astype(F32)
    dmat = jnp.where(diff >= 0, jnp.exp(log_g[:, None, None] * jnp.maximum(diff, 0.0)), 0.0)
    zeta = jnp.exp(log_g[:, None] * (c - 1 - i)[None, :].astype(F32))
    xi = jnp.exp(log_g[:, None] * (i + 1)[None, :].astype(F32))
    g_chunk = jnp.exp(log_g * c)
    lanes = lambda hc: jnp.repeat(hc.T, HEAD_DIM, axis=1)
    head = np.arange(256) // HEAD_DIM
    bd = jnp.asarray(head[:, None] == head[None, :], F32)
    decay = jnp.repeat(g_chunk, HEAD_DIM)[:, None] * jnp.ones((1, 256), F32)
    return dmat, lanes(xi), lanes(zeta), decay, bd


def _merge_kernel(x_ref, mod_ref, g_ref, ya_ref, yb_ref, yc_ref, yd_ref, z_ref,
                  wm_ref, wbr_ref, wout_ref, o_ref, *, d):
    xf = x_ref[0]
    mod = mod_ref[0]
    h = _modulated_norm(xf, g_ref[...], mod, d).astype(BF16)
    merged = None
    for i, y_ref in enumerate((ya_ref, yb_ref, yc_ref, yd_ref)):
        yi = (y_ref[0] * _silu(z_ref[0, :, 256 * i:256 * (i + 1)])).astype(BF16)
        term = _sigmoid(_dot(h, wm_ref[i])) * _dot(yi, wbr_ref[i])
        merged = term if merged is None else merged + term
    o_ref[0] = xf + mod[:, 2 * d:3 * d] * _dot(merged.astype(BF16), wout_ref[...])


def _merge(x, mod, g, ys, z, wm, wbr, wout, tm):
    b, s, d = x.shape
    ytile = pl.BlockSpec((1, tm, 256), lambda i, j: (i, j, 0))
    return pl.pallas_call(
        functools.partial(_merge_kernel, d=d),
        grid=(b, s // tm),
        in_specs=[
            pl.BlockSpec((1, tm, d), lambda i, j: (i, j, 0)),
            pl.BlockSpec((1, 1, 3 * d), lambda i, j: (i, 0, 0)),
            pl.BlockSpec((1, d), lambda i, j: (0, 0)),
            ytile, ytile, ytile, ytile,
            pl.BlockSpec((1, tm, 4 * 256), lambda i, j: (i, j, 0)),
            pl.BlockSpec((4, d, d), lambda i, j: (0, 0, 0)),
            pl.BlockSpec((4, 256, d), lambda i, j: (0, 0, 0)),
            pl.BlockSpec((d, d), lambda i, j: (0, 0)),
        ],
        out_specs=pl.BlockSpec((1, tm, d), lambda i, j: (i, j, 0)),
        out_shape=jax.ShapeDtypeStruct((b, s, d), F32),
        compiler_params=_params("arbitrary", "arbitrary"),
        name="merge_out",
    )(x, mod, g.reshape(1, d), *ys, z, wm, wbr, wout)


def _final_norm_kernel(x_ref, g_ref, o_ref):
    xf = x_ref[0]
    ms = jnp.mean(xf * xf, axis=-1, keepdims=True)
    o_ref[0] = xf * lax.rsqrt(ms + RMS_EPS) * g_ref[...]


def _final_norm(x, g, tm):
    b, s, d = x.shape
    return pl.pallas_call(
        _final_norm_kernel,
        grid=(b, s // tm),
        in_specs=[pl.BlockSpec((1, tm, d), lambda i, j: (i, j, 0)),
                  pl.BlockSpec((1, d), lambda i, j: (0, 0))],
        out_specs=pl.BlockSpec((1, tm, d), lambda i, j: (i, j, 0)),
        out_shape=jax.ShapeDtypeStruct((b, s, d), F32),
        compiler_params=_params("arbitrary", "arbitrary"),
        name="final_norm",
    )(x, g.reshape(1, d))


def _proj_columns():
    a, bb, cc, dd = 0, 908, 1676, 2700
    perm = np.concatenate([np.arange(64) + 64 * h for h in SWA_HEAD_ORDER])
    pad = lambda n: np.full((n,), -1)
    cols = np.concatenate([
        np.arange(a, a + 256),
        np.arange(a + 256, a + 384),
        np.arange(a + 384, a + 640),
        np.arange(a + 640, a + 652), pad(116),
        np.arange(a + 652, a + 908),
        bb + 512 + perm,
        np.arange(cc + 768, cc + 1024),
        np.arange(dd + 768, dd + 1024),
        bb + perm,
        np.arange(bb + 256, bb + 512),
        np.arange(cc, cc + 768),
        np.arange(dd, dd + 768),
    ])
    assert cols.shape[0] == PROJ_W
    return cols, perm


def _position_features(s):
    pos = np.arange(s)
    feat = np.zeros((s, 64), np.float32)
    feat[:, 0] = SEL_BLOCK * (pos // SEL_BLOCK)
    feat[:, 1] = pos % SEL_BLOCK
    onehot = (pos[:, None] // SEL_BLOCK == np.arange(128)[None, :]).astype(np.float32)
    return jnp.asarray(feat, BF16), jnp.asarray(onehot, BF16)


def kernel(x, c, w_ada, b_ada, norm_g, w_in, cmp_pos, cmp_w1, cmp_w2, sink, w_merge, w_br, w_out, final_g):
    b, s, d = x.shape
    depth = w_ada.shape[0]
    in_w = w_in.shape[2]
    nc = s // CMP_BLOCK
    tm = min(ROW_TILE, s)

    cols, perm = _proj_columns()
    w_in_z = jnp.concatenate([w_in, jnp.zeros((depth, d, 1), F32)], axis=2)
    w_in_p = jnp.take(w_in_z, jnp.asarray(np.where(cols < 0, in_w, cols)), axis=2).astype(BF16)
    wm_b = w_merge.astype(BF16)
    wbr_b = w_br.at[:, 1].set(w_br[:, 1][:, perm, :]).astype(BF16)
    wout_b = w_out.astype(BF16)
    w1_b = cmp_w1.astype(BF16)
    zeros64 = jnp.zeros((depth, HEAD_DIM, HEAD_DIM), F32)
    w2_p = jnp.stack([jnp.concatenate([cmp_w2[:, 0], zeros64], axis=2),
                      jnp.concatenate([zeros64, cmp_w2[:, 1]], axis=2)], axis=1).astype(BF16)
    pos_flat = cmp_pos.reshape(depth, 2, CMP_BLOCK * HEAD_DIM)
    feat, onehot = _position_features(s)
    feat_b = jnp.broadcast_to(feat[None], (b, s, 64))
    onehot_b = jnp.broadcast_to(onehot[None], (b, s, 128))
    ret_consts = _retention_consts()
    mods = _modulation(c, w_ada, b_ada)

    def even_odd_blocks(zc):
        zb = zc.reshape(b, nc // 2, 2, CMP_BLOCK * HEAD_DIM)
        return zb.swapaxes(1, 2).reshape(b, nc, CMP_BLOCK * HEAD_DIM)

    for l in range(depth):
        p = _project(x, mods[l], norm_g[l], w_in_p[l], tm)
        kc, vc = _compress(even_odd_blocks(p["acmp"][..., :64]), even_odd_blocks(p["acmp"][..., 64:]),
                           pos_flat[l], w1_b[l], w2_p[l])
        akv = p["akv"]
        ksel = jnp.concatenate([akv[..., 0:64], feat_b, onehot_b], axis=-1)
        kwin = jnp.concatenate([akv[..., 128:192], feat_b], axis=-1)
        y_a = _nsa(p["aq"], p["ag"], kc, vc, ksel, akv, kwin, NSA_TQ)
        y_b = _swa(sink[l], p["bq"], p["bkv"], SWA_TQ)
        y_c = _stick(p["cq"], p["ck"], p["cv"], STICK_TQ)
        y_d = _retention(p["dq"], p["dk"], p["dv"], ret_consts)
        x = _merge(x, mods[l], norm_g[l], (y_a, y_b, y_c, y_d), p["z"], wm_b[l], wbr_b[l], wout_b[l], tm)
    return _final_norm(x, final_g, tm)
```

```python
import functools

import numpy as np
import jax
import jax.numpy as jnp
from jax import lax
from jax.experimental import pallas as pl
from jax.experimental.pallas import tpu as pltpu

F32 = jnp.float32
BF16 = jnp.bfloat16

HEAD_DIM = 64
N_HEADS = 4
BRANCH_W = N_HEADS * HEAD_DIM
CMP_BLOCK = 32
SEL_BLOCK = 64
SEL_TOPK = 16
NSA_WINDOW = 512
NSA_TK = 512
SWA_WINDOW = 128
RET_CHUNK = 128
RMS_EPS = 1e-6
LN_EPS = 1e-5
NEG_INF = -1e30
TINY = 1e-30
FORCED_SCORE = 1e4
QK_SCALE = HEAD_DIM ** -0.5
STICK_EXIT = -110.0
STICK_EAGER_TILES = 2
MAX_BOUND_GAP = 60.0
BOUND_SLACK = 1.001

ROW_TILE = 512
NSA_TQ = 256
SWA_TQ = 512
RET_CHUNKS_PER_STEP = 4
STICK_TQ = 128

VMEM_LIMIT = 56 * 1024 * 1024
SLOPES = tuple(float(2.0 ** (-8.0 * (h + 1) / N_HEADS)) for h in range(N_HEADS))

PROJ_OUTS = (
    ("aq", 0, 256, BF16),
    ("acmp", 256, 128, F32),
    ("akv", 384, 256, BF16),
    ("ag", 640, 128, F32),
    ("z", 768, 1024, F32),
    ("bq", 1792, 256, BF16),
    ("bkv", 2048, 256, BF16),
    ("cq", 2304, 256, BF16),
    ("ck", 2560, 256, BF16),
    ("cv", 2816, 256, BF16),
    ("dq", 3072, 256, BF16),
    ("dk", 3328, 256, BF16),
    ("dv", 3584, 256, BF16),
)
PROJ_W = 3840
SWA_HEAD_ORDER = (0, 2, 1, 3)


def _dot(a, b):
    return jnp.dot(a, b, preferred_element_type=F32)


def _dot_t(a, b):
    return lax.dot_general(a, b, (((1,), (1,)), ((), ())), preferred_element_type=F32)


def _dot_split(a, b):
    hi = a.astype(BF16)
    lo = (a - hi.astype(F32)).astype(BF16)
    return _dot(hi, b) + _dot(lo, b)


def _sigmoid(v):
    return 1.0 / (1.0 + jnp.exp(-v))


def _silu(v):
    return v * _sigmoid(v)


def _params(*sem):
    return pltpu.CompilerParams(dimension_semantics=sem, vmem_limit_bytes=VMEM_LIMIT)


def _mod_kernel(c_ref, w_ref, b_ref, o_ref):
    cc = c_ref[...]
    o_ref[0] = _dot(_silu(cc).astype(BF16), w_ref[0].astype(BF16)) + b_ref[0]


def _modulation(c, w_ada, b_ada):
    depth, d, n = w_ada.shape
    b = c.shape[0]
    rows = 8
    cp = jnp.zeros((rows, d), F32).at[:b].set(c)
    tn = 512
    out = pl.pallas_call(
        _mod_kernel,
        grid=(depth, n // tn),
        in_specs=[
            pl.BlockSpec((rows, d), lambda l, j: (0, 0)),
            pl.BlockSpec((1, d, tn), lambda l, j: (l, 0, j)),
            pl.BlockSpec((1, 1, tn), lambda l, j: (l, 0, j)),
        ],
        out_specs=pl.BlockSpec((1, rows, tn), lambda l, j: (l, 0, j)),
        out_shape=jax.ShapeDtypeStruct((depth, rows, n), F32),
        compiler_params=_params("arbitrary", "arbitrary"),
        name="adaln_mod",
    )(cp, w_ada, b_ada.reshape(depth, 1, n))
    return out[:, :b].reshape(depth, b, 1, n)


def _modulated_norm(xf, g, mod, d):
    ms = jnp.mean(xf * xf, axis=-1, keepdims=True)
    y = xf * lax.rsqrt(ms + RMS_EPS) * g
    return y * (1.0 + mod[:, d:2 * d]) + mod[:, 0:d]


def _proj_kernel(x_ref, mod_ref, g_ref, w_ref, *out_refs, d):
    h = _modulated_norm(x_ref[0], g_ref[...], mod_ref[0], d).astype(BF16)
    for (_, start, width, dtype), o_ref in zip(PROJ_OUTS, out_refs):
        for c0 in range(0, width, 256):
            cw = min(256, width - c0)
            o_ref[0, :, c0:c0 + cw] = _dot(h, w_ref[:, start + c0:start + c0 + cw]).astype(dtype)


def _project(x, mod, g, w_p, tm):
    b, s, d = x.shape
    out_shapes = [jax.ShapeDtypeStruct((b, s, wd), dt) for (_, _, wd, dt) in PROJ_OUTS]
    out_specs = [pl.BlockSpec((1, tm, wd), lambda i, j: (i, j, 0)) for (_, _, wd, _) in PROJ_OUTS]
    outs = pl.pallas_call(
        functools.partial(_proj_kernel, d=d),
        grid=(b, s // tm),
        in_specs=[
            pl.BlockSpec((1, tm, d), lambda i, j: (i, j, 0)),
            pl.BlockSpec((1, 1, 3 * d), lambda i, j: (i, 0, 0)),
            pl.BlockSpec((1, d), lambda i, j: (0, 0)),
            pl.BlockSpec((d, PROJ_W), lambda i, j: (0, 0)),
        ],
        out_specs=out_specs,
        out_shape=out_shapes,
        compiler_params=_params("arbitrary", "arbitrary"),
        name="in_proj",
    )(x, mod, g.reshape(1, d), w_p)
    return {name: o for (name, _, _, _), o in zip(PROJ_OUTS, outs)}


def _compress_kernel(zk_ref, zv_ref, pos_ref, w1_ref, w2_ref, kc_ref, vc_ref):
    for j, (z_ref, o_ref) in enumerate(((zk_ref, kc_ref), (zv_ref, vc_ref))):
        zb = (z_ref[0] + pos_ref[j:j + 1, :]).astype(BF16)
        hid = _silu(_dot(zb, w1_ref[j])).astype(BF16)
        o_ref[0] = _dot(hid, w2_ref[j]).astype(BF16)


def _compress(zk, zv, pos, w1, w2p):
    b, nc, kd = zk.shape
    spec_z = pl.BlockSpec((1, nc, kd), lambda i: (i, 0, 0))
    spec_o = pl.BlockSpec((1, nc, 128), lambda i: (i, 0, 0))
    return pl.pallas_call(
        _compress_kernel,
        grid=(b,),
        in_specs=[
            spec_z, spec_z,
            pl.BlockSpec((2, kd), lambda i: (0, 0)),
            pl.BlockSpec((2, kd, HEAD_DIM), lambda i: (0, 0, 0)),
            pl.BlockSpec((2, HEAD_DIM, 128), lambda i: (0, 0, 0)),
        ],
        out_specs=[spec_o, spec_o],
        out_shape=[jax.ShapeDtypeStruct((b, nc, 128), BF16)] * 2,
        compiler_params=_params("arbitrary"),
        name="nsa_compress",
    )(zk, zv, pos, w1, w2p)


def _nsa_kernel(aq_ref, ag_ref, kc_ref, vc_ref, ksel_ref, vsel_ref, kwin_ref, vwin_ref, ya_ref,
                qaug_ref, mx_ref, sm_ref, acc_ref, owin_ref, kmax_ref, *, tq, seq):
    qi = pl.program_id(1)
    qstart = qi * tq
    nc = seq // CMP_BLOCK
    half = nc // 2
    ns = seq // SEL_BLOCK
    t_lane = lax.broadcasted_iota(jnp.int32, (1, tq), 1) + qstart
    lane128 = lax.broadcasted_iota(jnp.int32, (1, 128), 1)

    @pl.when(qi == 0)
    def _():
        def chunk(c, best):
            kk = ksel_ref[0, pl.ds(pl.multiple_of(c * NSA_TK, NSA_TK), NSA_TK), 0:128].astype(F32)
            sq = jnp.sum(jnp.where(lane128 < HEAD_DIM, kk * kk, 0.0), axis=-1, keepdims=True)
            return jnp.maximum(best, jnp.max(sq, axis=0, keepdims=True))
        best = lax.fori_loop(0, seq // NSA_TK, chunk, jnp.zeros((1, 1), F32))
        kmax_ref[0] = jnp.sqrt(best)[0, 0]

    row_c = lax.broadcasted_iota(jnp.int32, (nc, 1), 0)
    blk_c = jnp.where(row_c < half, 2 * row_c, 2 * (row_c - half) + 1)
    dist = t_lane - (blk_c * CMP_BLOCK + (CMP_BLOCK - 1))
    dist_f = dist.astype(F32)
    mask_c = dist >= 0
    kc = kc_ref[0]
    vc = vc_ref[0]
    psum = jnp.zeros((nc, tq), F32)
    ocmp = []
    for h in range(N_HEADS):
        grp = aq_ref[0, :, 128 * (h // 2):128 * (h // 2) + 128].astype(F32)
        if h % 2 == 1:
            grp = pltpu.roll(grp, 64, 1)
        left = jnp.where(lane128 < HEAD_DIM, grp * QK_SCALE,
                         jnp.where(lane128 < HEAD_DIM + 2, SLOPES[h], 0.0)).astype(BF16)
        qaug_ref[h * tq:(h + 1) * tq, 0:128] = left
        s = _dot_t(kc, left) - SLOPES[h] * dist_f
        s = jnp.where(mask_c, s, NEG_INF)
        m = jnp.max(s, axis=0, keepdims=True)
        e = jnp.where(mask_c, jnp.exp(s - m), 0.0)
        den = jnp.sum(e, axis=0, keepdims=True)
        p = e * (1.0 / jnp.maximum(den, TINY))
        psum = psum + p
        ocmp.append(lax.dot_general(p.astype(BF16), vc, (((0,), (0,)), ((), ())),
                                    preferred_element_type=F32))

    rows = N_HEADS * tq
    t_row = (lax.broadcasted_iota(jnp.int32, (rows, 1), 0) & (tq - 1)) + qstart
    n_win = NSA_WINDOW + tq
    w0 = pl.multiple_of(jnp.clip(qstart - NSA_WINDOW, 0, seq - n_win), tq)
    rel_w = t_row - (lax.broadcasted_iota(jnp.int32, (1, n_win), 1) + w0)
    sw = _dot_t(qaug_ref[:, 0:128], kwin_ref[0, pl.ds(w0, n_win), :])
    sw = jnp.where((rel_w >= 0) & (rel_w < NSA_WINDOW), sw, NEG_INF)
    ew = jnp.exp(sw - jnp.max(sw, axis=-1, keepdims=True))
    owin_ref[...] = (_dot(ew.astype(BF16), vwin_ref[0, pl.ds(w0, n_win), :])
                     * (1.0 / jnp.sum(ew, axis=-1, keepdims=True)))

    imp = psum[:half] + psum[half:]
    blk = lax.broadcasted_iota(jnp.int32, (ns, 1), 0)
    cur = t_lane >> 6
    future = blk * SEL_BLOCK > t_lane
    forced = (blk == 0) | (blk == cur) | (blk == cur - 1)
    score = jnp.where(forced, FORCED_SCORE, jnp.where(future, -1.0, imp))
    blk_f = blk.astype(F32)
    sel_t = jnp.zeros((ns, tq), F32)
    for _ in range(min(SEL_TOPK, ns)):
        mx = jnp.max(score, axis=0, keepdims=True)
        first = jnp.min(jnp.where(score == mx, blk_f, float(ns)), axis=0, keepdims=True)
        pick = blk_f == first
        sel_t = jnp.where(pick, 1.0, sel_t)
        score = jnp.where(pick, -jnp.inf, score)
    sel_f = jnp.transpose(sel_t)
    negmask = jnp.where(sel_f > 0.5, 0.0, NEG_INF).astype(BF16)
    if ns < 128:
        negmask = jnp.concatenate([negmask, jnp.zeros((tq, 128 - ns), BF16)], axis=1)
    for h in range(N_HEADS):
        qaug_ref[h * tq:(h + 1) * tq, 128:256] = negmask

    if ns < 128:
        sel_f = jnp.concatenate([sel_f, jnp.zeros((tq, 128 - ns), F32)], axis=1)
    any_blk = jnp.max(sel_f, axis=0, keepdims=True)
    blocks_per_tile = NSA_TK // SEL_BLOCK
    shift = 1
    while shift < blocks_per_tile:
        any_blk = jnp.maximum(any_blk, pltpu.roll(any_blk, 128 - shift, 1))
        shift *= 2
    tile_of_lane = lane128 >> (blocks_per_tile.bit_length() - 1)
    first_of_tile = (lane128 & (blocks_per_tile - 1)) == 0
    pow2 = lax.bitcast_convert_type(((tile_of_lane & 15) + 127) << 23, F32)
    weighted = jnp.where(first_of_tile, any_blk * pow2, 0.0)
    bits_lo = jnp.sum(jnp.where(tile_of_lane < 16, weighted, 0.0), axis=1, keepdims=True).astype(jnp.int32)[0, 0]
    bits_hi = jnp.sum(jnp.where(tile_of_lane >= 16, weighted, 0.0), axis=1, keepdims=True).astype(jnp.int32)[0, 0]

    def tile_selected(t):
        return ((jnp.where(t < 16, bits_lo, bits_hi) >> (t & 15)) & 1) == 1

    col_k = lax.broadcasted_iota(jnp.int32, (1, NSA_TK), 1)
    t_last = qi // (NSA_TK // tq)
    groups = NSA_TK // 128

    def scores(t, diagonal):
        k0 = pl.multiple_of(t * NSA_TK, NSA_TK)
        sc = _dot_t(qaug_ref[...], ksel_ref[0, pl.ds(k0, NSA_TK), :])
        if diagonal:
            sc = jnp.where(col_k + k0 <= t_row, sc, NEG_INF)
        return sc

    def fold_max(t, diagonal):
        sc = scores(t, diagonal)
        m = mx_ref[...]
        for g in range(groups):
            m = jnp.maximum(m, sc[:, 128 * g:128 * (g + 1)])
        mx_ref[...] = m

    def accumulate(t, diagonal):
        sc = scores(t, diagonal)
        m = mx_ref[...]
        e = jnp.exp(sc - jnp.concatenate([m] * groups, axis=1))
        part = sm_ref[...]
        for g in range(groups):
            part = part + e[:, 128 * g:128 * (g + 1)]
        sm_ref[...] = part
        k0 = pl.multiple_of(t * NSA_TK, NSA_TK)
        acc_ref[...] += _dot(e.astype(BF16), vsel_ref[0, pl.ds(k0, NSA_TK), :])

    def over_tiles(fn):
        def body(t, carry):
            pl.when(tile_selected(t))(lambda: fn(t, False))
            return carry
        lax.fori_loop(0, t_last, body, 0)
        fn(t_last, True)

    qpart = qaug_ref[:, 0:128].astype(F32)
    qnorm = jnp.sqrt(jnp.sum(jnp.where(lane128 < HEAD_DIM, qpart * qpart, 0.0), axis=-1, keepdims=True))
    reach = qnorm * (kmax_ref[0] * BOUND_SLACK)
    bound_ok = jnp.max(reach, axis=0, keepdims=True)[0, 0] * 2.0 <= MAX_BOUND_GAP

    @pl.when(bound_ok)
    def _():
        head_of_row = lax.broadcasted_iota(jnp.int32, (rows, 1), 0) >> (tq.bit_length() - 1)
        slope = jnp.full((rows, 1), SLOPES[0], F32)
        for h in range(1, N_HEADS):
            slope = jnp.where(head_of_row == h, SLOPES[h], slope)
        mx_ref[...] = jnp.broadcast_to(reach + slope * t_row.astype(F32), mx_ref.shape)

    @pl.when(jnp.logical_not(bound_ok))
    def _():
        mx_ref[...] = jnp.full(mx_ref.shape, NEG_INF, F32)
        over_tiles(fold_max)
        mx_ref[...] = jnp.broadcast_to(jnp.max(mx_ref[...], axis=-1, keepdims=True), mx_ref.shape)

    sm_ref[...] = jnp.zeros(sm_ref.shape, F32)
    acc_ref[...] = jnp.zeros(acc_ref.shape, F32)
    over_tiles(accumulate)
    osel = acc_ref[...] * (1.0 / jnp.sum(sm_ref[...], axis=-1, keepdims=True))

    gate = _sigmoid(ag_ref[0])
    comb = []
    for h in range(N_HEADS):
        head_rows = slice(h * tq, (h + 1) * tq)
        comb.append(gate[:, 3 * h:3 * h + 1] * ocmp[h]
                    + gate[:, 3 * h + 1:3 * h + 2] * osel[head_rows, :]
                    + gate[:, 3 * h + 2:3 * h + 3] * owin_ref[head_rows, :])
    for g in range(2):
        ya_ref[0, :, 128 * g:128 * (g + 1)] = jnp.where(
            lane128 < HEAD_DIM, pltpu.roll(comb[2 * g], 64, 1), comb[2 * g + 1])


def _nsa(aq, ag, kc, vc, ksel, akv, kwin, tq):
    b, s, _ = aq.shape
    nc = kc.shape[1]
    rows = N_HEADS * tq
    whole = lambda w, blk: pl.BlockSpec((1, s, w), lambda i, j, blk=blk: (i, 0, blk))
    return pl.pallas_call(
        functools.partial(_nsa_kernel, tq=tq, seq=s),
        grid=(b, s // tq),
        in_specs=[
            pl.BlockSpec((1, tq, 256), lambda i, j: (i, j, 0)),
            pl.BlockSpec((1, tq, 128), lambda i, j: (i, j, 0)),
            pl.BlockSpec((1, nc, 128), lambda i, j: (i, 0, 0)),
            pl.BlockSpec((1, nc, 128), lambda i, j: (i, 0, 0)),
            whole(256, 0),
            whole(128, 0),
            whole(128, 0),
            whole(128, 1),
        ],
        out_specs=pl.BlockSpec((1, tq, 256), lambda i, j: (i, j, 0)),
        out_shape=jax.ShapeDtypeStruct((b, s, 256), F32),
        scratch_shapes=[
            pltpu.VMEM((rows, 256), BF16),
            pltpu.VMEM((rows, 128), F32),
            pltpu.VMEM((rows, 128), F32),
            pltpu.VMEM((rows, 128), F32),
            pltpu.VMEM((rows, 128), F32),
            pltpu.SMEM((1,), F32),
        ],
        compiler_params=_params("arbitrary", "arbitrary"),
        name="nsa_attention",
    )(aq, ag, kc, vc, ksel, akv, kwin, akv)


def _swa_kernel(sink_ref, bq_ref, bkv_ref, yb_ref, *, tq, seq):
    qi = pl.program_id(1)
    sub = SWA_WINDOW
    nk = 2 * sub
    lane128 = lax.broadcasted_iota(jnp.int32, (1, 128), 1)
    for u in range(tq // sub):
        qstart = qi * tq + u * sub
        k0 = pl.multiple_of(jnp.maximum(qstart - sub, 0), sub)
        kk = bkv_ref[0, pl.ds(k0, nk), 0:128]
        vv = bkv_ref[0, pl.ds(k0, nk), 128:256]
        t_lane = lax.broadcasted_iota(jnp.int32, (1, sub), 1) + qstart
        s_col = lax.broadcasted_iota(jnp.int32, (nk, 1), 0) + k0
        rel = t_lane - s_col
        rel_f = rel.astype(F32)
        mask = (rel >= 0) & (rel < SWA_WINDOW)
        for g in range(2):
            qg = bq_ref[0, u * sub:(u + 1) * sub, 128 * g:128 * (g + 1)]
            outs = []
            for p in range(2):
                h = SWA_HEAD_ORDER[2 * g + p]
                in_half = (lane128 >= HEAD_DIM * p) & (lane128 < HEAD_DIM * (p + 1))
                qh = jnp.where(in_half, qg, jnp.zeros_like(qg))
                s = _dot_t(kk, qh) * QK_SCALE - SLOPES[h] * rel_f
                s = jnp.where(mask, s, NEG_INF)
                sink = sink_ref[h]
                m = jnp.maximum(jnp.max(s, axis=0, keepdims=True), sink)
                e = jnp.where(mask, jnp.exp(s - m), 0.0)
                den = jnp.sum(e, axis=0, keepdims=True) + jnp.exp(sink - m)
                pr = e * (1.0 / jnp.maximum(den, TINY))
                outs.append(lax.dot_general(pr.astype(BF16), vv, (((0,), (0,)), ((), ())),
                                            preferred_element_type=F32))
            yb_ref[0, u * sub:(u + 1) * sub, 128 * g:128 * (g + 1)] = jnp.where(
                lane128 < HEAD_DIM, outs[0], outs[1])


def _swa(sink, bq, bkv, tq):
    b, s, _ = bq.shape
    return pl.pallas_call(
        functools.partial(_swa_kernel, tq=tq, seq=s),
        grid=(b, s // tq),
        in_specs=[
            pl.BlockSpec(memory_space=pltpu.SMEM),
            pl.BlockSpec((1, tq, 256), lambda i, j: (i, j, 0)),
            pl.BlockSpec((1, s, 256), lambda i, j: (i, 0, 0)),
        ],
        out_specs=pl.BlockSpec((1, tq, 256), lambda i, j: (i, j, 0)),
        out_shape=jax.ShapeDtypeStruct((b, s, 256), F32),
        compiler_params=_params("arbitrary", "arbitrary"),
        name="swa_attention",
    )(sink, bq, bkv)


def _stick_kernel(cq_ref, ck_ref, cv_ref, yc_ref, qm_ref, carry_ref, o_ref, *, tq, seq):
    qi = pl.program_id(1)
    qstart = qi * tq
    lane256 = lax.broadcasted_iota(jnp.int32, (1, 256), 1)
    head_of_lane = lane256 >> 6
    rows = N_HEADS * tq
    q = cq_ref[0].astype(F32) * QK_SCALE
    for h in range(N_HEADS):
        qm_ref[h * tq:(h + 1) * tq, :] = jnp.where(head_of_lane == h, q, 0.0).astype(BF16)
    jj = lax.broadcasted_iota(jnp.int32, (tq, 2 * tq), 0)
    ss = lax.broadcasted_iota(jnp.int32, (tq, 2 * tq), 1)
    upper = ((jj > ss) | (ss >= tq)).astype(BF16)
    t_in = lax.broadcasted_iota(jnp.int32, (rows, tq), 0) & (tq - 1)
    strict = lax.broadcasted_iota(jnp.int32, (rows, tq), 1) < t_in

    def tile(k0, carry, acc, valid):
        diagonal = carry is None
        kt = ck_ref[0, pl.ds(k0, tq), :]
        vt = cv_ref[0, pl.ds(k0, tq), :]
        z = _dot_t(qm_ref[...], kt)
        soft = jnp.log(1.0 + jnp.exp(-jnp.abs(z)))
        log_beta = jnp.minimum(z, 0.0) - soft
        log_1m = log_beta - z
        if diagonal:
            log_1m = jnp.where(strict, log_1m, 0.0)
        sums = _dot_split(log_1m, upper)
        suffix = sums[:, :tq] if diagonal else sums[:, :tq] + carry
        a = jnp.exp(log_beta + suffix)
        if diagonal:
            a = jnp.where(strict, a, 0.0)
        if valid is not None:
            a = jnp.where(valid, a, 0.0)
        a_b = a.astype(BF16)
        a_cat = jnp.concatenate([a_b[h * tq:(h + 1) * tq, :] for h in range(N_HEADS)], axis=1)
        v_bd = jnp.concatenate([jnp.where(head_of_lane == h, vt, jnp.zeros_like(vt))
                                for h in range(N_HEADS)], axis=0)
        acc = acc + _dot(a_cat, v_bd)
        return (sums[:, tq:] if diagonal else carry + sums[:, tq:]), acc

    carry, acc = tile(pl.multiple_of(qstart, tq), None, jnp.zeros((tq, 256), F32), None)
    for d in range(1, STICK_EAGER_TILES + 1):
        j = qi - d
        carry, acc = tile(pl.multiple_of(jnp.maximum(j, 0) * tq, tq), carry, acc, j >= 0)
    carry_ref[...] = carry
    o_ref[...] = acc

    def cond(state):
        j, worst = state
        return (j >= 0) & (worst > STICK_EXIT)

    def body(state):
        j, _ = state
        new_carry, new_acc = tile(pl.multiple_of(j * tq, tq), carry_ref[...], o_ref[...], None)
        carry_ref[...] = new_carry
        o_ref[...] = new_acc
        return j - 1, jnp.max(new_carry, axis=0, keepdims=True)[0, 0]

    lax.while_loop(cond, body, (qi - 1 - STICK_EAGER_TILES, jnp.max(carry, axis=0, keepdims=True)[0, 0]))
    yc_ref[0] = o_ref[...]


def _stick(cq, ck, cv, tq):
    b, s, _ = cq.shape
    return pl.pallas_call(
        functools.partial(_stick_kernel, tq=tq, seq=s),
        grid=(b, s // tq),
        in_specs=[
            pl.BlockSpec((1, tq, 256), lambda i, j: (i, j, 0)),
            pl.BlockSpec((1, s, 256), lambda i, j: (i, 0, 0)),
            pl.BlockSpec((1, s, 256), lambda i, j: (i, 0, 0)),
        ],
        out_specs=pl.BlockSpec((1, tq, 256), lambda i, j: (i, j, 0)),
        out_shape=jax.ShapeDtypeStruct((b, s, 256), F32),
        scratch_shapes=[
            pltpu.VMEM((N_HEADS * tq, 256), BF16),
            pltpu.VMEM((N_HEADS * tq, tq), F32),
            pltpu.VMEM((tq, 256), F32),
        ],
        compiler_params=_params("arbitrary", "arbitrary"),
        name="stick_breaking",
    )(cq, ck, cv)


def _ret_kernel(dq_ref, dk_ref, dv_ref, dmat_ref, xi_ref, zeta_ref, decay_ref, bd_ref, yd_ref, r_ref):
    n = pl.program_id(1)

    @pl.when(n == 0)
    def _():
        r_ref[...] = jnp.zeros(r_ref.shape, F32)

    c = RET_CHUNK
    lane256 = lax.broadcasted_iota(jnp.int32, (1, 256), 1)
    head_of_lane = lane256 >> 6
    r = r_ref[...]
    for u in range(dq_ref.shape[1] // c):
        rows = slice(u * c, (u + 1) * c)
        q = dq_ref[0, rows, :].astype(F32) * QK_SCALE
        k = dk_ref[0, rows, :]
        v = dv_ref[0, rows, :]
        o = _dot(q.astype(BF16), r.astype(BF16)) * xi_ref[...]
        for h in range(N_HEADS):
            qh = jnp.where(head_of_lane == h, q, 0.0).astype(BF16)
            s = _dot_t(qh, k) * dmat_ref[h]
            o = o + jnp.where(head_of_lane == h, _dot(s.astype(BF16), v), 0.0)
        mu = jnp.zeros((c, 256), F32)
        for h in range(N_HEADS):
            in_h = head_of_lane == h
            mu = mu + jnp.where(in_h, jnp.sum(jnp.where(in_h, o, 0.0), axis=-1, keepdims=True), 0.0)
        cen = o - mu * (1.0 / HEAD_DIM)
        var = jnp.zeros((c, 256), F32)
        for h in range(N_HEADS):
            in_h = head_of_lane == h
            var = var + jnp.where(in_h, jnp.sum(jnp.where(in_h, cen * cen, 0.0), axis=-1, keepdims=True), 0.0)
        yd_ref[0, rows, :] = cen * lax.rsqrt(var * (1.0 / HEAD_DIM) + LN_EPS)
        kz = (k.astype(F32) * zeta_ref[...]).astype(BF16)
        upd = lax.dot_general(kz, v, (((0,), (0,)), ((), ())), preferred_element_type=F32)
        r = r * decay_ref[...] + upd * bd_ref[...]
    r_ref[...] = r


def _retention(dq, dk, dv, consts):
    b, s, _ = dq.shape
    c = RET_CHUNK
    dmat, xi, zeta, decay, bd = consts
    step = RET_CHUNKS_PER_STEP * c
    tile = pl.BlockSpec((1, step, 256), lambda i, j: (i, j, 0))
    return pl.pallas_call(
        _ret_kernel,
        grid=(b, s // step),
        in_specs=[
            tile, tile, tile,
            pl.BlockSpec((N_HEADS, c, c), lambda i, j: (0, 0, 0)),
            pl.BlockSpec((c, 256), lambda i, j: (0, 0)),
            pl.BlockSpec((c, 256), lambda i, j: (0, 0)),
            pl.BlockSpec((256, 256), lambda i, j: (0, 0)),
            pl.BlockSpec((256, 256), lambda i, j: (0, 0)),
        ],
        out_specs=tile,
        out_shape=jax.ShapeDtypeStruct((b, s, 256), F32),
        scratch_shapes=[pltpu.VMEM((256, 256), F32)],
        compiler_params=_params("arbitrary", "arbitrary"),
        name="retention",
    )(dq, dk, dv, dmat, xi, zeta, decay, bd)


def _retention_consts():
    c = RET_CHUNK
    log_g = jnp.log(1.0 - jnp.asarray(2.0 ** (-5.0 - np.arange(N_HEADS)), dtype=F32))
    i = jnp.arange(c)
    diff = (i[:, None] - i[None, :]).astype(F32)
    dmat = jnp.where(diff >= 0, jnp.exp(log_g[:, None, None] * jnp.maximum(diff, 0.0)), 0.0)
    zeta = jnp.exp(log_g[:, None] * (c - 1 - i)[None, :].astype(F32))
    xi = jnp.exp(log_g[:, None] * (i + 1)[None, :].astype(F32))
    g_chunk = jnp.exp(log_g * c)
    lanes = lambda hc: jnp.repeat(hc.T, HEAD_DIM, axis=1)
    head = np.arange(256) // HEAD_DIM
    bd = jnp.asarray(head[:, None] == head[None, :], F32)
    decay = jnp.repeat(g_chunk, HEAD_DIM)[:, None] * jnp.ones((1, 256), F32)
    return dmat, lanes(xi), lanes(zeta), decay, bd


def _merge_kernel(x_ref, mod_ref, g_ref, ya_ref, yb_ref, yc_ref, yd_ref, z_ref,
                  wm_ref, wbr_ref, wout_ref, fg_ref, o_ref, *, d, final):
    xf = x_ref[0]
    mod = mod_ref[0]
    h = _modulated_norm(xf, g_ref[...], mod, d).astype(BF16)
    merged = None
    for i, y_ref in enumerate((ya_ref, yb_ref, yc_ref, yd_ref)):
        yi = (y_ref[0] * _silu(z_ref[0, :, 256 * i:256 * (i + 1)])).astype(BF16)
        term = _sigmoid(_dot(h, wm_ref[i])) * _dot(yi, wbr_ref[i])
        merged = term if merged is None else merged + term
    out = xf + mod[:, 2 * d:3 * d] * _dot(merged.astype(BF16), wout_ref[...])
    if final:
        out = out * lax.rsqrt(jnp.mean(out * out, axis=-1, keepdims=True) + RMS_EPS) * fg_ref[...]
    o_ref[0] = out


def _merge(x, mod, g, ys, z, wm, wbr, wout, final_g, final, tm):
    b, s, d = x.shape
    ytile = pl.BlockSpec((1, tm, 256), lambda i, j: (i, j, 0))
    return pl.pallas_call(
        functools.partial(_merge_kernel, d=d, final=final),
        grid=(b, s // tm),
        in_specs=[
            pl.BlockSpec((1, tm, d), lambda i, j: (i, j, 0)),
            pl.BlockSpec((1, 1, 3 * d), lambda i, j: (i, 0, 0)),
            pl.BlockSpec((1, d), lambda i, j: (0, 0)),
            ytile, ytile, ytile, ytile,
            pl.BlockSpec((1, tm, 4 * 256), lambda i, j: (i, j, 0)),
            pl.BlockSpec((4, d, d), lambda i, j: (0, 0, 0)),
            pl.BlockSpec((4, 256, d), lambda i, j: (0, 0, 0)),
            pl.BlockSpec((d, d), lambda i, j: (0, 0)),
            pl.BlockSpec((1, d), lambda i, j: (0, 0)),
        ],
        out_specs=pl.BlockSpec((1, tm, d), lambda i, j: (i, j, 0)),
        out_shape=jax.ShapeDtypeStruct((b, s, d), F32),
        compiler_params=_params("arbitrary", "arbitrary"),
        name="merge_out",
    )(x, mod, g.reshape(1, d), *ys, z, wm, wbr, wout, final_g.reshape(1, d))


def _proj_columns():
    a, bb, cc, dd = 0, 908, 1676, 2700
    perm = np.concatenate([np.arange(64) + 64 * h for h in SWA_HEAD_ORDER])
    pad = lambda n: np.full((n,), -1)
    cols = np.concatenate([
        np.arange(a, a + 256),
        np.arange(a + 256, a + 384),
        np.arange(a + 384, a + 640),
        np.arange(a + 640, a + 652), pad(116),
        np.arange(a + 652, a + 908),
        bb + 512 + perm,
        np.arange(cc + 768, cc + 1024),
        np.arange(dd + 768, dd + 1024),
        bb + perm,
        np.arange(bb + 256, bb + 512),
        np.arange(cc, cc + 768),
        np.arange(dd, dd + 768),
    ])
    assert cols.shape[0] == PROJ_W
    return cols, perm


def _take_columns(w, cols):
    pieces, i = [], 0
    while i < len(cols):
        j = i + 1
        if cols[i] < 0:
            while j < len(cols) and cols[j] < 0:
                j += 1
            pieces.append(jnp.zeros(w.shape[:-1] + (j - i,), w.dtype))
        else:
            while j < len(cols) and cols[j] == cols[j - 1] + 1:
                j += 1
            pieces.append(w[..., int(cols[i]):int(cols[j - 1]) + 1])
        i = j
    return jnp.concatenate(pieces, axis=-1)


def _position_features(s):
    pos = np.arange(s)
    feat = np.zeros((s, 64), np.float32)
    feat[:, 0] = SEL_BLOCK * (pos // SEL_BLOCK)
    feat[:, 1] = pos % SEL_BLOCK
    onehot = (pos[:, None] // SEL_BLOCK == np.arange(128)[None, :]).astype(np.float32)
    return jnp.asarray(feat, BF16), jnp.asarray(onehot, BF16)


def kernel(x, c, w_ada, b_ada, norm_g, w_in, cmp_pos, cmp_w1, cmp_w2, sink, w_merge, w_br, w_out, final_g):
    b, s, d = x.shape
    depth = w_ada.shape[0]
    nc = s // CMP_BLOCK
    tm = min(ROW_TILE, s)

    cols, perm = _proj_columns()
    w_in_p = _take_columns(w_in, cols).astype(BF16)
    wm_b = w_merge.astype(BF16)
    wbr_b = w_br.at[:, 1].set(w_br[:, 1][:, perm, :]).astype(BF16)
    wout_b = w_out.astype(BF16)
    w1_b = cmp_w1.astype(BF16)
    zeros64 = jnp.zeros((depth, HEAD_DIM, HEAD_DIM), F32)
    w2_p = jnp.stack([jnp.concatenate([cmp_w2[:, 0], zeros64], axis=2),
                      jnp.concatenate([zeros64, cmp_w2[:, 1]], axis=2)], axis=1).astype(BF16)
    pos_flat = cmp_pos.reshape(depth, 2, CMP_BLOCK * HEAD_DIM)
    feat, onehot = _position_features(s)
    feat_b = jnp.broadcast_to(feat[None], (b, s, 64))
    onehot_b = jnp.broadcast_to(onehot[None], (b, s, 128))
    ret_consts = _retention_consts()
    mods = _modulation(c, w_ada, b_ada)

    def even_odd_blocks(zc):
        zb = zc.reshape(b, nc // 2, 2, CMP_BLOCK * HEAD_DIM)
        return zb.swapaxes(1, 2).reshape(b, nc, CMP_BLOCK * HEAD_DIM)

    for l in range(depth):
        p = _project(x, mods[l], norm_g[l], w_in_p[l], tm)
        kc, vc = _compress(even_odd_blocks(p["acmp"][..., :64]), even_odd_blocks(p["acmp"][..., 64:]),
                           pos_flat[l], w1_b[l], w2_p[l])
        akv = p["akv"]
        ksel = jnp.concatenate([akv[..., 0:64], feat_b, onehot_b], axis=-1)
        kwin = jnp.concatenate([akv[..., 128:192], feat_b], axis=-1)
        y_a = _nsa(p["aq"], p["ag"], kc, vc, ksel, akv, kwin, NSA_TQ)
        y_b = _swa(sink[l], p["bq"], p["bkv"], SWA_TQ)
        y_c = _stick(p["cq"], p["ck"], p["cv"], STICK_TQ)
        y_d = _retention(p["dq"], p["dk"], p["dv"], ret_consts)
        x = _merge(x, mods[l], norm_g[l], (y_a, y_b, y_c, y_d), p["z"], wm_b[l], wbr_b[l], wout_b[l],
                   final_g, l == depth - 1, tm)
    return x
```

```python
import functools

import numpy as np
import jax
import jax.numpy as jnp
from jax import lax
from jax.experimental import pallas as pl
from jax.experimental.pallas import tpu as pltpu

F32 = jnp.float32
BF16 = jnp.bfloat16

HEAD_DIM = 64
N_HEADS = 4
BRANCH_W = N_HEADS * HEAD_DIM
CMP_BLOCK = 32
SEL_BLOCK = 64
SEL_TOPK = 16
NSA_WINDOW = 512
NSA_TK = 512
SWA_WINDOW = 128
RET_CHUNK = 128
RMS_EPS = 1e-6
LN_EPS = 1e-5
NEG_INF = -1e30
TINY = 1e-30
FORCED_SCORE = 1e4
QK_SCALE = HEAD_DIM ** -0.5
STICK_EXIT = -110.0
STICK_EAGER_TILES = 2
MAX_BOUND_GAP = 60.0
BOUND_SLACK = 1.001

ROW_TILE = 512
MERGE_ROW_PARTS = 2
NSA_TQ = 256
NSA_SUB = 128
SWA_TQ = 512
RET_CHUNKS_PER_STEP = 4
STICK_TQ = 128

VMEM_LIMIT = 56 * 1024 * 1024
SLOPES = tuple(float(2.0 ** (-8.0 * (h + 1) / N_HEADS)) for h in range(N_HEADS))

PROJ_OUTS = (
    ("aq", 0, 256, BF16),
    ("acmp", 256, 128, F32),
    ("akv", 384, 256, BF16),
    ("ag", 640, 128, F32),
    ("z", 768, 1024, F32),
    ("bq", 1792, 256, BF16),
    ("bkv", 2048, 256, BF16),
    ("cq", 2304, 256, BF16),
    ("ck", 2560, 256, BF16),
    ("cv", 2816, 256, BF16),
    ("dq", 3072, 256, BF16),
    ("dk", 3328, 256, BF16),
    ("dv", 3584, 256, BF16),
)
PROJ_W = 3840
SWA_HEAD_ORDER = (0, 2, 1, 3)


def _dot(a, b):
    return jnp.dot(a, b, preferred_element_type=F32)


def _dot_t(a, b):
    return lax.dot_general(a, b, (((1,), (1,)), ((), ())), preferred_element_type=F32)


def _dot_split(a, b):
    hi = a.astype(BF16)
    lo = (a - hi.astype(F32)).astype(BF16)
    return _dot(hi, b) + _dot(lo, b)


def _sigmoid(v):
    return 1.0 / (1.0 + jnp.exp(-v))


def _silu(v):
    return v * _sigmoid(v)


def _params(*sem):
    return pltpu.CompilerParams(dimension_semantics=sem, vmem_limit_bytes=VMEM_LIMIT)


def _mod_kernel(c_ref, w_ref, b_ref, o_ref):
    cc = c_ref[...]
    o_ref[0] = _dot(_silu(cc).astype(BF16), w_ref[0].astype(BF16)) + b_ref[0]


def _modulation(c, w_ada, b_ada):
    depth, d, n = w_ada.shape
    b = c.shape[0]
    rows = 8
    cp = jnp.zeros((rows, d), F32).at[:b].set(c)
    tn = 512
    out = pl.pallas_call(
        _mod_kernel,
        grid=(depth, n // tn),
        in_specs=[
            pl.BlockSpec((rows, d), lambda l, j: (0, 0)),
            pl.BlockSpec((1, d, tn), lambda l, j: (l, 0, j)),
            pl.BlockSpec((1, 1, tn), lambda l, j: (l, 0, j)),
        ],
        out_specs=pl.BlockSpec((1, rows, tn), lambda l, j: (l, 0, j)),
        out_shape=jax.ShapeDtypeStruct((depth, rows, n), F32),
        compiler_params=_params("arbitrary", "arbitrary"),
        name="adaln_mod",
    )(cp, w_ada, b_ada.reshape(depth, 1, n))
    return out[:, :b].reshape(depth, b, 1, n)


def _modulated_norm(xf, g, mod, d):
    ms = jnp.mean(xf * xf, axis=-1, keepdims=True)
    y = xf * lax.rsqrt(ms + RMS_EPS) * g
    return y * (1.0 + mod[:, d:2 * d]) + mod[:, 0:d]


def _proj_kernel(x_ref, mod_ref, g_ref, w_ref, kconst_ref, *out_refs, d):
    h = _modulated_norm(x_ref[0], g_ref[...], mod_ref[0], d).astype(BF16)
    ksel_ref, kwin_ref = out_refs[len(PROJ_OUTS):]
    for (name, start, width, dtype), o_ref in zip(PROJ_OUTS, out_refs):
        for c0 in range(0, width, 256):
            cw = min(256, width - c0)
            acc = _dot(h, w_ref[:, start + c0:start + c0 + cw])
            o_ref[0, :, c0:c0 + cw] = acc.astype(dtype)
            if name == "akv":
                ksel_ref[0] = kconst_ref[...]
                ksel_ref[0, :, 0:HEAD_DIM] = acc[:, 0:HEAD_DIM].astype(BF16)
                kwin_ref[0] = kconst_ref[:, 0:128]
                kwin_ref[0, :, 0:HEAD_DIM] = acc[:, 128:128 + HEAD_DIM].astype(BF16)


def _project(x, mod, g, w_p, kconst, tm):
    b, s, d = x.shape
    widths = [(wd, dt) for (_, _, wd, dt) in PROJ_OUTS] + [(256, BF16), (128, BF16)]
    out_shapes = [jax.ShapeDtypeStruct((b, s, wd), dt) for wd, dt in widths]
    out_specs = [pl.BlockSpec((1, tm, wd), lambda i, j: (i, j, 0)) for wd, _ in widths]
    outs = pl.pallas_call(
        functools.partial(_proj_kernel, d=d),
        grid=(b, s // tm),
        in_specs=[
            pl.BlockSpec((1, tm, d), lambda i, j: (i, j, 0)),
            pl.BlockSpec((1, 1, 3 * d), lambda i, j: (i, 0, 0)),
            pl.BlockSpec((1, d), lambda i, j: (0, 0)),
            pl.BlockSpec((d, PROJ_W), lambda i, j: (0, 0)),
            pl.BlockSpec((tm, 256), lambda i, j: (j, 0)),
        ],
        out_specs=out_specs,
        out_shape=out_shapes,
        compiler_params=_params("arbitrary", "arbitrary"),
        name="in_proj",
    )(x, mod, g.reshape(1, d), w_p, kconst)
    names = [name for (name, _, _, _) in PROJ_OUTS] + ["ksel", "kwin"]
    return dict(zip(names, outs))


def _compress_kernel(acmp_ref, pos_ref, w1_ref, w2_ref, kc_ref, vc_ref, *, nc):
    half = nc // 2
    for parity in range(2):
        hid = jnp.zeros((half, 128), F32)
        for i in range(CMP_BLOCK):
            z = acmp_ref[0, pl.ds(parity * CMP_BLOCK + i, half, stride=2 * CMP_BLOCK), :]
            hid = hid + _dot((z + pos_ref[i:i + 1, :]).astype(BF16), w1_ref[i])
        act = _silu(hid).astype(BF16)
        out_rows = slice(parity * half, (parity + 1) * half)
        kc_ref[0, out_rows, :] = _dot(act, w2_ref[0]).astype(BF16)
        vc_ref[0, out_rows, :] = _dot(act, w2_ref[1]).astype(BF16)


def _compress(acmp, pos, w1bd, w2bd):
    b, s, _ = acmp.shape
    nc = s // CMP_BLOCK
    spec_o = pl.BlockSpec((1, nc, 128), lambda i: (i, 0, 0))
    return pl.pallas_call(
        functools.partial(_compress_kernel, nc=nc),
        grid=(b,),
        in_specs=[
            pl.BlockSpec((1, s, 128), lambda i: (i, 0, 0)),
            pl.BlockSpec((CMP_BLOCK, 128), lambda i: (0, 0)),
            pl.BlockSpec((CMP_BLOCK, 128, 128), lambda i: (0, 0, 0)),
            pl.BlockSpec((2, 128, 128), lambda i: (0, 0, 0)),
        ],
        out_specs=[spec_o, spec_o],
        out_shape=[jax.ShapeDtypeStruct((b, nc, 128), BF16)] * 2,
        compiler_params=_params("arbitrary"),
        name="nsa_compress",
    )(acmp, pos, w1bd, w2bd)


def _compress_weights(cmp_pos, cmp_w1, cmp_w2):
    depth = cmp_pos.shape[0]
    hd = HEAD_DIM
    pos = jnp.concatenate([cmp_pos[:, 0], cmp_pos[:, 1]], axis=-1)
    w1 = cmp_w1.reshape(depth, 2, CMP_BLOCK, hd, hd)
    z = jnp.zeros((depth, CMP_BLOCK, hd, hd), F32)
    w1bd = jnp.concatenate([jnp.concatenate([w1[:, 0], z], axis=-1),
                            jnp.concatenate([z, w1[:, 1]], axis=-1)], axis=-2).astype(BF16)
    z2 = jnp.zeros((depth, hd, hd), F32)
    w2k = jnp.concatenate([jnp.concatenate([cmp_w2[:, 0], z2], axis=-1),
                           jnp.concatenate([z2, z2], axis=-1)], axis=-2)
    w2v = jnp.concatenate([jnp.concatenate([z2, z2], axis=-1),
                           jnp.concatenate([z2, cmp_w2[:, 1]], axis=-1)], axis=-2)
    return pos, w1bd, jnp.stack([w2k, w2v], axis=1).astype(BF16)


def _nsa_kernel(aq_ref, ag_ref, kc_ref, vc_ref, ksel_ref, vsel_ref, kwin_ref, vwin_ref, wmask_ref, ya_ref,
                qaug_ref, mx_ref, sm_ref, acc_ref, owin_ref, kmax_ref, *, tq, seq):
    qi = pl.program_id(1)
    qstart = qi * tq
    nc = seq // CMP_BLOCK
    half = nc // 2
    ns = seq // SEL_BLOCK
    t_lane = lax.broadcasted_iota(jnp.int32, (1, tq), 1) + qstart
    lane128 = lax.broadcasted_iota(jnp.int32, (1, 128), 1)
    nsub = tq // NSA_SUB
    sub_rows_log2 = (N_HEADS * NSA_SUB).bit_length() - 1
    row0 = lambda u, h: (u * N_HEADS + h) * NSA_SUB

    @pl.when(qi == 0)
    def _():
        def chunk(c, best):
            kk = ksel_ref[0, pl.ds(pl.multiple_of(c * NSA_TK, NSA_TK), NSA_TK), 0:128].astype(F32)
            sq = jnp.sum(jnp.where(lane128 < HEAD_DIM, kk * kk, 0.0), axis=-1, keepdims=True)
            return jnp.maximum(best, jnp.max(sq, axis=0, keepdims=True))
        best = lax.fori_loop(0, seq // NSA_TK, chunk, jnp.zeros((1, 1), F32))
        kmax_ref[0] = jnp.sqrt(best)[0, 0]

    row_c = lax.broadcasted_iota(jnp.int32, (nc, 1), 0)
    blk_c = jnp.where(row_c < half, 2 * row_c, 2 * (row_c - half) + 1)
    dist = t_lane - (blk_c * CMP_BLOCK + (CMP_BLOCK - 1))
    dist_f = dist.astype(F32)
    mask_c = dist >= 0
    kc = kc_ref[0]
    vc = vc_ref[0]
    psum = jnp.zeros((nc, tq), F32)
    ocmp = []
    for h in range(N_HEADS):
        grp = aq_ref[0, :, 128 * (h // 2):128 * (h // 2) + 128].astype(F32)
        if h % 2 == 1:
            grp = pltpu.roll(grp, 64, 1)
        left = jnp.where(lane128 < HEAD_DIM, grp * QK_SCALE,
                         jnp.where(lane128 < HEAD_DIM + 2, SLOPES[h], 0.0)).astype(BF16)
        for u in range(nsub):
            qaug_ref[row0(u, h):row0(u, h) + NSA_SUB, 0:128] = left[u * NSA_SUB:(u + 1) * NSA_SUB]
        s = _dot_t(kc, left) - SLOPES[h] * dist_f
        s = jnp.where(mask_c, s, NEG_INF)
        m = jnp.max(s, axis=0, keepdims=True)
        e = jnp.where(mask_c, jnp.exp(s - m), 0.0)
        den = jnp.sum(e, axis=0, keepdims=True)
        p = e * (1.0 / jnp.maximum(den, TINY))
        psum = psum + p
        ocmp.append(lax.dot_general(p.astype(BF16), vc, (((0,), (0,)), ((), ())),
                                    preferred_element_type=F32))

    rows = N_HEADS * tq
    row_id = lax.broadcasted_iota(jnp.int32, (rows, 1), 0)
    t_row = qstart + (row_id >> sub_rows_log2) * NSA_SUB + (row_id & (NSA_SUB - 1))
    n_win = NSA_WINDOW + NSA_SUB
    full_chunks = NSA_WINDOW // NSA_SUB
    for u in range(nsub):
        q0 = qstart + u * NSA_SUB
        w0 = pl.multiple_of(jnp.clip(q0 - NSA_WINDOW, 0, seq - n_win), NSA_SUB)
        sub_rows = slice(row0(u, 0), row0(u, 0) + N_HEADS * NSA_SUB)
        sw = _dot_t(qaug_ref[sub_rows, 0:128], kwin_ref[0, pl.ds(w0, n_win), :])
        diag_chunk = (q0 - w0) >> (NSA_SUB.bit_length() - 1)
        pieces = []
        for c in range(n_win // NSA_SUB):
            d = diag_chunk - c
            pattern = jnp.where(d == 0, 1, jnp.where((d > 0) & (d < full_chunks), 2,
                                                     jnp.where(d == full_chunks, 3, 0)))
            pieces.append(sw[:, c * NSA_SUB:(c + 1) * NSA_SUB] + wmask_ref[pattern])
        sw = jnp.concatenate(pieces, axis=1)
        ew = jnp.exp(sw - jnp.max(sw, axis=-1, keepdims=True))
        owin_ref[sub_rows, :] = (_dot(ew.astype(BF16), vwin_ref[0, pl.ds(w0, n_win), :])
                                 * (1.0 / jnp.sum(ew, axis=-1, keepdims=True)))

    imp = psum[:half] + psum[half:]
    blk = lax.broadcasted_iota(jnp.int32, (ns, 1), 0)
    cur = t_lane >> 6
    future = blk * SEL_BLOCK > t_lane
    forced = (blk == 0) | (blk == cur) | (blk == cur - 1)
    score = jnp.where(forced, FORCED_SCORE, jnp.where(future, -1.0, imp))
    blk_f = blk.astype(F32)
    sel_t = jnp.zeros((ns, tq), F32)
    for _ in range(min(SEL_TOPK, ns)):
        mx = jnp.max(score, axis=0, keepdims=True)
        first = jnp.min(jnp.where(score == mx, blk_f, float(ns)), axis=0, keepdims=True)
        pick = blk_f == first
        sel_t = jnp.where(pick, 1.0, sel_t)
        score = jnp.where(pick, -jnp.inf, score)
    sel_f = jnp.transpose(sel_t)
    negmask = jnp.where(sel_f > 0.5, 0.0, NEG_INF).astype(BF16)
    if ns < 128:
        negmask = jnp.concatenate([negmask, jnp.zeros((tq, 128 - ns), BF16)], axis=1)
    for u in range(nsub):
        for h in range(N_HEADS):
            qaug_ref[row0(u, h):row0(u, h) + NSA_SUB, 128:256] = negmask[u * NSA_SUB:(u + 1) * NSA_SUB]

    if ns < 128:
        sel_f = jnp.concatenate([sel_f, jnp.zeros((tq, 128 - ns), F32)], axis=1)
    any_blk = jnp.max(sel_f, axis=0, keepdims=True)
    blocks_per_tile = NSA_TK // SEL_BLOCK
    shift = 1
    while shift < blocks_per_tile:
        any_blk = jnp.maximum(any_blk, pltpu.roll(any_blk, 128 - shift, 1))
        shift *= 2
    tile_of_lane = lane128 >> (blocks_per_tile.bit_length() - 1)
    first_of_tile = (lane128 & (blocks_per_tile - 1)) == 0
    pow2 = lax.bitcast_convert_type(((tile_of_lane & 15) + 127) << 23, F32)
    weighted = jnp.where(first_of_tile, any_blk * pow2, 0.0)
    bits_lo = jnp.sum(jnp.where(tile_of_lane < 16, weighted, 0.0), axis=1, keepdims=True).astype(jnp.int32)[0, 0]
    bits_hi = jnp.sum(jnp.where(tile_of_lane >= 16, weighted, 0.0), axis=1, keepdims=True).astype(jnp.int32)[0, 0]

    def tile_selected(t):
        return ((jnp.where(t < 16, bits_lo, bits_hi) >> (t & 15)) & 1) == 1

    col_k = lax.broadcasted_iota(jnp.int32, (1, NSA_TK), 1)
    t_last = qi // (NSA_TK // tq)
    groups = NSA_TK // 128

    def scores(t, diagonal):
        k0 = pl.multiple_of(t * NSA_TK, NSA_TK)
        sc = _dot_t(qaug_ref[...], ksel_ref[0, pl.ds(k0, NSA_TK), :])
        if diagonal:
            sc = jnp.where(col_k + k0 <= t_row, sc, NEG_INF)
        return sc

    def fold_max(t, diagonal):
        sc = scores(t, diagonal)
        m = mx_ref[...]
        for g in range(groups):
            m = jnp.maximum(m, sc[:, 128 * g:128 * (g + 1)])
        mx_ref[...] = m

    def accumulate(t, diagonal):
        sc = scores(t, diagonal)
        m = mx_ref[...]
        e = jnp.exp(sc - jnp.concatenate([m] * groups, axis=1))
        part = sm_ref[...]
        for g in range(groups):
            part = part + e[:, 128 * g:128 * (g + 1)]
        sm_ref[...] = part
        k0 = pl.multiple_of(t * NSA_TK, NSA_TK)
        acc_ref[...] += _dot(e.astype(BF16), vsel_ref[0, pl.ds(k0, NSA_TK), :])

    def over_tiles(fn):
        def body(t, carry):
            pl.when(tile_selected(t))(lambda: fn(t, False))
            return carry
        lax.fori_loop(0, t_last, body, 0)
        fn(t_last, True)

    qpart = qaug_ref[:, 0:128].astype(F32)
    qnorm = jnp.sqrt(jnp.sum(jnp.where(lane128 < HEAD_DIM, qpart * qpart, 0.0), axis=-1, keepdims=True))
    reach = qnorm * (kmax_ref[0] * BOUND_SLACK)
    bound_ok = jnp.max(reach, axis=0, keepdims=True)[0, 0] * 2.0 <= MAX_BOUND_GAP

    @pl.when(bound_ok)
    def _():
        head_of_row = (row_id >> (NSA_SUB.bit_length() - 1)) & (N_HEADS - 1)
        slope = jnp.full((rows, 1), SLOPES[0], F32)
        for h in range(1, N_HEADS):
            slope = jnp.where(head_of_row == h, SLOPES[h], slope)
        mx_ref[...] = jnp.broadcast_to(reach + slope * t_row.astype(F32), mx_ref.shape)

    @pl.when(jnp.logical_not(bound_ok))
    def _():
        mx_ref[...] = jnp.full(mx_ref.shape, NEG_INF, F32)
        over_tiles(fold_max)
        mx_ref[...] = jnp.broadcast_to(jnp.max(mx_ref[...], axis=-1, keepdims=True), mx_ref.shape)

    sm_ref[...] = jnp.zeros(sm_ref.shape, F32)
    acc_ref[...] = jnp.zeros(acc_ref.shape, F32)
    over_tiles(accumulate)
    osel = acc_ref[...] * (1.0 / jnp.sum(sm_ref[...], axis=-1, keepdims=True))

    gate = _sigmoid(ag_ref[0])
    for u in range(nsub):
        q_rows = slice(u * NSA_SUB, (u + 1) * NSA_SUB)
        comb = []
        for h in range(N_HEADS):
            head_rows = slice(row0(u, h), row0(u, h) + NSA_SUB)
            comb.append(gate[q_rows, 3 * h:3 * h + 1] * ocmp[h][q_rows, :]
                        + gate[q_rows, 3 * h + 1:3 * h + 2] * osel[head_rows, :]
                        + gate[q_rows, 3 * h + 2:3 * h + 3] * owin_ref[head_rows, :])
        for g in range(2):
            ya_ref[0, q_rows, 128 * g:128 * (g + 1)] = jnp.where(
                lane128 < HEAD_DIM, pltpu.roll(comb[2 * g], 64, 1), comb[2 * g + 1])


def _nsa(aq, ag, kc, vc, ksel, akv, kwin, tq):
    b, s, _ = aq.shape
    nc = kc.shape[1]
    rows = N_HEADS * tq
    whole = lambda w, blk: pl.BlockSpec((1, s, w), lambda i, j, blk=blk: (i, 0, blk))
    return pl.pallas_call(
        functools.partial(_nsa_kernel, tq=tq, seq=s),
        grid=(b, s // tq),
        in_specs=[
            pl.BlockSpec((1, tq, 256), lambda i, j: (i, j, 0)),
            pl.BlockSpec((1, tq, 128), lambda i, j: (i, j, 0)),
            pl.BlockSpec((1, nc, 128), lambda i, j: (i, 0, 0)),
            pl.BlockSpec((1, nc, 128), lambda i, j: (i, 0, 0)),
            whole(256, 0),
            whole(128, 0),
            whole(128, 0),
            whole(128, 1),
            pl.BlockSpec((4, N_HEADS * NSA_SUB, NSA_SUB), lambda i, j: (0, 0, 0)),
        ],
        out_specs=pl.BlockSpec((1, tq, 256), lambda i, j: (i, j, 0)),
        out_shape=jax.ShapeDtypeStruct((b, s, 256), F32),
        scratch_shapes=[
            pltpu.VMEM((rows, 256), BF16),
            pltpu.VMEM((rows, 128), F32),
            pltpu.VMEM((rows, 128), F32),
            pltpu.VMEM((rows, 128), F32),
            pltpu.VMEM((rows, 128), F32),
            pltpu.SMEM((1,), F32),
        ],
        compiler_params=_params("arbitrary", "arbitrary"),
        name="nsa_attention",
    )(aq, ag, kc, vc, ksel, akv, kwin, akv, _window_mask_table())


def _window_mask_table():
    r = np.arange(N_HEADS * NSA_SUB)[:, None] % NSA_SUB
    j = np.arange(NSA_SUB)[None, :]
    keep = np.stack([np.zeros_like(j <= r), j <= r, np.ones_like(j <= r), j > r])
    return jnp.asarray(np.where(keep, 0.0, NEG_INF), F32)


def _swa_kernel(sink_ref, bq_ref, bkv_ref, yb_ref, *, tq, seq):
    qi = pl.program_id(1)
    sub = SWA_WINDOW
    nk = 2 * sub
    lane128 = lax.broadcasted_iota(jnp.int32, (1, 128), 1)
    for u in range(tq // sub):
        qstart = qi * tq + u * sub
        k0 = pl.multiple_of(jnp.maximum(qstart - sub, 0), sub)
        kk = bkv_ref[0, pl.ds(k0, nk), 0:128]
        vv = bkv_ref[0, pl.ds(k0, nk), 128:256]
        t_lane = lax.broadcasted_iota(jnp.int32, (1, sub), 1) + qstart
        s_col = lax.broadcasted_iota(jnp.int32, (nk, 1), 0) + k0
        rel = t_lane - s_col
        rel_f = rel.astype(F32)
        mask = (rel >= 0) & (rel < SWA_WINDOW)
        for g in range(2):
            qg = bq_ref[0, u * sub:(u + 1) * sub, 128 * g:128 * (g + 1)]
            outs = []
            for p in range(2):
                h = SWA_HEAD_ORDER[2 * g + p]
                in_half = (lane128 >= HEAD_DIM * p) & (lane128 < HEAD_DIM * (p + 1))
                qh = jnp.where(in_half, qg, jnp.zeros_like(qg))
                s = _dot_t(kk, qh) * QK_SCALE - SLOPES[h] * rel_f
                s = jnp.where(mask, s, NEG_INF)
                sink = sink_ref[h]
                m = jnp.maximum(jnp.max(s, axis=0, keepdims=True), sink)
                e = jnp.where(mask, jnp.exp(s - m), 0.0)
                den = jnp.sum(e, axis=0, keepdims=True) + jnp.exp(sink - m)
                pr = e * (1.0 / jnp.maximum(den, TINY))
                outs.append(lax.dot_general(pr.astype(BF16), vv, (((0,), (0,)), ((), ())),
                                            preferred_element_type=F32))
            yb_ref[0, u * sub:(u + 1) * sub, 128 * g:128 * (g + 1)] = jnp.where(
                lane128 < HEAD_DIM, outs[0], outs[1])


def _swa(sink, bq, bkv, tq):
    b, s, _ = bq.shape
    return pl.pallas_call(
        functools.partial(_swa_kernel, tq=tq, seq=s),
        grid=(b, s // tq),
        in_specs=[
            pl.BlockSpec(memory_space=pltpu.SMEM),
            pl.BlockSpec((1, tq, 256), lambda i, j: (i, j, 0)),
            pl.BlockSpec((1, s, 256), lambda i, j: (i, 0, 0)),
        ],
        out_specs=pl.BlockSpec((1, tq, 256), lambda i, j: (i, j, 0)),
        out_shape=jax.ShapeDtypeStruct((b, s, 256), F32),
        compiler_params=_params("arbitrary", "arbitrary"),
        name="swa_attention",
    )(sink, bq, bkv)


def _stick_kernel(cq_ref, ck_ref, cv_ref, yc_ref, qm_ref, carry_ref, o_ref, *, tq, seq):
    qi = pl.program_id(1)
    qstart = qi * tq
    lane256 = lax.broadcasted_iota(jnp.int32, (1, 256), 1)
    head_of_lane = lane256 >> 6
    rows = N_HEADS * tq
    q = cq_ref[0].astype(F32) * QK_SCALE
    for h in range(N_HEADS):
        qm_ref[h * tq:(h + 1) * tq, :] = jnp.where(head_of_lane == h, q, 0.0).astype(BF16)
    jj = lax.broadcasted_iota(jnp.int32, (tq, 2 * tq), 0)
    ss = lax.broadcasted_iota(jnp.int32, (tq, 2 * tq), 1)
    upper = ((jj > ss) | (ss >= tq)).astype(BF16)
    upper2 = jnp.concatenate([upper, upper], axis=0)
    t_in = lax.broadcasted_iota(jnp.int32, (rows, tq), 0) & (tq - 1)
    strict = lax.broadcasted_iota(jnp.int32, (rows, tq), 1) < t_in

    def tile(k0, carry, acc, valid):
        diagonal = carry is None
        kt = ck_ref[0, pl.ds(k0, tq), :]
        vt = cv_ref[0, pl.ds(k0, tq), :]
        z = _dot_t(qm_ref[...], kt)
        soft = jnp.log(1.0 + jnp.exp(-jnp.abs(z)))
        log_beta = jnp.minimum(z, 0.0) - soft
        log_1m = log_beta - z
        if diagonal:
            log_1m = jnp.where(strict, log_1m, 0.0)
        hi = log_1m.astype(BF16)
        lo = (log_1m - hi.astype(F32)).astype(BF16)
        sums = _dot(jnp.concatenate([hi, lo], axis=1), upper2)
        suffix = sums[:, :tq] if diagonal else sums[:, :tq] + carry
        a = jnp.exp(log_beta + suffix)
        if diagonal:
            a = jnp.where(strict, a, 0.0)
        if valid is not None:
            a = jnp.where(valid, a, 0.0)
        a_b = a.astype(BF16)
        a_cat = jnp.concatenate([a_b[h * tq:(h + 1) * tq, :] for h in range(N_HEADS)], axis=1)
        v_bd = jnp.concatenate([jnp.where(head_of_lane == h, vt, jnp.zeros_like(vt))
                                for h in range(N_HEADS)], axis=0)
        acc = acc + _dot(a_cat, v_bd)
        return (sums[:, tq:] if diagonal else carry + sums[:, tq:]), acc

    carry, acc = tile(pl.multiple_of(qstart, tq), None, jnp.zeros((tq, 256), F32), None)
    for d in range(1, STICK_EAGER_TILES + 1):
        j = qi - d
        carry, acc = tile(pl.multiple_of(jnp.maximum(j, 0) * tq, tq), carry, acc, j >= 0)
    carry_ref[...] = carry
    o_ref[...] = acc

    def cond(state):
        j, worst = state
        return (j >= 0) & (worst > STICK_EXIT)

    def body(state):
        j, _ = state
        new_carry, new_acc = tile(pl.multiple_of(j * tq, tq), carry_ref[...], o_ref[...], None)
        carry_ref[...] = new_carry
        o_ref[...] = new_acc
        return j - 1, jnp.max(new_carry, axis=0, keepdims=True)[0, 0]

    lax.while_loop(cond, body, (qi - 1 - STICK_EAGER_TILES, jnp.max(carry, axis=0, keepdims=True)[0, 0]))
    yc_ref[0] = o_ref[...]


def _stick(cq, ck, cv, tq):
    b, s, _ = cq.shape
    return pl.pallas_call(
        functools.partial(_stick_kernel, tq=tq, seq=s),
        grid=(b, s // tq),
        in_specs=[
            pl.BlockSpec((1, tq, 256), lambda i, j: (i, j, 0)),
            pl.BlockSpec((1, s, 256), lambda i, j: (i, 0, 0)),
            pl.BlockSpec((1, s, 256), lambda i, j: (i, 0, 0)),
        ],
        out_specs=pl.BlockSpec((1, tq, 256), lambda i, j: (i, j, 0)),
        out_shape=jax.ShapeDtypeStruct((b, s, 256), F32),
        scratch_shapes=[
            pltpu.VMEM((N_HEADS * tq, 256), BF16),
            pltpu.VMEM((N_HEADS * tq, tq), F32),
            pltpu.VMEM((tq, 256), F32),
        ],
        compiler_params=_params("arbitrary", "arbitrary"),
        name="stick_breaking",
    )(cq, ck, cv)


def _ret_kernel(dq_ref, dk_ref, dv_ref, dmat_ref, xi_ref, zeta_ref, decay_ref, bd_ref, yd_ref, r_ref):
    n = pl.program_id(1)

    @pl.when(n == 0)
    def _():
        r_ref[...] = jnp.zeros(r_ref.shape, F32)

    c = RET_CHUNK
    lane256 = lax.broadcasted_iota(jnp.int32, (1, 256), 1)
    head_of_lane = lane256 >> 6
    r = r_ref[...]
    for u in range(dq_ref.shape[1] // c):
        rows = slice(u * c, (u + 1) * c)
        q = dq_ref[0, rows, :].astype(F32) * QK_SCALE
        k = dk_ref[0, rows, :]
        v = dv_ref[0, rows, :]
        o = _dot(q.astype(BF16), r.astype(BF16)) * xi_ref[...]
        for h in range(N_HEADS):
            qh = jnp.where(head_of_lane == h, q, 0.0).astype(BF16)
            s = _dot_t(qh, k) * dmat_ref[h]
            o = o + jnp.where(head_of_lane == h, _dot(s.astype(BF16), v), 0.0)
        mu = jnp.zeros((c, 256), F32)
        for h in range(N_HEADS):
            in_h = head_of_lane == h
            mu = mu + jnp.where(in_h, jnp.sum(jnp.where(in_h, o, 0.0), axis=-1, keepdims=True), 0.0)
        cen = o - mu * (1.0 / HEAD_DIM)
        var = jnp.zeros((c, 256), F32)
        for h in range(N_HEADS):
            in_h = head_of_lane == h
            var = var + jnp.where(in_h, jnp.sum(jnp.where(in_h, cen * cen, 0.0), axis=-1, keepdims=True), 0.0)
        yd_ref[0, rows, :] = cen * lax.rsqrt(var * (1.0 / HEAD_DIM) + LN_EPS)
        kz = (k.astype(F32) * zeta_ref[...]).astype(BF16)
        upd = lax.dot_general(kz, v, (((0,), (0,)), ((), ())), preferred_element_type=F32)
        r = r * decay_ref[...] + upd * bd_ref[...]
    r_ref[...] = r


def _retention(dq, dk, dv, consts):
    b, s, _ = dq.shape
    c = RET_CHUNK
    dmat, xi, zeta, decay, bd = consts
    step = RET_CHUNKS_PER_STEP * c
    tile = pl.BlockSpec((1, step, 256), lambda i, j: (i, j, 0))
    return pl.pallas_call(
        _ret_kernel,
        grid=(b, s // step),
        in_specs=[
            tile, tile, tile,
            pl.BlockSpec((N_HEADS, c, c), lambda i, j: (0, 0, 0)),
            pl.BlockSpec((c, 256), lambda i, j: (0, 0)),
            pl.BlockSpec((c, 256), lambda i, j: (0, 0)),
            pl.BlockSpec((256, 256), lambda i, j: (0, 0)),
            pl.BlockSpec((256, 256), lambda i, j: (0, 0)),
        ],
        out_specs=tile,
        out_shape=jax.ShapeDtypeStruct((b, s, 256), F32),
        scratch_shapes=[pltpu.VMEM((256, 256), F32)],
        compiler_params=_params("arbitrary", "arbitrary"),
        name="retention",
    )(dq, dk, dv, dmat, xi, zeta, decay, bd)


def _retention_consts():
    c = RET_CHUNK
    log_g = jnp.log(1.0 - jnp.asarray(2.0 ** (-5.0 - np.arange(N_HEADS)), dtype=F32))
    i = jnp.arange(c)
    diff = (i[:, None] - i[None, :]).astype(F32)
    dmat = jnp.where(diff >= 0, jnp.exp(log_g[:, None, None] * jnp.maximum(diff, 0.0)), 0.0)
    zeta = jnp.exp(log_g[:, None] * (c - 1 - i)[None, :].astype(F32))
    xi = jnp.exp(log_g[:, None] * (i + 1)[None, :].astype(F32))
    g_chunk = jnp.exp(log_g * c)
    lanes = lambda hc: jnp.repeat(hc.T, HEAD_DIM, axis=1)
    head = np.arange(256) // HEAD_DIM
    bd = jnp.asarray(head[:, None] == head[None, :], F32)
    decay = jnp.repeat(g_chunk, HEAD_DIM)[:, None] * jnp.ones((1, 256), F32)
    return dmat, lanes(xi), lanes(zeta), decay, bd


def _merge_kernel(x_ref, mod_ref, g_ref, ya_ref, yb_ref, yc_ref, yd_ref, z_ref,
                  wm_ref, wbr_ref, wout_ref, fg_ref, o_ref, *, d, final):
    mod = mod_ref[0]
    tm = x_ref.shape[1]
    part = tm // MERGE_ROW_PARTS
    for r in range(MERGE_ROW_PARTS):
        rows = slice(r * part, (r + 1) * part)
        xf = x_ref[0, rows, :]
        h = _modulated_norm(xf, g_ref[...], mod, d).astype(BF16)
        merged = None
        for i, y_ref in enumerate((ya_ref, yb_ref, yc_ref, yd_ref)):
            yi = (y_ref[0, rows, :] * _silu(z_ref[0, rows, 256 * i:256 * (i + 1)])).astype(BF16)
            term = _sigmoid(_dot(h, wm_ref[i])) * _dot(yi, wbr_ref[i])
            merged = term if merged is None else merged + term
        out = xf + mod[:, 2 * d:3 * d] * _dot(merged.astype(BF16), wout_ref[...])
        if final:
            out = out * lax.rsqrt(jnp.mean(out * out, axis=-1, keepdims=True) + RMS_EPS) * fg_ref[...]
        o_ref[0, rows, :] = out


def _merge(x, mod, g, ys, z, wm, wbr, wout, final_g, final, tm):
    b, s, d = x.shape
    ytile = pl.BlockSpec((1, tm, 256), lambda i, j: (i, j, 0))
    return pl.pallas_call(
        functools.partial(_merge_kernel, d=d, final=final),
        grid=(b, s // tm),
        in_specs=[
            pl.BlockSpec((1, tm, d), lambda i, j: (i, j, 0)),
            pl.BlockSpec((1, 1, 3 * d), lambda i, j: (i, 0, 0)),
            pl.BlockSpec((1, d), lambda i, j: (0, 0)),
            ytile, ytile, ytile, ytile,
            pl.BlockSpec((1, tm, 4 * 256), lambda i, j: (i, j, 0)),
            pl.BlockSpec((4, d, d), lambda i, j: (0, 0, 0)),
            pl.BlockSpec((4, 256, d), lambda i, j: (0, 0, 0)),
            pl.BlockSpec((d, d), lambda i, j: (0, 0)),
            pl.BlockSpec((1, d), lambda i, j: (0, 0)),
        ],
        out_specs=pl.BlockSpec((1, tm, d), lambda i, j: (i, j, 0)),
        out_shape=jax.ShapeDtypeStruct((b, s, d), F32),
        compiler_params=_params("arbitrary", "arbitrary"),
        name="merge_out",
    )(x, mod, g.reshape(1, d), *ys, z, wm, wbr, wout, final_g.reshape(1, d))


def _proj_columns():
    a, bb, cc, dd = 0, 908, 1676, 2700
    perm = np.concatenate([np.arange(64) + 64 * h for h in SWA_HEAD_ORDER])
    pad = lambda n: np.full((n,), -1)
    cols = np.concatenate([
        np.arange(a, a + 256),
        np.arange(a + 256, a + 384),
        np.arange(a + 384, a + 640),
        np.arange(a + 640, a + 652), pad(116),
        np.arange(a + 652, a + 908),
        bb + 512 + perm,
        np.arange(cc + 768, cc + 1024),
        np.arange(dd + 768, dd + 1024),
        bb + perm,
        np.arange(bb + 256, bb + 512),
        np.arange(cc, cc + 768),
        np.arange(dd, dd + 768),
    ])
    assert cols.shape[0] == PROJ_W
    return cols, perm


def _take_columns(w, cols):
    pieces, i = [], 0
    while i < len(cols):
        j = i + 1
        if cols[i] < 0:
            while j < len(cols) and cols[j] < 0:
                j += 1
            pieces.append(jnp.zeros(w.shape[:-1] + (j - i,), w.dtype))
        else:
            while j < len(cols) and cols[j] == cols[j - 1] + 1:
                j += 1
            pieces.append(w[..., int(cols[i]):int(cols[j - 1]) + 1])
        i = j
    return jnp.concatenate(pieces, axis=-1)


def _position_features(s):
    pos = np.arange(s)
    feat = np.zeros((s, 64), np.float32)
    feat[:, 0] = SEL_BLOCK * (pos // SEL_BLOCK)
    feat[:, 1] = pos % SEL_BLOCK
    onehot = (pos[:, None] // SEL_BLOCK == np.arange(128)[None, :]).astype(np.float32)
    return jnp.asarray(feat, BF16), jnp.asarray(onehot, BF16)


def kernel(x, c, w_ada, b_ada, norm_g, w_in, cmp_pos, cmp_w1, cmp_w2, sink, w_merge, w_br, w_out, final_g):
    b, s, d = x.shape
    depth = w_ada.shape[0]
    tm = min(ROW_TILE, s)

    cols, perm = _proj_columns()
    w_in_p = _take_columns(w_in, cols).astype(BF16)
    wm_b = w_merge.astype(BF16)
    wbr_b = w_br.at[:, 1].set(w_br[:, 1][:, perm, :]).astype(BF16)
    wout_b = w_out.astype(BF16)
    pos_l, w1bd, w2bd = _compress_weights(cmp_pos, cmp_w1, cmp_w2)
    feat, onehot = _position_features(s)
    kconst = jnp.concatenate([jnp.zeros((s, HEAD_DIM), BF16), feat, onehot], axis=-1)
    ret_consts = _retention_consts()
    mods = _modulation(c, w_ada, b_ada)

    for l in range(depth):
        p = _project(x, mods[l], norm_g[l], w_in_p[l], kconst, tm)
        kc, vc = _compress(p["acmp"], pos_l[l], w1bd[l], w2bd[l])
        y_a = _nsa(p["aq"], p["ag"], kc, vc, p["ksel"], p["akv"], p["kwin"], NSA_TQ)
        y_b = _swa(sink[l], p["bq"], p["bkv"], SWA_TQ)
        y_c = _stick(p["cq"], p["ck"], p["cv"], STICK_TQ)
        y_d = _retention(p["dq"], p["dk"], p["dv"], ret_consts)
        x = _merge(x, mods[l], norm_g[l], (y_a, y_b, y_c, y_d), p["z"], wm_b[l], wbr_b[l], wout_b[l],
                   final_g, l == depth - 1, tm)
    return x
```

```python
import functools

import numpy as np
import jax
import jax.numpy as jnp
from jax import lax
from jax.experimental import pallas as pl
from jax.experimental.pallas import tpu as pltpu

F32 = jnp.float32
BF16 = jnp.bfloat16

HEAD_DIM = 64
N_HEADS = 4
BRANCH_W = N_HEADS * HEAD_DIM
CMP_BLOCK = 32
SEL_BLOCK = 64
SEL_TOPK = 16
NSA_WINDOW = 512
NSA_TK = 512
SWA_WINDOW = 128
RET_CHUNK = 128
RMS_EPS = 1e-6
LN_EPS = 1e-5
NEG_INF = -1e30
TINY = 1e-30
FORCED_SCORE = 1e4
QK_SCALE = HEAD_DIM ** -0.5
STICK_EXIT = -110.0
STICK_EAGER_TILES = 2
MAX_BOUND_GAP = 60.0
BOUND_SLACK = 1.001

ROW_TILE = 512
MERGE_ROW_PARTS = 2
NSA_TQ = 256
NSA_SUB = 128
SWA_TQ = 512
RET_CHUNKS_PER_STEP = 4
STICK_TQ = 128

VMEM_LIMIT = 56 * 1024 * 1024
SLOPES = tuple(float(2.0 ** (-8.0 * (h + 1) / N_HEADS)) for h in range(N_HEADS))

PROJ_OUTS = (
    ("aq", 0, 256, BF16),
    ("acmp", 256, 128, F32),
    ("akv", 384, 256, BF16),
    ("ag", 640, 128, F32),
    ("z", 768, 1024, F32),
    ("bq", 1792, 256, BF16),
    ("bkv", 2048, 256, BF16),
    ("cq", 2304, 256, BF16),
    ("ck", 2560, 256, BF16),
    ("cv", 2816, 256, BF16),
    ("dq", 3072, 256, BF16),
    ("dk", 3328, 256, BF16),
    ("dv", 3584, 256, BF16),
)
PROJ_W = 3840
SWA_HEAD_ORDER = (0, 2, 1, 3)


def _dot(a, b):
    return jnp.dot(a, b, preferred_element_type=F32)


def _dot_t(a, b):
    return lax.dot_general(a, b, (((1,), (1,)), ((), ())), preferred_element_type=F32)


def _dot_split(a, b):
    hi = a.astype(BF16)
    lo = (a - hi.astype(F32)).astype(BF16)
    return _dot(hi, b) + _dot(lo, b)


def _sigmoid(v):
    return 1.0 / (1.0 + jnp.exp(-v))


def _silu(v):
    return v * _sigmoid(v)


def _params(*sem):
    return pltpu.CompilerParams(dimension_semantics=sem, vmem_limit_bytes=VMEM_LIMIT)


def _mod_kernel(c_ref, w_ref, b_ref, o_ref):
    cc = c_ref[...]
    o_ref[0] = _dot(_silu(cc).astype(BF16), w_ref[0].astype(BF16)) + b_ref[0]


def _modulation(c, w_ada, b_ada):
    depth, d, n = w_ada.shape
    b = c.shape[0]
    rows = 8
    cp = jnp.zeros((rows, d), F32).at[:b].set(c)
    tn = 512
    out = pl.pallas_call(
        _mod_kernel,
        grid=(depth, n // tn),
        in_specs=[
            pl.BlockSpec((rows, d), lambda l, j: (0, 0)),
            pl.BlockSpec((1, d, tn), lambda l, j: (l, 0, j)),
            pl.BlockSpec((1, 1, tn), lambda l, j: (l, 0, j)),
        ],
        out_specs=pl.BlockSpec((1, rows, tn), lambda l, j: (l, 0, j)),
        out_shape=jax.ShapeDtypeStruct((depth, rows, n), F32),
        compiler_params=_params("arbitrary", "arbitrary"),
        name="adaln_mod",
    )(cp, w_ada, b_ada.reshape(depth, 1, n))
    return out[:, :b].reshape(depth, b, 1, n)


def _modulated_norm(xf, g, mod, d):
    ms = jnp.mean(xf * xf, axis=-1, keepdims=True)
    y = xf * lax.rsqrt(ms + RMS_EPS) * g
    return y * (1.0 + mod[:, d:2 * d]) + mod[:, 0:d]


def _proj_kernel(x_ref, mod_ref, g_ref, w_ref, kconst_ref, *out_refs, d):
    h = _modulated_norm(x_ref[0], g_ref[...], mod_ref[0], d).astype(BF16)
    ksel_ref, kwin_ref = out_refs[len(PROJ_OUTS):]
    for (name, start, width, dtype), o_ref in zip(PROJ_OUTS, out_refs):
        for c0 in range(0, width, 256):
            cw = min(256, width - c0)
            acc = _dot(h, w_ref[:, start + c0:start + c0 + cw])
            o_ref[0, :, c0:c0 + cw] = acc.astype(dtype)
            if name == "akv":
                ksel_ref[0] = kconst_ref[...]
                ksel_ref[0, :, 0:HEAD_DIM] = acc[:, 0:HEAD_DIM].astype(BF16)
                kwin_ref[0] = kconst_ref[:, 0:128]
                kwin_ref[0, :, 0:HEAD_DIM] = acc[:, 128:128 + HEAD_DIM].astype(BF16)


def _project(x, mod, g, w_p, kconst, tm):
    b, s, d = x.shape
    widths = [(wd, dt) for (_, _, wd, dt) in PROJ_OUTS] + [(256, BF16), (128, BF16)]
    out_shapes = [jax.ShapeDtypeStruct((b, s, wd), dt) for wd, dt in widths]
    out_specs = [pl.BlockSpec((1, tm, wd), lambda i, j: (i, j, 0)) for wd, _ in widths]
    outs = pl.pallas_call(
        functools.partial(_proj_kernel, d=d),
        grid=(b, s // tm),
        in_specs=[
            pl.BlockSpec((1, tm, d), lambda i, j: (i, j, 0)),
            pl.BlockSpec((1, 1, 3 * d), lambda i, j: (i, 0, 0)),
            pl.BlockSpec((1, d), lambda i, j: (0, 0)),
            pl.BlockSpec((d, PROJ_W), lambda i, j: (0, 0)),
            pl.BlockSpec((tm, 256), lambda i, j: (j, 0)),
        ],
        out_specs=out_specs,
        out_shape=out_shapes,
        compiler_params=_params("arbitrary", "arbitrary"),
        name="in_proj",
    )(x, mod, g.reshape(1, d), w_p, kconst)
    names = [name for (name, _, _, _) in PROJ_OUTS] + ["ksel", "kwin"]
    return dict(zip(names, outs))


def _compress_kernel(acmp_ref, pos_ref, w1_ref, w2_ref, kc_ref, vc_ref, *, nc):
    half = nc // 2
    for parity in range(2):
        hid = jnp.zeros((half, 128), F32)
        for i in range(CMP_BLOCK):
            z = acmp_ref[0, pl.ds(parity * CMP_BLOCK + i, half, stride=2 * CMP_BLOCK), :]
            hid = hid + _dot((z + pos_ref[i:i + 1, :]).astype(BF16), w1_ref[i])
        act = _silu(hid).astype(BF16)
        out_rows = slice(parity * half, (parity + 1) * half)
        kc_ref[0, out_rows, :] = _dot(act, w2_ref[0]).astype(BF16)
        vc_ref[0, out_rows, :] = _dot(act, w2_ref[1]).astype(BF16)


def _compress(acmp, pos, w1bd, w2bd):
    b, s, _ = acmp.shape
    nc = s // CMP_BLOCK
    spec_o = pl.BlockSpec((1, nc, 128), lambda i: (i, 0, 0))
    return pl.pallas_call(
        functools.partial(_compress_kernel, nc=nc),
        grid=(b,),
        in_specs=[
            pl.BlockSpec((1, s, 128), lambda i: (i, 0, 0)),
            pl.BlockSpec((CMP_BLOCK, 128), lambda i: (0, 0)),
            pl.BlockSpec((CMP_BLOCK, 128, 128), lambda i: (0, 0, 0)),
            pl.BlockSpec((2, 128, 128), lambda i: (0, 0, 0)),
        ],
        out_specs=[spec_o, spec_o],
        out_shape=[jax.ShapeDtypeStruct((b, nc, 128), BF16)] * 2,
        compiler_params=_params("arbitrary"),
        name="nsa_compress",
    )(acmp, pos, w1bd, w2bd)


def _compress_weights(cmp_pos, cmp_w1, cmp_w2):
    depth = cmp_pos.shape[0]
    hd = HEAD_DIM
    pos = jnp.concatenate([cmp_pos[:, 0], cmp_pos[:, 1]], axis=-1)
    w1 = cmp_w1.reshape(depth, 2, CMP_BLOCK, hd, hd)
    z = jnp.zeros((depth, CMP_BLOCK, hd, hd), F32)
    w1bd = jnp.concatenate([jnp.concatenate([w1[:, 0], z], axis=-1),
                            jnp.concatenate([z, w1[:, 1]], axis=-1)], axis=-2).astype(BF16)
    z2 = jnp.zeros((depth, hd, hd), F32)
    w2k = jnp.concatenate([jnp.concatenate([cmp_w2[:, 0], z2], axis=-1),
                           jnp.concatenate([z2, z2], axis=-1)], axis=-2)
    w2v = jnp.concatenate([jnp.concatenate([z2, z2], axis=-1),
                           jnp.concatenate([z2, cmp_w2[:, 1]], axis=-1)], axis=-2)
    return pos, w1bd, jnp.stack([w2k, w2v], axis=1).astype(BF16)


def _nsa_kernel(aq_ref, ag_ref, kc_ref, vc_ref, ksel_ref, vsel_ref, kwin_ref, vwin_ref, wmask_ref, ya_ref,
                qaug_ref, mx_ref, sm_ref, acc_ref, owin_ref, kmax_ref, *, tq, seq):
    qi = pl.program_id(1)
    qstart = qi * tq
    nc = seq // CMP_BLOCK
    half = nc // 2
    ns = seq // SEL_BLOCK
    t_lane = lax.broadcasted_iota(jnp.int32, (1, tq), 1) + qstart
    lane128 = lax.broadcasted_iota(jnp.int32, (1, 128), 1)
    nsub = tq // NSA_SUB
    sub_rows_log2 = (N_HEADS * NSA_SUB).bit_length() - 1
    row0 = lambda u, h: (u * N_HEADS + h) * NSA_SUB
    rows = N_HEADS * tq
    row_id = lax.broadcasted_iota(jnp.int32, (rows, 1), 0)
    t_row = qstart + (row_id >> sub_rows_log2) * NSA_SUB + (row_id & (NSA_SUB - 1))

    @pl.when(qi == 0)
    def _():
        def chunk(c, best):
            kk = ksel_ref[0, pl.ds(pl.multiple_of(c * NSA_TK, NSA_TK), NSA_TK), 0:128].astype(F32)
            sq = jnp.sum(jnp.where(lane128 < HEAD_DIM, kk * kk, 0.0), axis=-1, keepdims=True)
            return jnp.maximum(best, jnp.max(sq, axis=0, keepdims=True))
        best = lax.fori_loop(0, seq // NSA_TK, chunk, jnp.zeros((1, 1), F32))
        kmax_ref[0] = jnp.sqrt(best)[0, 0]

    row_c = lax.broadcasted_iota(jnp.int32, (nc, 1), 0)
    blk_c = jnp.where(row_c < half, 2 * row_c, 2 * (row_c - half) + 1)
    dist = t_lane - (blk_c * CMP_BLOCK + (CMP_BLOCK - 1))
    dist_f = dist.astype(F32)
    mask_c = dist >= 0
    kc = kc_ref[0]
    vc = vc_ref[0]
    psum = jnp.zeros((nc, tq), F32)
    ocmp = []
    for h in range(N_HEADS):
        grp = aq_ref[0, :, 128 * (h // 2):128 * (h // 2) + 128].astype(F32)
        if h % 2 == 1:
            grp = pltpu.roll(grp, 64, 1)
        left = jnp.where(lane128 < HEAD_DIM, grp * QK_SCALE,
                         jnp.where(lane128 < HEAD_DIM + 2, SLOPES[h], 0.0)).astype(BF16)
        for u in range(nsub):
            qaug_ref[row0(u, h):row0(u, h) + NSA_SUB, 0:128] = left[u * NSA_SUB:(u + 1) * NSA_SUB]
        s = _dot_t(kc, left) - SLOPES[h] * dist_f
        s = jnp.where(mask_c, s, NEG_INF)
        m = jnp.max(s, axis=0, keepdims=True)
        e = jnp.where(mask_c, jnp.exp(s - m), 0.0)
        den = jnp.sum(e, axis=0, keepdims=True)
        p = e * (1.0 / jnp.maximum(den, TINY))
        psum = psum + p
        ocmp.append(lax.dot_general(p.astype(BF16), vc, (((0,), (0,)), ((), ())),
                                    preferred_element_type=F32))

    def fold(sc):
        m = mx_ref[...]
        for g in range(sc.shape[1] // 128):
            m = jnp.maximum(m, sc[:, 128 * g:128 * (g + 1)])
        mx_ref[...] = m

    def accumulate(sc, v):
        m = mx_ref[...]
        e = jnp.exp(sc - jnp.concatenate([m] * (sc.shape[1] // 128), axis=1))
        part = sm_ref[...]
        for g in range(sc.shape[1] // 128):
            part = part + e[:, 128 * g:128 * (g + 1)]
        sm_ref[...] = part
        acc_ref[...] += _dot(e.astype(BF16), v)

    own0 = pl.multiple_of(qstart, tq)

    def own_scores():
        sc = _dot_t(qaug_ref[:, 0:128], ksel_ref[0, pl.ds(own0, tq), 0:128])
        return jnp.where(lax.broadcasted_iota(jnp.int32, (1, tq), 1) + qstart <= t_row, sc, NEG_INF)

    def reset_sums():
        sm_ref[...] = jnp.zeros(sm_ref.shape, F32)
        acc_ref[...] = jnp.zeros(acc_ref.shape, F32)

    qpart = qaug_ref[:, 0:128].astype(F32)
    qnorm = jnp.sqrt(jnp.sum(jnp.where(lane128 < HEAD_DIM, qpart * qpart, 0.0), axis=-1, keepdims=True))
    reach = qnorm * (kmax_ref[0] * BOUND_SLACK)
    bound_ok = jnp.max(reach, axis=0, keepdims=True)[0, 0] * 2.0 <= MAX_BOUND_GAP
    head_of_row = (row_id >> (NSA_SUB.bit_length() - 1)) & (N_HEADS - 1)
    slope = jnp.full((rows, 1), SLOPES[0], F32)
    for h in range(1, N_HEADS):
        slope = jnp.where(head_of_row == h, SLOPES[h], slope)
    mx_ref[...] = jnp.broadcast_to(reach + slope * t_row.astype(F32), mx_ref.shape)
    reset_sums()
    accumulate(own_scores(), vsel_ref[0, pl.ds(own0, tq), :])

    n_win = NSA_WINDOW + NSA_SUB
    full_chunks = NSA_WINDOW // NSA_SUB
    for u in range(nsub):
        q0 = qstart + u * NSA_SUB
        w0 = pl.multiple_of(jnp.clip(q0 - NSA_WINDOW, 0, seq - n_win), NSA_SUB)
        sub_rows = slice(row0(u, 0), row0(u, 0) + N_HEADS * NSA_SUB)
        sw = _dot_t(qaug_ref[sub_rows, 0:128], kwin_ref[0, pl.ds(w0, n_win), :])
        diag_chunk = (q0 - w0) >> (NSA_SUB.bit_length() - 1)
        pieces = []
        for c in range(n_win // NSA_SUB):
            d = diag_chunk - c
            pattern = jnp.where(d == 0, 1, jnp.where((d > 0) & (d < full_chunks), 2,
                                                     jnp.where(d == full_chunks, 3, 0)))
            pieces.append(sw[:, c * NSA_SUB:(c + 1) * NSA_SUB] + wmask_ref[pattern])
        sw = jnp.concatenate(pieces, axis=1)
        ew = jnp.exp(sw - jnp.max(sw, axis=-1, keepdims=True))
        owin_ref[sub_rows, :] = (_dot(ew.astype(BF16), vwin_ref[0, pl.ds(w0, n_win), :])
                                 * (1.0 / jnp.sum(ew, axis=-1, keepdims=True)))

    imp = psum[:half] + psum[half:]
    blk = lax.broadcasted_iota(jnp.int32, (ns, 1), 0)
    cur = t_lane >> 6
    future = blk * SEL_BLOCK > t_lane
    forced = (blk == 0) | (blk == cur) | (blk == cur - 1)
    score = jnp.where(forced, FORCED_SCORE, jnp.where(future, -1.0, imp))
    blk_f = blk.astype(F32)
    sel_t = jnp.zeros((ns, tq), F32)
    for _ in range(min(SEL_TOPK, ns)):
        mx = jnp.max(score, axis=0, keepdims=True)
        first = jnp.min(jnp.where(score == mx, blk_f, float(ns)), axis=0, keepdims=True)
        pick = blk_f == first
        sel_t = jnp.where(pick, 1.0, sel_t)
        score = jnp.where(pick, -jnp.inf, score)
    sel_f = jnp.transpose(sel_t)
    negmask = jnp.where(sel_f > 0.5, 0.0, NEG_INF).astype(BF16)
    if ns < 128:
        negmask = jnp.concatenate([negmask, jnp.zeros((tq, 128 - ns), BF16)], axis=1)
    for u in range(nsub):
        for h in range(N_HEADS):
            qaug_ref[row0(u, h):row0(u, h) + NSA_SUB, 128:256] = negmask[u * NSA_SUB:(u + 1) * NSA_SUB]

    if ns < 128:
        sel_f = jnp.concatenate([sel_f, jnp.zeros((tq, 128 - ns), F32)], axis=1)
    any_blk = jnp.max(sel_f, axis=0, keepdims=True)
    blocks_per_tile = NSA_TK // SEL_BLOCK
    shift = 1
    while shift < blocks_per_tile:
        any_blk = jnp.maximum(any_blk, pltpu.roll(any_blk, 128 - shift, 1))
        shift *= 2
    tile_of_lane = lane128 >> (blocks_per_tile.bit_length() - 1)
    first_of_tile = (lane128 & (blocks_per_tile - 1)) == 0
    pow2 = lax.bitcast_convert_type(((tile_of_lane & 15) + 127) << 23, F32)
    weighted = jnp.where(first_of_tile, any_blk * pow2, 0.0)
    bits_lo = jnp.sum(jnp.where(tile_of_lane < 16, weighted, 0.0), axis=1, keepdims=True).astype(jnp.int32)[0, 0]
    bits_hi = jnp.sum(jnp.where(tile_of_lane >= 16, weighted, 0.0), axis=1, keepdims=True).astype(jnp.int32)[0, 0]

    def tile_selected(t):
        return ((jnp.where(t < 16, bits_lo, bits_hi) >> (t & 15)) & 1) == 1

    pieces_per_tile = NSA_TK // tq
    t_own = qi // pieces_per_tile
    n_pieces = qi - t_own * pieces_per_tile

    def sweep(use):
        def piece(p, carry):
            k0 = pl.multiple_of(t_own * NSA_TK + p * tq, tq)
            use(_dot_t(qaug_ref[...], ksel_ref[0, pl.ds(k0, tq), :]), k0, tq)
            return carry

        def tile(t, carry):
            def visit():
                k0 = pl.multiple_of(t * NSA_TK, NSA_TK)
                use(_dot_t(qaug_ref[...], ksel_ref[0, pl.ds(k0, NSA_TK), :]), k0, NSA_TK)
            pl.when(tile_selected(t))(visit)
            return carry

        lax.fori_loop(0, n_pieces, piece, 0)
        lax.fori_loop(0, t_own, tile, 0)

    @pl.when(jnp.logical_not(bound_ok))
    def _():
        mx_ref[...] = jnp.full(mx_ref.shape, NEG_INF, F32)
        sweep(lambda sc, k0, n: fold(sc))
        fold(own_scores())
        mx_ref[...] = jnp.broadcast_to(jnp.max(mx_ref[...], axis=-1, keepdims=True), mx_ref.shape)
        reset_sums()
        accumulate(own_scores(), vsel_ref[0, pl.ds(own0, tq), :])

    sweep(lambda sc, k0, n: accumulate(sc, vsel_ref[0, pl.ds(k0, n), :]))
    osel = acc_ref[...] * (1.0 / jnp.sum(sm_ref[...], axis=-1, keepdims=True))

    gate = _sigmoid(ag_ref[0])
    for u in range(nsub):
        q_rows = slice(u * NSA_SUB, (u + 1) * NSA_SUB)
        comb = []
        for h in range(N_HEADS):
            head_rows = slice(row0(u, h), row0(u, h) + NSA_SUB)
            comb.append(gate[q_rows, 3 * h:3 * h + 1] * ocmp[h][q_rows, :]
                        + gate[q_rows, 3 * h + 1:3 * h + 2] * osel[head_rows, :]
                        + gate[q_rows, 3 * h + 2:3 * h + 3] * owin_ref[head_rows, :])
        for g in range(2):
            ya_ref[0, q_rows, 128 * g:128 * (g + 1)] = jnp.where(
                lane128 < HEAD_DIM, pltpu.roll(comb[2 * g], 64, 1), comb[2 * g + 1])


def _nsa(aq, ag, kc, vc, ksel, akv, kwin, tq):
    b, s, _ = aq.shape
    nc = kc.shape[1]
    rows = N_HEADS * tq
    whole = lambda w, blk: pl.BlockSpec((1, s, w), lambda i, j, blk=blk: (i, 0, blk))
    return pl.pallas_call(
        functools.partial(_nsa_kernel, tq=tq, seq=s),
        grid=(b, s // tq),
        in_specs=[
            pl.BlockSpec((1, tq, 256), lambda i, j: (i, j, 0)),
            pl.BlockSpec((1, tq, 128), lambda i, j: (i, j, 0)),
            pl.BlockSpec((1, nc, 128), lambda i, j: (i, 0, 0)),
            pl.BlockSpec((1, nc, 128), lambda i, j: (i, 0, 0)),
            whole(256, 0),
            whole(128, 0),
            whole(128, 0),
            whole(128, 1),
            pl.BlockSpec((4, N_HEADS * NSA_SUB, NSA_SUB), lambda i, j: (0, 0, 0)),
        ],
        out_specs=pl.BlockSpec((1, tq, 256), lambda i, j: (i, j, 0)),
        out_shape=jax.ShapeDtypeStruct((b, s, 256), F32),
        scratch_shapes=[
            pltpu.VMEM((rows, 256), BF16),
            pltpu.VMEM((rows, 128), F32),
            pltpu.VMEM((rows, 128), F32),
            pltpu.VMEM((rows, 128), F32),
            pltpu.VMEM((rows, 128), F32),
            pltpu.SMEM((1,), F32),
        ],
        compiler_params=_params("arbitrary", "arbitrary"),
        name="nsa_attention",
    )(aq, ag, kc, vc, ksel, akv, kwin, akv, _window_mask_table())


def _window_mask_table():
    r = np.arange(N_HEADS * NSA_SUB)[:, None] % NSA_SUB
    j = np.arange(NSA_SUB)[None, :]
    keep = np.stack([np.zeros_like(j <= r), j <= r, np.ones_like(j <= r), j > r])
    return jnp.asarray(np.where(keep, 0.0, NEG_INF), F32)


def _swa_kernel(sink_ref, bq_ref, bkv_ref, yb_ref, *, tq, seq):
    qi = pl.program_id(1)
    sub = SWA_WINDOW
    nk = 2 * sub
    lane128 = lax.broadcasted_iota(jnp.int32, (1, 128), 1)
    for u in range(tq // sub):
        qstart = qi * tq + u * sub
        k0 = pl.multiple_of(jnp.maximum(qstart - sub, 0), sub)
        kk = bkv_ref[0, pl.ds(k0, nk), 0:128]
        vv = bkv_ref[0, pl.ds(k0, nk), 128:256]
        t_lane = lax.broadcasted_iota(jnp.int32, (1, sub), 1) + qstart
        s_col = lax.broadcasted_iota(jnp.int32, (nk, 1), 0) + k0
        rel = t_lane - s_col
        rel_f = rel.astype(F32)
        mask = (rel >= 0) & (rel < SWA_WINDOW)
        for g in range(2):
            qg = bq_ref[0, u * sub:(u + 1) * sub, 128 * g:128 * (g + 1)]
            outs = []
            for p in range(2):
                h = SWA_HEAD_ORDER[2 * g + p]
                in_half = (lane128 >= HEAD_DIM * p) & (lane128 < HEAD_DIM * (p + 1))
                qh = jnp.where(in_half, qg, jnp.zeros_like(qg))
                s = _dot_t(kk, qh) * QK_SCALE - SLOPES[h] * rel_f
                s = jnp.where(mask, s, NEG_INF)
                sink = sink_ref[h]
                m = jnp.maximum(jnp.max(s, axis=0, keepdims=True), sink)
                e = jnp.where(mask, jnp.exp(s - m), 0.0)
                den = jnp.sum(e, axis=0, keepdims=True) + jnp.exp(sink - m)
                pr = e * (1.0 / jnp.maximum(den, TINY))
                outs.append(lax.dot_general(pr.astype(BF16), vv, (((0,), (0,)), ((), ())),
                                            preferred_element_type=F32))
            yb_ref[0, u * sub:(u + 1) * sub, 128 * g:128 * (g + 1)] = jnp.where(
                lane128 < HEAD_DIM, outs[0], outs[1])


def _swa(sink, bq, bkv, tq):
    b, s, _ = bq.shape
    return pl.pallas_call(
        functools.partial(_swa_kernel, tq=tq, seq=s),
        grid=(b, s // tq),
        in_specs=[
            pl.BlockSpec(memory_space=pltpu.SMEM),
            pl.BlockSpec((1, tq, 256), lambda i, j: (i, j, 0)),
            pl.BlockSpec((1, s, 256), lambda i, j: (i, 0, 0)),
        ],
        out_specs=pl.BlockSpec((1, tq, 256), lambda i, j: (i, j, 0)),
        out_shape=jax.ShapeDtypeStruct((b, s, 256), F32),
        compiler_params=_params("arbitrary", "arbitrary"),
        name="swa_attention",
    )(sink, bq, bkv)


def _stick_kernel(cq_ref, ck_ref, cv_ref, yc_ref, qm_ref, carry_ref, o_ref, *, tq, seq):
    qi = pl.program_id(1)
    qstart = qi * tq
    lane256 = lax.broadcasted_iota(jnp.int32, (1, 256), 1)
    head_of_lane = lane256 >> 6
    rows = N_HEADS * tq
    q = cq_ref[0].astype(F32) * QK_SCALE
    for h in range(N_HEADS):
        qm_ref[h * tq:(h + 1) * tq, :] = jnp.where(head_of_lane == h, q, 0.0).astype(BF16)
    jj = lax.broadcasted_iota(jnp.int32, (tq, 2 * tq), 0)
    ss = lax.broadcasted_iota(jnp.int32, (tq, 2 * tq), 1)
    upper = ((jj > ss) | (ss >= tq)).astype(BF16)
    upper2 = jnp.concatenate([upper, upper], axis=0)
    t_in = lax.broadcasted_iota(jnp.int32, (rows, tq), 0) & (tq - 1)
    strict = lax.broadcasted_iota(jnp.int32, (rows, tq), 1) < t_in

    def tile(k0, carry, acc, valid):
        diagonal = carry is None
        kt = ck_ref[0, pl.ds(k0, tq), :]
        vt = cv_ref[0, pl.ds(k0, tq), :]
        z = _dot_t(qm_ref[...], kt)
        soft = jnp.log(1.0 + jnp.exp(-jnp.abs(z)))
        log_beta = jnp.minimum(z, 0.0) - soft
        log_1m = log_beta - z
        if diagonal:
            log_1m = jnp.where(strict, log_1m, 0.0)
        hi = log_1m.astype(BF16)
        lo = (log_1m - hi.astype(F32)).astype(BF16)
        sums = _dot(jnp.concatenate([hi, lo], axis=1), upper2)
        suffix = sums[:, :tq] if diagonal else sums[:, :tq] + carry
        a = jnp.exp(log_beta + suffix)
        if diagonal:
            a = jnp.where(strict, a, 0.0)
        if valid is not None:
            a = jnp.where(valid, a, 0.0)
        a_b = a.astype(BF16)
        a_cat = jnp.concatenate([a_b[h * tq:(h + 1) * tq, :] for h in range(N_HEADS)], axis=1)
        v_bd = jnp.concatenate([jnp.where(head_of_lane == h, vt, jnp.zeros_like(vt))
                                for h in range(N_HEADS)], axis=0)
        acc = acc + _dot(a_cat, v_bd)
        return (sums[:, tq:] if diagonal else carry + sums[:, tq:]), acc

    carry, acc = tile(pl.multiple_of(qstart, tq), None, jnp.zeros((tq, 256), F32), None)
    for d in range(1, STICK_EAGER_TILES + 1):
        j = qi - d
        carry, acc = tile(pl.multiple_of(jnp.maximum(j, 0) * tq, tq), carry, acc, j >= 0)
    carry_ref[...] = carry
    o_ref[...] = acc

    def cond(state):
        j, worst = state
        return (j >= 0) & (worst > STICK_EXIT)

    def body(state):
        j, _ = state
        new_carry, new_acc = tile(pl.multiple_of(j * tq, tq), carry_ref[...], o_ref[...], None)
        carry_ref[...] = new_carry
        o_ref[...] = new_acc
        return j - 1, jnp.max(new_carry, axis=0, keepdims=True)[0, 0]

    lax.while_loop(cond, body, (qi - 1 - STICK_EAGER_TILES, jnp.max(carry, axis=0, keepdims=True)[0, 0]))
    yc_ref[0] = o_ref[...]


def _stick(cq, ck, cv, tq):
    b, s, _ = cq.shape
    return pl.pallas_call(
        functools.partial(_stick_kernel, tq=tq, seq=s),
        grid=(b, s // tq),
        in_specs=[
            pl.BlockSpec((1, tq, 256), lambda i, j: (i, j, 0)),
            pl.BlockSpec((1, s, 256), lambda i, j: (i, 0, 0)),
            pl.BlockSpec((1, s, 256), lambda i, j: (i, 0, 0)),
        ],
        out_specs=pl.BlockSpec((1, tq, 256), lambda i, j: (i, j, 0)),
        out_shape=jax.ShapeDtypeStruct((b, s, 256), F32),
        scratch_shapes=[
            pltpu.VMEM((N_HEADS * tq, 256), BF16),
            pltpu.VMEM((N_HEADS * tq, tq), F32),
            pltpu.VMEM((tq, 256), F32),
        ],
        compiler_params=_params("arbitrary", "arbitrary"),
        name="stick_breaking",
    )(cq, ck, cv)


def _ret_kernel(dq_ref, dk_ref, dv_ref, dmat_ref, xi_ref, zeta_ref, decay_ref, bd_ref, yd_ref, r_ref):
    n = pl.program_id(1)

    @pl.when(n == 0)
    def _():
        r_ref[...] = jnp.zeros(r_ref.shape, F32)

    c = RET_CHUNK
    lane256 = lax.broadcasted_iota(jnp.int32, (1, 256), 1)
    head_of_lane = lane256 >> 6
    r = r_ref[...]
    for u in range(dq_ref.shape[1] // c):
        rows = slice(u * c, (u + 1) * c)
        q = dq_ref[0, rows, :].astype(F32) * QK_SCALE
        k = dk_ref[0, rows, :]
        v = dv_ref[0, rows, :]
        o = _dot(q.astype(BF16), r.astype(BF16)) * xi_ref[...]
        for h in range(N_HEADS):
            qh = jnp.where(head_of_lane == h, q, 0.0).astype(BF16)
            s = _dot_t(qh, k) * dmat_ref[h]
            o = o + jnp.where(head_of_lane == h, _dot(s.astype(BF16), v), 0.0)
        mu = jnp.zeros((c, 256), F32)
        for h in range(N_HEADS):
            in_h = head_of_lane == h
            mu = mu + jnp.where(in_h, jnp.sum(jnp.where(in_h, o, 0.0), axis=-1, keepdims=True), 0.0)
        cen = o - mu * (1.0 / HEAD_DIM)
        var = jnp.zeros((c, 256), F32)
        for h in range(N_HEADS):
            in_h = head_of_lane == h
            var = var + jnp.where(in_h, jnp.sum(jnp.where(in_h, cen * cen, 0.0), axis=-1, keepdims=True), 0.0)
        yd_ref[0, rows, :] = cen * lax.rsqrt(var * (1.0 / HEAD_DIM) + LN_EPS)
        kz = (k.astype(F32) * zeta_ref[...]).astype(BF16)
        upd = lax.dot_general(kz, v, (((0,), (0,)), ((), ())), preferred_element_type=F32)
        r = r * decay_ref[...] + upd * bd_ref[...]
    r_ref[...] = r


def _retention(dq, dk, dv, consts):
    b, s, _ = dq.shape
    c = RET_CHUNK
    dmat, xi, zeta, decay, bd = consts
    step = RET_CHUNKS_PER_STEP * c
    tile = pl.BlockSpec((1, step, 256), lambda i, j: (i, j, 0))
    return pl.pallas_call(
        _ret_kernel,
        grid=(b, s // step),
        in_specs=[
            tile, tile, tile,
            pl.BlockSpec((N_HEADS, c, c), lambda i, j: (0, 0, 0)),
            pl.BlockSpec((c, 256), lambda i, j: (0, 0)),
            pl.BlockSpec((c, 256), lambda i, j: (0, 0)),
            pl.BlockSpec((256, 256), lambda i, j: (0, 0)),
            pl.BlockSpec((256, 256), lambda i, j: (0, 0)),
        ],
        out_specs=tile,
        out_shape=jax.ShapeDtypeStruct((b, s, 256), F32),
        scratch_shapes=[pltpu.VMEM((256, 256), F32)],
        compiler_params=_params("arbitrary", "arbitrary"),
        name="retention",
    )(dq, dk, dv, dmat, xi, zeta, decay, bd)


def _retention_consts():
    c = RET_CHUNK
    log_g = jnp.log(1.0 - jnp.asarray(2.0 ** (-5.0 - np.arange(N_HEADS)), dtype=F32))
    i = jnp.arange(c)
    diff = (i[:, None] - i[None, :]).astype(F32)
    dmat = jnp.where(diff >= 0, jnp.exp(log_g[:, None, None] * jnp.maximum(diff, 0.0)), 0.0)
    zeta = jnp.exp(log_g[:, None] * (c - 1 - i)[None, :].astype(F32))
    xi = jnp.exp(log_g[:, None] * (i + 1)[None, :].astype(F32))
    g_chunk = jnp.exp(log_g * c)
    lanes = lambda hc: jnp.repeat(hc.T, HEAD_DIM, axis=1)
    head = np.arange(256) // HEAD_DIM
    bd = jnp.asarray(head[:, None] == head[None, :], F32)
    decay = jnp.repeat(g_chunk, HEAD_DIM)[:, None] * jnp.ones((1, 256), F32)
    return dmat, lanes(xi), lanes(zeta), decay, bd


def _merge_kernel(x_ref, mod_ref, g_ref, ya_ref, yb_ref, yc_ref, yd_ref, z_ref,
                  wm_ref, wbr_ref, wout_ref, fg_ref, o_ref, *, d, final):
    mod = mod_ref[0]
    tm = x_ref.shape[1]
    part = tm // MERGE_ROW_PARTS
    for r in range(MERGE_ROW_PARTS):
        rows = slice(r * part, (r + 1) * part)
        xf = x_ref[0, rows, :]
        h = _modulated_norm(xf, g_ref[...], mod, d).astype(BF16)
        merged = None
        for i, y_ref in enumerate((ya_ref, yb_ref, yc_ref, yd_ref)):
            yi = (y_ref[0, rows, :] * _silu(z_ref[0, rows, 256 * i:256 * (i + 1)])).astype(BF16)
            term = _sigmoid(_dot(h, wm_ref[i])) * _dot(yi, wbr_ref[i])
            merged = term if merged is None else merged + term
        out = xf + mod[:, 2 * d:3 * d] * _dot(merged.astype(BF16), wout_ref[...])
        if final:
            out = out * lax.rsqrt(jnp.mean(out * out, axis=-1, keepdims=True) + RMS_EPS) * fg_ref[...]
        o_ref[0, rows, :] = out


def _merge(x, mod, g, ys, z, wm, wbr, wout, final_g, final, tm):
    b, s, d = x.shape
    ytile = pl.BlockSpec((1, tm, 256), lambda i, j: (i, j, 0))
    return pl.pallas_call(
        functools.partial(_merge_kernel, d=d, final=final),
        grid=(b, s // tm),
        in_specs=[
            pl.BlockSpec((1, tm, d), lambda i, j: (i, j, 0)),
            pl.BlockSpec((1, 1, 3 * d), lambda i, j: (i, 0, 0)),
            pl.BlockSpec((1, d), lambda i, j: (0, 0)),
            ytile, ytile, ytile, ytile,
            pl.BlockSpec((1, tm, 4 * 256), lambda i, j: (i, j, 0)),
            pl.BlockSpec((4, d, d), lambda i, j: (0, 0, 0)),
            pl.BlockSpec((4, 256, d), lambda i, j: (0, 0, 0)),
            pl.BlockSpec((d, d), lambda i, j: (0, 0)),
            pl.BlockSpec((1, d), lambda i, j: (0, 0)),
        ],
        out_specs=pl.BlockSpec((1, tm, d), lambda i, j: (i, j, 0)),
        out_shape=jax.ShapeDtypeStruct((b, s, d), F32),
        compiler_params=_params("arbitrary", "arbitrary"),
        name="merge_out",
    )(x, mod, g.reshape(1, d), *ys, z, wm, wbr, wout, final_g.reshape(1, d))


def _proj_columns():
    a, bb, cc, dd = 0, 908, 1676, 2700
    perm = np.concatenate([np.arange(64) + 64 * h for h in SWA_HEAD_ORDER])
    pad = lambda n: np.full((n,), -1)
    cols = np.concatenate([
        np.arange(a, a + 256),
        np.arange(a + 256, a + 384),
        np.arange(a + 384, a + 640),
        np.arange(a + 640, a + 652), pad(116),
        np.arange(a + 652, a + 908),
        bb + 512 + perm,
        np.arange(cc + 768, cc + 1024),
        np.arange(dd + 768, dd + 1024),
        bb + perm,
        np.arange(bb + 256, bb + 512),
        np.arange(cc, cc + 768),
        np.arange(dd, dd + 768),
    ])
    assert cols.shape[0] == PROJ_W
    return cols, perm


def _take_columns(w, cols):
    pieces, i = [], 0
    while i < len(cols):
        j = i + 1
        if cols[i] < 0:
            while j < len(cols) and cols[j] < 0:
                j += 1
            pieces.append(jnp.zeros(w.shape[:-1] + (j - i,), w.dtype))
        else:
            while j < len(cols) and cols[j] == cols[j - 1] + 1:
                j += 1
            pieces.append(w[..., int(cols[i]):int(cols[j - 1]) + 1])
        i = j
    return jnp.concatenate(pieces, axis=-1)


def _position_features(s):
    pos = np.arange(s)
    feat = np.zeros((s, 64), np.float32)
    feat[:, 0] = SEL_BLOCK * (pos // SEL_BLOCK)
    feat[:, 1] = pos % SEL_BLOCK
    onehot = (pos[:, None] // SEL_BLOCK == np.arange(128)[None, :]).astype(np.float32)
    return jnp.asarray(feat, BF16), jnp.asarray(onehot, BF16)


def kernel(x, c, w_ada, b_ada, norm_g, w_in, cmp_pos, cmp_w1, cmp_w2, sink, w_merge, w_br, w_out, final_g):
    b, s, d = x.shape
    depth = w_ada.shape[0]
    tm = min(ROW_TILE, s)

    cols, perm = _proj_columns()
    w_in_p = _take_columns(w_in, cols).astype(BF16)
    wm_b = w_merge.astype(BF16)
    wbr_b = w_br.at[:, 1].set(w_br[:, 1][:, perm, :]).astype(BF16)
    wout_b = w_out.astype(BF16)
    pos_l, w1bd, w2bd = _compress_weights(cmp_pos, cmp_w1, cmp_w2)
    feat, onehot = _position_features(s)
    kconst = jnp.concatenate([jnp.zeros((s, HEAD_DIM), BF16), feat, onehot], axis=-1)
    ret_consts = _retention_consts()
    mods = _modulation(c, w_ada, b_ada)

    for l in range(depth):
        p = _project(x, mods[l], norm_g[l], w_in_p[l], kconst, tm)
        kc, vc = _compress(p["acmp"], pos_l[l], w1bd[l], w2bd[l])
        y_a = _nsa(p["aq"], p["ag"], kc, vc, p["ksel"], p["akv"], p["kwin"], NSA_TQ)
        y_b = _swa(sink[l], p["bq"], p["bkv"], SWA_TQ)
        y_c = _stick(p["cq"], p["ck"], p["cv"], STICK_TQ)
        y_d = _retention(p["dq"], p["dk"], p["dv"], ret_consts)
        x = _merge(x, mods[l], norm_g[l], (y_a, y_b, y_c, y_d), p["z"], wm_b[l], wbr_b[l], wout_b[l],
                   final_g, l == depth - 1, tm)
    return x
```

```python
import functools

import numpy as np
import jax
import jax.numpy as jnp
from jax import lax
from jax.experimental import pallas as pl
from jax.experimental.pallas import tpu as pltpu

F32 = jnp.float32
BF16 = jnp.bfloat16

HEAD_DIM = 64
N_HEADS = 4
BRANCH_W = N_HEADS * HEAD_DIM
CMP_BLOCK = 32
SEL_BLOCK = 64
SEL_TOPK = 16
NSA_WINDOW = 512
NSA_TK = 512
SWA_WINDOW = 128
RET_CHUNK = 128
RMS_EPS = 1e-6
LN_EPS = 1e-5
NEG_INF = -1e30
TINY = 1e-30
FORCED_SCORE = 1e4
QK_SCALE = HEAD_DIM ** -0.5
STICK_EXIT = -110.0
STICK_EAGER_TILES = 2
MAX_BOUND_GAP = 60.0
BOUND_SLACK = 1.001

ROW_TILE = 512
MERGE_ROW_PARTS = 2
NSA_TQ = 256
NSA_SUB = 128
SWA_TQ = 512
RET_CHUNKS_PER_STEP = 4
STICK_TQ = 128
STICK_SUBTILES = 4

VMEM_LIMIT = 56 * 1024 * 1024
SLOPES = tuple(float(2.0 ** (-8.0 * (h + 1) / N_HEADS)) for h in range(N_HEADS))

PROJ_OUTS = (
    ("aq", 0, 256, BF16),
    ("acmp", 256, 128, F32),
    ("akv", 384, 256, BF16),
    ("ag", 640, 128, F32),
    ("z", 768, 1024, F32),
    ("bq", 1792, 256, BF16),
    ("bkv", 2048, 256, BF16),
    ("cq", 2304, 256, BF16),
    ("ck", 2560, 256, BF16),
    ("cv", 2816, 256, BF16),
    ("dq", 3072, 256, BF16),
    ("dk", 3328, 256, BF16),
    ("dv", 3584, 256, BF16),
)
PROJ_W = 3840
SWA_HEAD_ORDER = (0, 2, 1, 3)


def _dot(a, b):
    return jnp.dot(a, b, preferred_element_type=F32)


def _dot_t(a, b):
    return lax.dot_general(a, b, (((1,), (1,)), ((), ())), preferred_element_type=F32)


def _dot_split(a, b):
    hi = a.astype(BF16)
    lo = (a - hi.astype(F32)).astype(BF16)
    return _dot(hi, b) + _dot(lo, b)


def _sigmoid(v):
    return 1.0 / (1.0 + jnp.exp(-v))


def _silu(v):
    return v * _sigmoid(v)


def _params(*sem):
    return pltpu.CompilerParams(dimension_semantics=sem, vmem_limit_bytes=VMEM_LIMIT)


def _mod_kernel(c_ref, w_ref, b_ref, o_ref):
    cc = c_ref[...]
    o_ref[0] = _dot(_silu(cc).astype(BF16), w_ref[0].astype(BF16)) + b_ref[0]


def _modulation(c, w_ada, b_ada):
    depth, d, n = w_ada.shape
    b = c.shape[0]
    rows = 8
    cp = jnp.zeros((rows, d), F32).at[:b].set(c)
    tn = 512
    out = pl.pallas_call(
        _mod_kernel,
        grid=(depth, n // tn),
        in_specs=[
            pl.BlockSpec((rows, d), lambda l, j: (0, 0)),
            pl.BlockSpec((1, d, tn), lambda l, j: (l, 0, j)),
            pl.BlockSpec((1, 1, tn), lambda l, j: (l, 0, j)),
        ],
        out_specs=pl.BlockSpec((1, rows, tn), lambda l, j: (l, 0, j)),
        out_shape=jax.ShapeDtypeStruct((depth, rows, n), F32),
        compiler_params=_params("arbitrary", "arbitrary"),
        name="adaln_mod",
    )(cp, w_ada, b_ada.reshape(depth, 1, n))
    return out[:, :b].reshape(depth, b, 1, n)


def _modulated_norm(xf, g, mod, d):
    ms = jnp.mean(xf * xf, axis=-1, keepdims=True)
    y = xf * lax.rsqrt(ms + RMS_EPS) * g
    return y * (1.0 + mod[:, d:2 * d]) + mod[:, 0:d]


def _proj_kernel(x_ref, mod_ref, g_ref, w_ref, kconst_ref, *out_refs, d):
    h = _modulated_norm(x_ref[0], g_ref[...], mod_ref[0], d).astype(BF16)
    ksel_ref, kwin_ref = out_refs[len(PROJ_OUTS):]
    for (name, start, width, dtype), o_ref in zip(PROJ_OUTS, out_refs):
        for c0 in range(0, width, 256):
            cw = min(256, width - c0)
            acc = _dot(h, w_ref[:, start + c0:start + c0 + cw])
            o_ref[0, :, c0:c0 + cw] = acc.astype(dtype)
            if name == "akv":
                ksel_ref[0] = kconst_ref[...]
                ksel_ref[0, :, 0:HEAD_DIM] = acc[:, 0:HEAD_DIM].astype(BF16)
                kwin_ref[0] = kconst_ref[:, 0:128]
                kwin_ref[0, :, 0:HEAD_DIM] = acc[:, 128:128 + HEAD_DIM].astype(BF16)


def _project(x, mod, g, w_p, kconst, tm):
    b, s, d = x.shape
    widths = [(wd, dt) for (_, _, wd, dt) in PROJ_OUTS] + [(256, BF16), (128, BF16)]
    out_shapes = [jax.ShapeDtypeStruct((b, s, wd), dt) for wd, dt in widths]
    out_specs = [pl.BlockSpec((1, tm, wd), lambda i, j: (i, j, 0)) for wd, _ in widths]
    outs = pl.pallas_call(
        functools.partial(_proj_kernel, d=d),
        grid=(b, s // tm),
        in_specs=[
            pl.BlockSpec((1, tm, d), lambda i, j: (i, j, 0)),
            pl.BlockSpec((1, 1, 3 * d), lambda i, j: (i, 0, 0)),
            pl.BlockSpec((1, d), lambda i, j: (0, 0)),
            pl.BlockSpec((d, PROJ_W), lambda i, j: (0, 0)),
            pl.BlockSpec((tm, 256), lambda i, j: (j, 0)),
        ],
        out_specs=out_specs,
        out_shape=out_shapes,
        compiler_params=_params("arbitrary", "arbitrary"),
        name="in_proj",
    )(x, mod, g.reshape(1, d), w_p, kconst)
    names = [name for (name, _, _, _) in PROJ_OUTS] + ["ksel", "kwin"]
    return dict(zip(names, outs))


def _compress_kernel(acmp_ref, pos_ref, w1_ref, w2_ref, kc_ref, vc_ref, *, nc):
    half = nc // 2
    for parity in range(2):
        hid = jnp.zeros((half, 128), F32)
        for i in range(CMP_BLOCK):
            z = acmp_ref[0, pl.ds(parity * CMP_BLOCK + i, half, stride=2 * CMP_BLOCK), :]
            hid = hid + _dot((z + pos_ref[i:i + 1, :]).astype(BF16), w1_ref[i])
        act = _silu(hid).astype(BF16)
        out_rows = slice(parity * half, (parity + 1) * half)
        kc_ref[0, out_rows, :] = _dot(act, w2_ref[0]).astype(BF16)
        vc_ref[0, out_rows, :] = _dot(act, w2_ref[1]).astype(BF16)


def _compress(acmp, pos, w1bd, w2bd):
    b, s, _ = acmp.shape
    nc = s // CMP_BLOCK
    spec_o = pl.BlockSpec((1, nc, 128), lambda i: (i, 0, 0))
    return pl.pallas_call(
        functools.partial(_compress_kernel, nc=nc),
        grid=(b,),
        in_specs=[
            pl.BlockSpec((1, s, 128), lambda i: (i, 0, 0)),
            pl.BlockSpec((CMP_BLOCK, 128), lambda i: (0, 0)),
            pl.BlockSpec((CMP_BLOCK, 128, 128), lambda i: (0, 0, 0)),
            pl.BlockSpec((2, 128, 128), lambda i: (0, 0, 0)),
        ],
        out_specs=[spec_o, spec_o],
        out_shape=[jax.ShapeDtypeStruct((b, nc, 128), BF16)] * 2,
        compiler_params=_params("arbitrary"),
        name="nsa_compress",
    )(acmp, pos, w1bd, w2bd)


def _compress_weights(cmp_pos, cmp_w1, cmp_w2):
    depth = cmp_pos.shape[0]
    hd = HEAD_DIM
    pos = jnp.concatenate([cmp_pos[:, 0], cmp_pos[:, 1]], axis=-1)
    w1 = cmp_w1.reshape(depth, 2, CMP_BLOCK, hd, hd)
    z = jnp.zeros((depth, CMP_BLOCK, hd, hd), F32)
    w1bd = jnp.concatenate([jnp.concatenate([w1[:, 0], z], axis=-1),
                            jnp.concatenate([z, w1[:, 1]], axis=-1)], axis=-2).astype(BF16)
    z2 = jnp.zeros((depth, hd, hd), F32)
    w2k = jnp.concatenate([jnp.concatenate([cmp_w2[:, 0], z2], axis=-1),
                           jnp.concatenate([z2, z2], axis=-1)], axis=-2)
    w2v = jnp.concatenate([jnp.concatenate([z2, z2], axis=-1),
                           jnp.concatenate([z2, cmp_w2[:, 1]], axis=-1)], axis=-2)
    return pos, w1bd, jnp.stack([w2k, w2v], axis=1).astype(BF16)


def _nsa_kernel(aq_ref, ag_ref, kc_ref, vc_ref, ksel_ref, vsel_ref, kwin_ref, vwin_ref, wmask_ref, ya_ref,
                qaug_ref, mx_ref, sm_ref, acc_ref, owin_ref, kmax_ref, *, tq, seq):
    qi = pl.program_id(1)
    qstart = qi * tq
    nc = seq // CMP_BLOCK
    half = nc // 2
    ns = seq // SEL_BLOCK
    t_lane = lax.broadcasted_iota(jnp.int32, (1, tq), 1) + qstart
    lane128 = lax.broadcasted_iota(jnp.int32, (1, 128), 1)
    nsub = tq // NSA_SUB
    sub_rows_log2 = (N_HEADS * NSA_SUB).bit_length() - 1
    row0 = lambda u, h: (u * N_HEADS + h) * NSA_SUB
    rows = N_HEADS * tq
    row_id = lax.broadcasted_iota(jnp.int32, (rows, 1), 0)
    t_row = qstart + (row_id >> sub_rows_log2) * NSA_SUB + (row_id & (NSA_SUB - 1))

    @pl.when(qi == 0)
    def _():
        def chunk(c, best):
            kk = ksel_ref[0, pl.ds(pl.multiple_of(c * NSA_TK, NSA_TK), NSA_TK), 0:128].astype(F32)
            sq = jnp.sum(jnp.where(lane128 < HEAD_DIM, kk * kk, 0.0), axis=-1, keepdims=True)
            return jnp.maximum(best, jnp.max(sq, axis=0, keepdims=True))
        best = lax.fori_loop(0, seq // NSA_TK, chunk, jnp.zeros((1, 1), F32))
        kmax_ref[0] = jnp.sqrt(best)[0, 0]

    row_c = lax.broadcasted_iota(jnp.int32, (nc, 1), 0)
    blk_c = jnp.where(row_c < half, 2 * row_c, 2 * (row_c - half) + 1)
    dist = t_lane - (blk_c * CMP_BLOCK + (CMP_BLOCK - 1))
    dist_f = dist.astype(F32)
    mask_c = dist >= 0
    kc = kc_ref[0]
    vc = vc_ref[0]
    psum = jnp.zeros((nc, tq), F32)
    ocmp = []
    for h in range(N_HEADS):
        grp = aq_ref[0, :, 128 * (h // 2):128 * (h // 2) + 128].astype(F32)
        if h % 2 == 1:
            grp = pltpu.roll(grp, 64, 1)
        left = jnp.where(lane128 < HEAD_DIM, grp * QK_SCALE,
                         jnp.where(lane128 < HEAD_DIM + 2, SLOPES[h], 0.0)).astype(BF16)
        for u in range(nsub):
            qaug_ref[row0(u, h):row0(u, h) + NSA_SUB, 0:128] = left[u * NSA_SUB:(u + 1) * NSA_SUB]
        s = _dot_t(kc, left) - SLOPES[h] * dist_f
        s = jnp.where(mask_c, s, NEG_INF)
        m = jnp.max(s, axis=0, keepdims=True)
        e = jnp.where(mask_c, jnp.exp(s - m), 0.0)
        den = jnp.sum(e, axis=0, keepdims=True)
        p = e * (1.0 / jnp.maximum(den, TINY))
        psum = psum + p
        ocmp.append(lax.dot_general(p.astype(BF16), vc, (((0,), (0,)), ((), ())),
                                    preferred_element_type=F32))

    def fold(sc):
        m = mx_ref[...]
        for g in range(sc.shape[1] // 128):
            m = jnp.maximum(m, sc[:, 128 * g:128 * (g + 1)])
        mx_ref[...] = m

    def accumulate(sc, v):
        m = mx_ref[...]
        e = jnp.exp(sc - jnp.concatenate([m] * (sc.shape[1] // 128), axis=1))
        part = sm_ref[...]
        for g in range(sc.shape[1] // 128):
            part = part + e[:, 128 * g:128 * (g + 1)]
        sm_ref[...] = part
        acc_ref[...] += _dot(e.astype(BF16), v)

    own0 = pl.multiple_of(qstart, tq)

    def own_scores():
        sc = _dot_t(qaug_ref[:, 0:128], ksel_ref[0, pl.ds(own0, tq), 0:128])
        return jnp.where(lax.broadcasted_iota(jnp.int32, (1, tq), 1) + qstart <= t_row, sc, NEG_INF)

    def reset_sums():
        sm_ref[...] = jnp.zeros(sm_ref.shape, F32)
        acc_ref[...] = jnp.zeros(acc_ref.shape, F32)

    qpart = qaug_ref[:, 0:128].astype(F32)
    qnorm = jnp.sqrt(jnp.sum(jnp.where(lane128 < HEAD_DIM, qpart * qpart, 0.0), axis=-1, keepdims=True))
    reach = qnorm * (kmax_ref[0] * BOUND_SLACK)
    bound_ok = jnp.max(reach, axis=0, keepdims=True)[0, 0] * 2.0 <= MAX_BOUND_GAP
    head_of_row = (row_id >> (NSA_SUB.bit_length() - 1)) & (N_HEADS - 1)
    slope = jnp.full((rows, 1), SLOPES[0], F32)
    for h in range(1, N_HEADS):
        slope = jnp.where(head_of_row == h, SLOPES[h], slope)
    mx_ref[...] = jnp.broadcast_to(reach + slope * t_row.astype(F32), mx_ref.shape)
    reset_sums()
    accumulate(own_scores(), vsel_ref[0, pl.ds(own0, tq), :])

    n_win = NSA_WINDOW + NSA_SUB
    full_chunks = NSA_WINDOW // NSA_SUB
    for u in range(nsub):
        q0 = qstart + u * NSA_SUB
        w0 = pl.multiple_of(jnp.clip(q0 - NSA_WINDOW, 0, seq - n_win), NSA_SUB)
        sub_rows = slice(row0(u, 0), row0(u, 0) + N_HEADS * NSA_SUB)
        sw = _dot_t(qaug_ref[sub_rows, 0:128], kwin_ref[0, pl.ds(w0, n_win), :])
        diag_chunk = (q0 - w0) >> (NSA_SUB.bit_length() - 1)
        pieces = []
        for c in range(n_win // NSA_SUB):
            d = diag_chunk - c
            pattern = jnp.where(d == 0, 1, jnp.where((d > 0) & (d < full_chunks), 2,
                                                     jnp.where(d == full_chunks, 3, 0)))
            pieces.append(sw[:, c * NSA_SUB:(c + 1) * NSA_SUB] + wmask_ref[pattern])
        sw = jnp.concatenate(pieces, axis=1)
        ew = jnp.exp(sw - jnp.max(sw, axis=-1, keepdims=True))
        owin_ref[sub_rows, :] = (_dot(ew.astype(BF16), vwin_ref[0, pl.ds(w0, n_win), :])
                                 * (1.0 / jnp.sum(ew, axis=-1, keepdims=True)))

    imp = psum[:half] + psum[half:]
    blk = lax.broadcasted_iota(jnp.int32, (ns, 1), 0)
    cur = t_lane >> 6
    future = blk * SEL_BLOCK > t_lane
    forced = (blk == 0) | (blk == cur) | (blk == cur - 1)
    score = jnp.where(forced, FORCED_SCORE, jnp.where(future, -1.0, imp))
    blk_f = blk.astype(F32)
    sel_t = jnp.zeros((ns, tq), F32)
    for _ in range(min(SEL_TOPK, ns)):
        mx = jnp.max(score, axis=0, keepdims=True)
        first = jnp.min(jnp.where(score == mx, blk_f, float(ns)), axis=0, keepdims=True)
        pick = blk_f == first
        sel_t = jnp.where(pick, 1.0, sel_t)
        score = jnp.where(pick, -jnp.inf, score)
    sel_f = jnp.transpose(sel_t)
    negmask = jnp.where(sel_f > 0.5, 0.0, NEG_INF).astype(BF16)
    if ns < 128:
        negmask = jnp.concatenate([negmask, jnp.zeros((tq, 128 - ns), BF16)], axis=1)
    for u in range(nsub):
        for h in range(N_HEADS):
            qaug_ref[row0(u, h):row0(u, h) + NSA_SUB, 128:256] = negmask[u * NSA_SUB:(u + 1) * NSA_SUB]

    if ns < 128:
        sel_f = jnp.concatenate([sel_f, jnp.zeros((tq, 128 - ns), F32)], axis=1)
    any_blk = jnp.max(sel_f, axis=0, keepdims=True)
    blocks_per_tile = NSA_TK // SEL_BLOCK
    shift = 1
    while shift < blocks_per_tile:
        any_blk = jnp.maximum(any_blk, pltpu.roll(any_blk, 128 - shift, 1))
        shift *= 2
    tile_of_lane = lane128 >> (blocks_per_tile.bit_length() - 1)
    first_of_tile = (lane128 & (blocks_per_tile - 1)) == 0
    pow2 = lax.bitcast_convert_type(((tile_of_lane & 15) + 127) << 23, F32)
    weighted = jnp.where(first_of_tile, any_blk * pow2, 0.0)
    bits_lo = jnp.sum(jnp.where(tile_of_lane < 16, weighted, 0.0), axis=1, keepdims=True).astype(jnp.int32)[0, 0]
    bits_hi = jnp.sum(jnp.where(tile_of_lane >= 16, weighted, 0.0), axis=1, keepdims=True).astype(jnp.int32)[0, 0]

    def tile_selected(t):
        return ((jnp.where(t < 16, bits_lo, bits_hi) >> (t & 15)) & 1) == 1

    pieces_per_tile = NSA_TK // tq
    t_own = qi // pieces_per_tile
    n_pieces = qi - t_own * pieces_per_tile

    def sweep(use):
        def piece(p, carry):
            k0 = pl.multiple_of(t_own * NSA_TK + p * tq, tq)
            use(_dot_t(qaug_ref[...], ksel_ref[0, pl.ds(k0, tq), :]), k0, tq)
            return carry

        def tile(t, carry):
            def visit():
                k0 = pl.multiple_of(t * NSA_TK, NSA_TK)
                use(_dot_t(qaug_ref[...], ksel_ref[0, pl.ds(k0, NSA_TK), :]), k0, NSA_TK)
            pl.when(tile_selected(t))(visit)
            return carry

        lax.fori_loop(0, n_pieces, piece, 0)
        lax.fori_loop(0, t_own, tile, 0)

    @pl.when(jnp.logical_not(bound_ok))
    def _():
        mx_ref[...] = jnp.full(mx_ref.shape, NEG_INF, F32)
        sweep(lambda sc, k0, n: fold(sc))
        fold(own_scores())
        mx_ref[...] = jnp.broadcast_to(jnp.max(mx_ref[...], axis=-1, keepdims=True), mx_ref.shape)
        reset_sums()
        accumulate(own_scores(), vsel_ref[0, pl.ds(own0, tq), :])

    sweep(lambda sc, k0, n: accumulate(sc, vsel_ref[0, pl.ds(k0, n), :]))
    osel = acc_ref[...] * (1.0 / jnp.sum(sm_ref[...], axis=-1, keepdims=True))

    gate = _sigmoid(ag_ref[0])
    for u in range(nsub):
        q_rows = slice(u * NSA_SUB, (u + 1) * NSA_SUB)
        comb = []
        for h in range(N_HEADS):
            head_rows = slice(row0(u, h), row0(u, h) + NSA_SUB)
            comb.append(gate[q_rows, 3 * h:3 * h + 1] * ocmp[h][q_rows, :]
                        + gate[q_rows, 3 * h + 1:3 * h + 2] * osel[head_rows, :]
                        + gate[q_rows, 3 * h + 2:3 * h + 3] * owin_ref[head_rows, :])
        for g in range(2):
            ya_ref[0, q_rows, 128 * g:128 * (g + 1)] = jnp.where(
                lane128 < HEAD_DIM, pltpu.roll(comb[2 * g], 64, 1), comb[2 * g + 1])


def _nsa(aq, ag, kc, vc, ksel, akv, kwin, tq):
    b, s, _ = aq.shape
    nc = kc.shape[1]
    rows = N_HEADS * tq
    whole = lambda w, blk: pl.BlockSpec((1, s, w), lambda i, j, blk=blk: (i, 0, blk))
    return pl.pallas_call(
        functools.partial(_nsa_kernel, tq=tq, seq=s),
        grid=(b, s // tq),
        in_specs=[
            pl.BlockSpec((1, tq, 256), lambda i, j: (i, j, 0)),
            pl.BlockSpec((1, tq, 128), lambda i, j: (i, j, 0)),
            pl.BlockSpec((1, nc, 128), lambda i, j: (i, 0, 0)),
            pl.BlockSpec((1, nc, 128), lambda i, j: (i, 0, 0)),
            whole(256, 0),
            whole(128, 0),
            whole(128, 0),
            whole(128, 1),
            pl.BlockSpec((4, N_HEADS * NSA_SUB, NSA_SUB), lambda i, j: (0, 0, 0)),
        ],
        out_specs=pl.BlockSpec((1, tq, 256), lambda i, j: (i, j, 0)),
        out_shape=jax.ShapeDtypeStruct((b, s, 256), F32),
        scratch_shapes=[
            pltpu.VMEM((rows, 256), BF16),
            pltpu.VMEM((rows, 128), F32),
            pltpu.VMEM((rows, 128), F32),
            pltpu.VMEM((rows, 128), F32),
            pltpu.VMEM((rows, 128), F32),
            pltpu.SMEM((1,), F32),
        ],
        compiler_params=_params("arbitrary", "arbitrary"),
        name="nsa_attention",
    )(aq, ag, kc, vc, ksel, akv, kwin, akv, _window_mask_table())


def _window_mask_table():
    r = np.arange(N_HEADS * NSA_SUB)[:, None] % NSA_SUB
    j = np.arange(NSA_SUB)[None, :]
    keep = np.stack([np.zeros_like(j <= r), j <= r, np.ones_like(j <= r), j > r])
    return jnp.asarray(np.where(keep, 0.0, NEG_INF), F32)


def _swa_kernel(sink_ref, bq_ref, bkv_ref, bias_ref, yb_ref, *, tq, seq):
    qi = pl.program_id(1)
    sub = SWA_WINDOW
    nk = 2 * sub
    lane128 = lax.broadcasted_iota(jnp.int32, (1, 128), 1)
    for u in range(tq // sub):
        qstart = qi * tq + u * sub
        k0 = pl.multiple_of(jnp.maximum(qstart - sub, 0), sub)
        at_start = jnp.where(qstart == 0, 1, 0)
        kk = bkv_ref[0, pl.ds(k0, nk), 0:128]
        vv = bkv_ref[0, pl.ds(k0, nk), 128:256]
        for g in range(2):
            qg = bq_ref[0, u * sub:(u + 1) * sub, 128 * g:128 * (g + 1)].astype(F32) * QK_SCALE
            outs = []
            for p in range(2):
                h = SWA_HEAD_ORDER[2 * g + p]
                in_half = (lane128 >= HEAD_DIM * p) & (lane128 < HEAD_DIM * (p + 1))
                qh = jnp.where(in_half, qg, 0.0).astype(BF16)
                s = _dot_t(kk, qh) + bias_ref[h, at_start]
                sink = sink_ref[h]
                m = jnp.maximum(jnp.max(s, axis=0, keepdims=True), sink)
                e = jnp.exp(s - m)
                den = jnp.sum(e, axis=0, keepdims=True) + jnp.exp(sink - m)
                pr = e * (1.0 / jnp.maximum(den, TINY))
                outs.append(lax.dot_general(pr.astype(BF16), vv, (((0,), (0,)), ((), ())),
                                            preferred_element_type=F32))
            yb_ref[0, u * sub:(u + 1) * sub, 128 * g:128 * (g + 1)] = jnp.where(
                lane128 < HEAD_DIM, outs[0], outs[1])


def _swa_bias_table():
    sub = SWA_WINDOW
    j = np.arange(2 * sub)[:, None]
    r = np.arange(sub)[None, :]
    rel = np.stack([r + sub - j, r - j])
    ok = (rel >= 0) & (rel < sub)
    slopes = np.asarray(SLOPES, np.float32)[:, None, None, None]
    return jnp.asarray(np.where(ok[None], -slopes * rel[None].astype(np.float32), NEG_INF), F32)


def _swa(sink, bq, bkv, tq):
    b, s, _ = bq.shape
    return pl.pallas_call(
        functools.partial(_swa_kernel, tq=tq, seq=s),
        grid=(b, s // tq),
        in_specs=[
            pl.BlockSpec(memory_space=pltpu.SMEM),
            pl.BlockSpec((1, tq, 256), lambda i, j: (i, j, 0)),
            pl.BlockSpec((1, s, 256), lambda i, j: (i, 0, 0)),
            pl.BlockSpec((N_HEADS, 2, 2 * SWA_WINDOW, SWA_WINDOW), lambda i, j: (0, 0, 0, 0)),
        ],
        out_specs=pl.BlockSpec((1, tq, 256), lambda i, j: (i, j, 0)),
        out_shape=jax.ShapeDtypeStruct((b, s, 256), F32),
        compiler_params=_params("arbitrary", "arbitrary"),
        name="swa_attention",
    )(sink, bq, bkv, _swa_bias_table())


def _stick_kernel(cq_ref, ck_ref, cv_ref, yc_ref, qm_ref, carry_ref, o_ref, *, tq, nsub, seq):
    lane256 = lax.broadcasted_iota(jnp.int32, (1, 256), 1)
    head_of_lane = lane256 >> 6
    rows = N_HEADS * tq
    jj = lax.broadcasted_iota(jnp.int32, (tq, 2 * tq), 0)
    ss = lax.broadcasted_iota(jnp.int32, (tq, 2 * tq), 1)
    upper = ((jj > ss) | (ss >= tq)).astype(BF16)
    upper2 = jnp.concatenate([upper, upper], axis=0)
    t_in = lax.broadcasted_iota(jnp.int32, (rows, tq), 0) & (tq - 1)
    strict = lax.broadcasted_iota(jnp.int32, (rows, tq), 1) < t_in

    def tile(u, k0, carry, acc, valid):
        diagonal = carry is None
        kt = ck_ref[0, pl.ds(k0, tq), :]
        vt = cv_ref[0, pl.ds(k0, tq), :]
        z = _dot_t(qm_ref[u], kt)
        soft = jnp.log(1.0 + jnp.exp(-jnp.abs(z)))
        log_beta = jnp.minimum(z, 0.0) - soft
        log_1m = log_beta - z
        if diagonal:
            log_1m = jnp.where(strict, log_1m, 0.0)
        hi = log_1m.astype(BF16)
        lo = (log_1m - hi.astype(F32)).astype(BF16)
        sums = _dot(jnp.concatenate([hi, lo], axis=1), upper2)
        suffix = sums[:, :tq] if diagonal else sums[:, :tq] + carry
        a = jnp.exp(log_beta + suffix)
        if diagonal:
            a = jnp.where(strict, a, 0.0)
        if valid is not None:
            a = jnp.where(valid, a, 0.0)
        a_b = a.astype(BF16)
        a_cat = jnp.concatenate([a_b[h * tq:(h + 1) * tq, :] for h in range(N_HEADS)], axis=1)
        v_bd = jnp.concatenate([jnp.where(head_of_lane == h, vt, jnp.zeros_like(vt))
                                for h in range(N_HEADS)], axis=0)
        acc = acc + _dot(a_cat, v_bd)
        return (sums[:, tq:] if diagonal else carry + sums[:, tq:]), acc

    worst = []
    for u in range(nsub):
        qi = pl.program_id(1) * nsub + u
        q = cq_ref[0, u * tq:(u + 1) * tq, :].astype(F32) * QK_SCALE
        for h in range(N_HEADS):
            qm_ref[u, h * tq:(h + 1) * tq, :] = jnp.where(head_of_lane == h, q, 0.0).astype(BF16)
        carry, acc = tile(u, pl.multiple_of(qi * tq, tq), None, jnp.zeros((tq, 256), F32), None)
        for d in range(1, STICK_EAGER_TILES + 1):
            j = qi - d
            carry, acc = tile(u, pl.multiple_of(jnp.maximum(j, 0) * tq, tq), carry, acc, j >= 0)
        carry_ref[u] = carry
        o_ref[u] = acc
        worst.append(jnp.max(carry, axis=0, keepdims=True)[0, 0])

    for u in range(nsub):
        qi = pl.program_id(1) * nsub + u

        def cond(state):
            j, worst_carry = state
            return (j >= 0) & (worst_carry > STICK_EXIT)

        def body(state, u=u):
            j, _ = state
            new_carry, new_acc = tile(u, pl.multiple_of(j * tq, tq), carry_ref[u], o_ref[u], None)
            carry_ref[u] = new_carry
            o_ref[u] = new_acc
            return j - 1, jnp.max(new_carry, axis=0, keepdims=True)[0, 0]

        lax.while_loop(cond, body, (qi - 1 - STICK_EAGER_TILES, worst[u]))
        yc_ref[0, u * tq:(u + 1) * tq, :] = o_ref[u]


def _stick(cq, ck, cv, tq, nsub):
    b, s, _ = cq.shape
    step = tq * nsub
    return pl.pallas_call(
        functools.partial(_stick_kernel, tq=tq, nsub=nsub, seq=s),
        grid=(b, s // step),
        in_specs=[
            pl.BlockSpec((1, step, 256), lambda i, j: (i, j, 0)),
            pl.BlockSpec((1, s, 256), lambda i, j: (i, 0, 0)),
            pl.BlockSpec((1, s, 256), lambda i, j: (i, 0, 0)),
        ],
        out_specs=pl.BlockSpec((1, step, 256), lambda i, j: (i, j, 0)),
        out_shape=jax.ShapeDtypeStruct((b, s, 256), F32),
        scratch_shapes=[
            pltpu.VMEM((nsub, N_HEADS * tq, 256), BF16),
            pltpu.VMEM((nsub, N_HEADS * tq, tq), F32),
            pltpu.VMEM((nsub, tq, 256), F32),
        ],
        compiler_params=_params("arbitrary", "arbitrary"),
        name="stick_breaking",
    )(cq, ck, cv)


def _ret_kernel(dq_ref, dk_ref, dv_ref, dmat_ref, xi_ref, zeta_ref, decay_ref, bd_ref, yd_ref, r_ref):
    n = pl.program_id(1)

    @pl.when(n == 0)
    def _():
        r_ref[...] = jnp.zeros(r_ref.shape, F32)

    c = RET_CHUNK
    lane256 = lax.broadcasted_iota(jnp.int32, (1, 256), 1)
    head_of_lane = lane256 >> 6
    r = r_ref[...]
    for u in range(dq_ref.shape[1] // c):
        rows = slice(u * c, (u + 1) * c)
        q = dq_ref[0, rows, :].astype(F32) * QK_SCALE
        k = dk_ref[0, rows, :]
        v = dv_ref[0, rows, :]
        o = _dot(q.astype(BF16), r.astype(BF16)) * xi_ref[...]
        for h in range(N_HEADS):
            qh = jnp.where(head_of_lane == h, q, 0.0).astype(BF16)
            s = _dot_t(qh, k) * dmat_ref[h]
            o = o + jnp.where(head_of_lane == h, _dot(s.astype(BF16), v), 0.0)
        mu = jnp.zeros((c, 256), F32)
        for h in range(N_HEADS):
            in_h = head_of_lane == h
            mu = mu + jnp.where(in_h, jnp.sum(jnp.where(in_h, o, 0.0), axis=-1, keepdims=True), 0.0)
        cen = o - mu * (1.0 / HEAD_DIM)
        var = jnp.zeros((c, 256), F32)
        for h in range(N_HEADS):
            in_h = head_of_lane == h
            var = var + jnp.where(in_h, jnp.sum(jnp.where(in_h, cen * cen, 0.0), axis=-1, keepdims=True), 0.0)
        yd_ref[0, rows, :] = cen * lax.rsqrt(var * (1.0 / HEAD_DIM) + LN_EPS)
        kz = (k.astype(F32) * zeta_ref[...]).astype(BF16)
        upd = lax.dot_general(kz, v, (((0,), (0,)), ((), ())), preferred_element_type=F32)
        r = r * decay_ref[...] + upd * bd_ref[...]
    r_ref[...] = r


def _retention(dq, dk, dv, consts):
    b, s, _ = dq.shape
    c = RET_CHUNK
    dmat, xi, zeta, decay, bd = consts
    step = RET_CHUNKS_PER_STEP * c
    tile = pl.BlockSpec((1, step, 256), lambda i, j: (i, j, 0))
    return pl.pallas_call(
        _ret_kernel,
        grid=(b, s // step),
        in_specs=[
            tile, tile, tile,
            pl.BlockSpec((N_HEADS, c, c), lambda i, j: (0, 0, 0)),
            pl.BlockSpec((c, 256), lambda i, j: (0, 0)),
            pl.BlockSpec((c, 256), lambda i, j: (0, 0)),
            pl.BlockSpec((256, 256), lambda i, j: (0, 0)),
            pl.BlockSpec((256, 256), lambda i, j: (0, 0)),
        ],
        out_specs=tile,
        out_shape=jax.ShapeDtypeStruct((b, s, 256), F32),
        scratch_shapes=[pltpu.VMEM((256, 256), F32)],
        compiler_params=_params("arbitrary", "arbitrary"),
        name="retention",
    )(dq, dk, dv, dmat, xi, zeta, decay, bd)


def _retention_consts():
    c = RET_CHUNK
    log_g = jnp.log(1.0 - jnp.asarray(2.0 ** (-5.0 - np.arange(N_HEADS)), dtype=F32))
    i = jnp.arange(c)
    diff = (i[:, None] - i[None, :]).astype(F32)
    dmat = jnp.where(diff >= 0, jnp.exp(log_g[:, None, None] * jnp.maximum(diff, 0.0)), 0.0)
    zeta = jnp.exp(log_g[:, None] * (c - 1 - i)[None, :].astype(F32))
    xi = jnp.exp(log_g[:, None] * (i + 1)[None, :].astype(F32))
    g_chunk = jnp.exp(log_g * c)
    lanes = lambda hc: jnp.repeat(hc.T, HEAD_DIM, axis=1)
    head = np.arange(256) // HEAD_DIM
    bd = jnp.asarray(head[:, None] == head[None, :], F32)
    decay = jnp.repeat(g_chunk, HEAD_DIM)[:, None] * jnp.ones((1, 256), F32)
    return dmat, lanes(xi), lanes(zeta), decay, bd


def _merge_kernel(x_ref, mod_ref, g_ref, ya_ref, yb_ref, yc_ref, yd_ref, z_ref,
                  wm_ref, wbr_ref, wout_ref, fg_ref, o_ref, *, d, final):
    mod = mod_ref[0]
    tm = x_ref.shape[1]
    part = tm // MERGE_ROW_PARTS
    for r in range(MERGE_ROW_PARTS):
        rows = slice(r * part, (r + 1) * part)
        xf = x_ref[0, rows, :]
        h = _modulated_norm(xf, g_ref[...], mod, d).astype(BF16)
        merged = None
        for i, y_ref in enumerate((ya_ref, yb_ref, yc_ref, yd_ref)):
            yi = (y_ref[0, rows, :] * _silu(z_ref[0, rows, 256 * i:256 * (i + 1)])).astype(BF16)
            term = _sigmoid(_dot(h, wm_ref[i])) * _dot(yi, wbr_ref[i])
            merged = term if merged is None else merged + term
        out = xf + mod[:, 2 * d:3 * d] * _dot(merged.astype(BF16), wout_ref[...])
        if final:
            out = out * lax.rsqrt(jnp.mean(out * out, axis=-1, keepdims=True) + RMS_EPS) * fg_ref[...]
        o_ref[0, rows, :] = out


def _merge(x, mod, g, ys, z, wm, wbr, wout, final_g, final, tm):
    b, s, d = x.shape
    ytile = pl.BlockSpec((1, tm, 256), lambda i, j: (i, j, 0))
    return pl.pallas_call(
        functools.partial(_merge_kernel, d=d, final=final),
        grid=(b, s // tm),
        in_specs=[
            pl.BlockSpec((1, tm, d), lambda i, j: (i, j, 0)),
            pl.BlockSpec((1, 1, 3 * d), lambda i, j: (i, 0, 0)),
            pl.BlockSpec((1, d), lambda i, j: (0, 0)),
            ytile, ytile, ytile, ytile,
            pl.BlockSpec((1, tm, 4 * 256), lambda i, j: (i, j, 0)),
            pl.BlockSpec((4, d, d), lambda i, j: (0, 0, 0)),
            pl.BlockSpec((4, 256, d), lambda i, j: (0, 0, 0)),
            pl.BlockSpec((d, d), lambda i, j: (0, 0)),
            pl.BlockSpec((1, d), lambda i, j: (0, 0)),
        ],
        out_specs=pl.BlockSpec((1, tm, d), lambda i, j: (i, j, 0)),
        out_shape=jax.ShapeDtypeStruct((b, s, d), F32),
        compiler_params=_params("arbitrary", "arbitrary"),
        name="merge_out",
    )(x, mod, g.reshape(1, d), *ys, z, wm, wbr, wout, final_g.reshape(1, d))


def _proj_columns():
    a, bb, cc, dd = 0, 908, 1676, 2700
    perm = np.concatenate([np.arange(64) + 64 * h for h in SWA_HEAD_ORDER])
    pad = lambda n: np.full((n,), -1)
    cols = np.concatenate([
        np.arange(a, a + 256),
        np.arange(a + 256, a + 384),
        np.arange(a + 384, a + 640),
        np.arange(a + 640, a + 652), pad(116),
        np.arange(a + 652, a + 908),
        bb + 512 + perm,
        np.arange(cc + 768, cc + 1024),
        np.arange(dd + 768, dd + 1024),
        bb + perm,
        np.arange(bb + 256, bb + 512),
        np.arange(cc, cc + 768),
        np.arange(dd, dd + 768),
    ])
    assert cols.shape[0] == PROJ_W
    return cols, perm


def _take_columns(w, cols):
    pieces, i = [], 0
    while i < len(cols):
        j = i + 1
        if cols[i] < 0:
            while j < len(cols) and cols[j] < 0:
                j += 1
            pieces.append(jnp.zeros(w.shape[:-1] + (j - i,), w.dtype))
        else:
            while j < len(cols) and cols[j] == cols[j - 1] + 1:
                j += 1
            pieces.append(w[..., int(cols[i]):int(cols[j - 1]) + 1])
        i = j
    return jnp.concatenate(pieces, axis=-1)


def _position_features(s):
    pos = np.arange(s)
    feat = np.zeros((s, 64), np.float32)
    feat[:, 0] = SEL_BLOCK * (pos // SEL_BLOCK)
    feat[:, 1] = pos % SEL_BLOCK
    onehot = (pos[:, None] // SEL_BLOCK == np.arange(128)[None, :]).astype(np.float32)
    return jnp.asarray(feat, BF16), jnp.asarray(onehot, BF16)


def kernel(x, c, w_ada, b_ada, norm_g, w_in, cmp_pos, cmp_w1, cmp_w2, sink, w_merge, w_br, w_out, final_g):
    b, s, d = x.shape
    depth = w_ada.shape[0]
    tm = min(ROW_TILE, s)

    cols, perm = _proj_columns()
    w_in_p = _take_columns(w_in, cols).astype(BF16)
    wm_b = w_merge.astype(BF16)
    wbr_b = w_br.at[:, 1].set(w_br[:, 1][:, perm, :]).astype(BF16)
    wout_b = w_out.astype(BF16)
    pos_l, w1bd, w2bd = _compress_weights(cmp_pos, cmp_w1, cmp_w2)
    feat, onehot = _position_features(s)
    kconst = jnp.concatenate([jnp.zeros((s, HEAD_DIM), BF16), feat, onehot], axis=-1)
    ret_consts = _retention_consts()
    mods = _modulation(c, w_ada, b_ada)

    for l in range(depth):
        p = _project(x, mods[l], norm_g[l], w_in_p[l], kconst, tm)
        kc, vc = _compress(p["acmp"], pos_l[l], w1bd[l], w2bd[l])
        y_a = _nsa(p["aq"], p["ag"], kc, vc, p["ksel"], p["akv"], p["kwin"], NSA_TQ)
        y_b = _swa(sink[l], p["bq"], p["bkv"], SWA_TQ)
        y_c = _stick(p["cq"], p["ck"], p["cv"], STICK_TQ, STICK_SUBTILES)
        y_d = _retention(p["dq"], p["dk"], p["dv"], ret_consts)
        x = _merge(x, mods[l], norm_g[l], (y_a, y_b, y_c, y_d), p["z"], wm_b[l], wbr_b[l], wout_b[l],
                   final_g, l == depth - 1, tm)
    return x
```

```python
import functools

import numpy as np
import jax
import jax.numpy as jnp
from jax import lax
from jax.experimental import pallas as pl
from jax.experimental.pallas import tpu as pltpu

F32 = jnp.float32
BF16 = jnp.bfloat16

HEAD_DIM = 64
N_HEADS = 4
BRANCH_W = N_HEADS * HEAD_DIM
CMP_BLOCK = 32
SEL_BLOCK = 64
SEL_TOPK = 16
NSA_WINDOW = 512
NSA_TK = 512
SWA_WINDOW = 128
RET_CHUNK = 128
RMS_EPS = 1e-6
LN_EPS = 1e-5
NEG_INF = -1e30
TINY = 1e-30
FORCED_SCORE = 1e4
QK_SCALE = HEAD_DIM ** -0.5
STICK_EXIT = -110.0
STICK_EAGER_TILES = 2
MAX_BOUND_GAP = 60.0
BOUND_SLACK = 1.001

ROW_TILE = 512
MERGE_ROW_PARTS = 2
NSA_TQ = 256
NSA_SUB = 128
SWA_TQ = 1024
RET_CHUNKS_PER_STEP = 8
STICK_TQ = 128
STICK_SUBTILES = 4

VMEM_LIMIT = 56 * 1024 * 1024
SLOPES = tuple(float(2.0 ** (-8.0 * (h + 1) / N_HEADS)) for h in range(N_HEADS))

PROJ_OUTS = (
    ("aq", 0, 256, BF16),
    ("acmp", 256, 128, F32),
    ("akv", 384, 256, BF16),
    ("ag", 640, 128, F32),
    ("z", 768, 1024, F32),
    ("bq", 1792, 256, BF16),
    ("bkv", 2048, 256, BF16),
    ("cq", 2304, 256, BF16),
    ("ck", 2560, 256, BF16),
    ("cv", 2816, 256, BF16),
    ("dq", 3072, 256, BF16),
    ("dk", 3328, 256, BF16),
    ("dv", 3584, 256, BF16),
)
PROJ_W = 3840
SWA_HEAD_ORDER = (0, 2, 1, 3)


def _dot(a, b):
    return jnp.dot(a, b, preferred_element_type=F32)


def _dot_t(a, b):
    return lax.dot_general(a, b, (((1,), (1,)), ((), ())), preferred_element_type=F32)


def _dot_split(a, b):
    hi = a.astype(BF16)
    lo = (a - hi.astype(F32)).astype(BF16)
    return _dot(hi, b) + _dot(lo, b)


def _sigmoid(v):
    return 1.0 / (1.0 + jnp.exp(-v))


def _silu(v):
    return v * _sigmoid(v)


def _params(*sem):
    return pltpu.CompilerParams(dimension_semantics=sem, vmem_limit_bytes=VMEM_LIMIT)


def _mod_kernel(c_ref, w_ref, b_ref, o_ref):
    cc = c_ref[...]
    o_ref[0] = _dot(_silu(cc).astype(BF16), w_ref[0].astype(BF16)) + b_ref[0]


def _modulation(c, w_ada, b_ada):
    depth, d, n = w_ada.shape
    b = c.shape[0]
    rows = 8
    cp = jnp.zeros((rows, d), F32).at[:b].set(c)
    tn = 512
    out = pl.pallas_call(
        _mod_kernel,
        grid=(depth, n // tn),
        in_specs=[
            pl.BlockSpec((rows, d), lambda l, j: (0, 0)),
            pl.BlockSpec((1, d, tn), lambda l, j: (l, 0, j)),
            pl.BlockSpec((1, 1, tn), lambda l, j: (l, 0, j)),
        ],
        out_specs=pl.BlockSpec((1, rows, tn), lambda l, j: (l, 0, j)),
        out_shape=jax.ShapeDtypeStruct((depth, rows, n), F32),
        compiler_params=_params("arbitrary", "arbitrary"),
        name="adaln_mod",
    )(cp, w_ada, b_ada.reshape(depth, 1, n))
    return out[:, :b].reshape(depth, b, 1, n)


def _modulated_norm(xf, g, mod, d):
    ms = jnp.mean(xf * xf, axis=-1, keepdims=True)
    y = xf * lax.rsqrt(ms + RMS_EPS) * g
    return y * (1.0 + mod[:, d:2 * d]) + mod[:, 0:d]


def _proj_kernel(x_ref, mod_ref, g_ref, w_ref, kconst_ref, *out_refs, d):
    h = _modulated_norm(x_ref[0], g_ref[...], mod_ref[0], d).astype(BF16)
    ksel_ref, kwin_ref = out_refs[len(PROJ_OUTS):]
    for (name, start, width, dtype), o_ref in zip(PROJ_OUTS, out_refs):
        for c0 in range(0, width, 256):
            cw = min(256, width - c0)
            acc = _dot(h, w_ref[:, start + c0:start + c0 + cw])
            o_ref[0, :, c0:c0 + cw] = acc.astype(dtype)
            if name == "akv":
                ksel_ref[0] = kconst_ref[...]
                ksel_ref[0, :, 0:HEAD_DIM] = acc[:, 0:HEAD_DIM].astype(BF16)
                kwin_ref[0] = kconst_ref[:, 0:128]
                kwin_ref[0, :, 0:HEAD_DIM] = acc[:, 128:128 + HEAD_DIM].astype(BF16)


def _project(x, mod, g, w_p, kconst, tm):
    b, s, d = x.shape
    widths = [(wd, dt) for (_, _, wd, dt) in PROJ_OUTS] + [(256, BF16), (128, BF16)]
    out_shapes = [jax.ShapeDtypeStruct((b, s, wd), dt) for wd, dt in widths]
    out_specs = [pl.BlockSpec((1, tm, wd), lambda i, j: (i, j, 0)) for wd, _ in widths]
    outs = pl.pallas_call(
        functools.partial(_proj_kernel, d=d),
        grid=(b, s // tm),
        in_specs=[
            pl.BlockSpec((1, tm, d), lambda i, j: (i, j, 0)),
            pl.BlockSpec((1, 1, 3 * d), lambda i, j: (i, 0, 0)),
            pl.BlockSpec((1, d), lambda i, j: (0, 0)),
            pl.BlockSpec((d, PROJ_W), lambda i, j: (0, 0)),
            pl.BlockSpec((tm, 256), lambda i, j: (j, 0)),
        ],
        out_specs=out_specs,
        out_shape=out_shapes,
        compiler_params=_params("arbitrary", "arbitrary"),
        name="in_proj",
    )(x, mod, g.reshape(1, d), w_p, kconst)
    names = [name for (name, _, _, _) in PROJ_OUTS] + ["ksel", "kwin"]
    return dict(zip(names, outs))


def _compress_kernel(acmp_ref, pos_ref, w1_ref, w2_ref, kc_ref, vc_ref, *, nc):
    half = nc // 2
    for parity in range(2):
        hid = jnp.zeros((half, 128), F32)
        for i in range(CMP_BLOCK):
            z = acmp_ref[0, pl.ds(parity * CMP_BLOCK + i, half, stride=2 * CMP_BLOCK), :]
            hid = hid + _dot((z + pos_ref[i:i + 1, :]).astype(BF16), w1_ref[i])
        act = _silu(hid).astype(BF16)
        out_rows = slice(parity * half, (parity + 1) * half)
        kc_ref[0, out_rows, :] = _dot(act, w2_ref[0]).astype(BF16)
        vc_ref[0, out_rows, :] = _dot(act, w2_ref[1]).astype(BF16)


def _compress(acmp, pos, w1bd, w2bd):
    b, s, _ = acmp.shape
    nc = s // CMP_BLOCK
    spec_o = pl.BlockSpec((1, nc, 128), lambda i: (i, 0, 0))
    return pl.pallas_call(
        functools.partial(_compress_kernel, nc=nc),
        grid=(b,),
        in_specs=[
            pl.BlockSpec((1, s, 128), lambda i: (i, 0, 0)),
            pl.BlockSpec((CMP_BLOCK, 128), lambda i: (0, 0)),
            pl.BlockSpec((CMP_BLOCK, 128, 128), lambda i: (0, 0, 0)),
            pl.BlockSpec((2, 128, 128), lambda i: (0, 0, 0)),
        ],
        out_specs=[spec_o, spec_o],
        out_shape=[jax.ShapeDtypeStruct((b, nc, 128), BF16)] * 2,
        compiler_params=_params("arbitrary"),
        name="nsa_compress",
    )(acmp, pos, w1bd, w2bd)


def _compress_weights(cmp_pos, cmp_w1, cmp_w2):
    depth = cmp_pos.shape[0]
    hd = HEAD_DIM
    pos = jnp.concatenate([cmp_pos[:, 0], cmp_pos[:, 1]], axis=-1)
    w1 = cmp_w1.reshape(depth, 2, CMP_BLOCK, hd, hd)
    z = jnp.zeros((depth, CMP_BLOCK, hd, hd), F32)
    w1bd = jnp.concatenate([jnp.concatenate([w1[:, 0], z], axis=-1),
                            jnp.concatenate([z, w1[:, 1]], axis=-1)], axis=-2).astype(BF16)
    z2 = jnp.zeros((depth, hd, hd), F32)
    w2k = jnp.concatenate([jnp.concatenate([cmp_w2[:, 0], z2], axis=-1),
                           jnp.concatenate([z2, z2], axis=-1)], axis=-2)
    w2v = jnp.concatenate([jnp.concatenate([z2, z2], axis=-1),
                           jnp.concatenate([z2, cmp_w2[:, 1]], axis=-1)], axis=-2)
    return pos, w1bd, jnp.stack([w2k, w2v], axis=1).astype(BF16)


def _nsa_kernel(aq_ref, ag_ref, kc_ref, vc_ref, ksel_ref, vsel_ref, kwin_ref, vwin_ref, wmask_ref, ya_ref,
                qaug_ref, mx_ref, sm_ref, acc_ref, owin_ref, kmax_ref, *, tq, seq):
    qi = pl.program_id(1)
    qstart = qi * tq
    nc = seq // CMP_BLOCK
    half = nc // 2
    ns = seq // SEL_BLOCK
    t_lane = lax.broadcasted_iota(jnp.int32, (1, tq), 1) + qstart
    lane128 = lax.broadcasted_iota(jnp.int32, (1, 128), 1)
    nsub = tq // NSA_SUB
    sub_rows_log2 = (N_HEADS * NSA_SUB).bit_length() - 1
    row0 = lambda u, h: (u * N_HEADS + h) * NSA_SUB
    rows = N_HEADS * tq
    row_id = lax.broadcasted_iota(jnp.int32, (rows, 1), 0)
    t_row = qstart + (row_id >> sub_rows_log2) * NSA_SUB + (row_id & (NSA_SUB - 1))

    @pl.when(qi == 0)
    def _():
        def chunk(c, best):
            kk = ksel_ref[0, pl.ds(pl.multiple_of(c * NSA_TK, NSA_TK), NSA_TK), 0:128].astype(F32)
            sq = jnp.sum(jnp.where(lane128 < HEAD_DIM, kk * kk, 0.0), axis=-1, keepdims=True)
            return jnp.maximum(best, jnp.max(sq, axis=0, keepdims=True))
        best = lax.fori_loop(0, seq // NSA_TK, chunk, jnp.zeros((1, 1), F32))
        kmax_ref[0] = jnp.sqrt(best)[0, 0]

    row_c = lax.broadcasted_iota(jnp.int32, (nc, 1), 0)
    blk_c = jnp.where(row_c < half, 2 * row_c, 2 * (row_c - half) + 1)
    dist = t_lane - (blk_c * CMP_BLOCK + (CMP_BLOCK - 1))
    dist_f = dist.astype(F32)
    mask_c = dist >= 0
    kc = kc_ref[0]
    vc = vc_ref[0]
    psum = jnp.zeros((nc, tq), F32)
    ocmp = []
    for h in range(N_HEADS):
        grp = aq_ref[0, :, 128 * (h // 2):128 * (h // 2) + 128].astype(F32)
        if h % 2 == 1:
            grp = pltpu.roll(grp, 64, 1)
        left = jnp.where(lane128 < HEAD_DIM, grp * QK_SCALE,
                         jnp.where(lane128 < HEAD_DIM + 2, SLOPES[h], 0.0)).astype(BF16)
        for u in range(nsub):
            qaug_ref[row0(u, h):row0(u, h) + NSA_SUB, 0:128] = left[u * NSA_SUB:(u + 1) * NSA_SUB]
        s = _dot_t(kc, left) - SLOPES[h] * dist_f
        s = jnp.where(mask_c, s, NEG_INF)
        m = jnp.max(s, axis=0, keepdims=True)
        e = jnp.where(mask_c, jnp.exp(s - m), 0.0)
        den = jnp.sum(e, axis=0, keepdims=True)
        p = e * (1.0 / jnp.maximum(den, TINY))
        psum = psum + p
        ocmp.append(lax.dot_general(p.astype(BF16), vc, (((0,), (0,)), ((), ())),
                                    preferred_element_type=F32))

    def fold(sc):
        m = mx_ref[...]
        for g in range(sc.shape[1] // 128):
            m = jnp.maximum(m, sc[:, 128 * g:128 * (g + 1)])
        mx_ref[...] = m

    def accumulate(sc, v):
        m = mx_ref[...]
        e = jnp.exp(sc - jnp.concatenate([m] * (sc.shape[1] // 128), axis=1))
        part = sm_ref[...]
        for g in range(sc.shape[1] // 128):
            part = part + e[:, 128 * g:128 * (g + 1)]
        sm_ref[...] = part
        acc_ref[...] += _dot(e.astype(BF16), v)

    own0 = pl.multiple_of(qstart, tq)

    def own_scores():
        sc = _dot_t(qaug_ref[:, 0:128], ksel_ref[0, pl.ds(own0, tq), 0:128])
        return jnp.where(lax.broadcasted_iota(jnp.int32, (1, tq), 1) + qstart <= t_row, sc, NEG_INF)

    def reset_sums():
        sm_ref[...] = jnp.zeros(sm_ref.shape, F32)
        acc_ref[...] = jnp.zeros(acc_ref.shape, F32)

    qpart = qaug_ref[:, 0:128].astype(F32)
    qnorm = jnp.sqrt(jnp.sum(jnp.where(lane128 < HEAD_DIM, qpart * qpart, 0.0), axis=-1, keepdims=True))
    reach = qnorm * (kmax_ref[0] * BOUND_SLACK)
    bound_ok = jnp.max(reach, axis=0, keepdims=True)[0, 0] * 2.0 <= MAX_BOUND_GAP
    head_of_row = (row_id >> (NSA_SUB.bit_length() - 1)) & (N_HEADS - 1)
    slope = jnp.full((rows, 1), SLOPES[0], F32)
    for h in range(1, N_HEADS):
        slope = jnp.where(head_of_row == h, SLOPES[h], slope)
    mx_ref[...] = jnp.broadcast_to(reach + slope * t_row.astype(F32), mx_ref.shape)
    reset_sums()
    accumulate(own_scores(), vsel_ref[0, pl.ds(own0, tq), :])

    n_win = NSA_WINDOW + NSA_SUB
    full_chunks = NSA_WINDOW // NSA_SUB
    for u in range(nsub):
        q0 = qstart + u * NSA_SUB
        w0 = pl.multiple_of(jnp.clip(q0 - NSA_WINDOW, 0, seq - n_win), NSA_SUB)
        sub_rows = slice(row0(u, 0), row0(u, 0) + N_HEADS * NSA_SUB)
        sw = _dot_t(qaug_ref[sub_rows, 0:128], kwin_ref[0, pl.ds(w0, n_win), :])
        diag_chunk = (q0 - w0) >> (NSA_SUB.bit_length() - 1)
        pieces = []
        for c in range(n_win // NSA_SUB):
            d = diag_chunk - c
            pattern = jnp.where(d == 0, 1, jnp.where((d > 0) & (d < full_chunks), 2,
                                                     jnp.where(d == full_chunks, 3, 0)))
            pieces.append(sw[:, c * NSA_SUB:(c + 1) * NSA_SUB] + wmask_ref[pattern])
        sw = jnp.concatenate(pieces, axis=1)
        ew = jnp.exp(sw - jnp.max(sw, axis=-1, keepdims=True))
        owin_ref[sub_rows, :] = (_dot(ew.astype(BF16), vwin_ref[0, pl.ds(w0, n_win), :])
                                 * (1.0 / jnp.sum(ew, axis=-1, keepdims=True)))

    imp = psum[:half] + psum[half:]
    blk = lax.broadcasted_iota(jnp.int32, (ns, 1), 0)
    cur = t_lane >> 6
    future = blk * SEL_BLOCK > t_lane
    forced = (blk == 0) | (blk == cur) | (blk == cur - 1)
    score = jnp.where(forced, FORCED_SCORE, jnp.where(future, -1.0, imp))
    blk_f = blk.astype(F32)
    sel_t = jnp.zeros((ns, tq), F32)
    for _ in range(min(SEL_TOPK, ns)):
        mx = jnp.max(score, axis=0, keepdims=True)
        first = jnp.min(jnp.where(score == mx, blk_f, float(ns)), axis=0, keepdims=True)
        pick = blk_f == first
        sel_t = jnp.where(pick, 1.0, sel_t)
        score = jnp.where(pick, -jnp.inf, score)
    sel_f = jnp.transpose(sel_t)
    negmask = jnp.where(sel_f > 0.5, 0.0, NEG_INF).astype(BF16)
    if ns < 128:
        negmask = jnp.concatenate([negmask, jnp.zeros((tq, 128 - ns), BF16)], axis=1)
    for u in range(nsub):
        for h in range(N_HEADS):
            qaug_ref[row0(u, h):row0(u, h) + NSA_SUB, 128:256] = negmask[u * NSA_SUB:(u + 1) * NSA_SUB]

    if ns < 128:
        sel_f = jnp.concatenate([sel_f, jnp.zeros((tq, 128 - ns), F32)], axis=1)
    any_blk = jnp.max(sel_f, axis=0, keepdims=True)
    blocks_per_tile = NSA_TK // SEL_BLOCK
    shift = 1
    while shift < blocks_per_tile:
        any_blk = jnp.maximum(any_blk, pltpu.roll(any_blk, 128 - shift, 1))
        shift *= 2
    tile_of_lane = lane128 >> (blocks_per_tile.bit_length() - 1)
    first_of_tile = (lane128 & (blocks_per_tile - 1)) == 0
    pow2 = lax.bitcast_convert_type(((tile_of_lane & 15) + 127) << 23, F32)
    weighted = jnp.where(first_of_tile, any_blk * pow2, 0.0)
    bits_lo = jnp.sum(jnp.where(tile_of_lane < 16, weighted, 0.0), axis=1, keepdims=True).astype(jnp.int32)[0, 0]
    bits_hi = jnp.sum(jnp.where(tile_of_lane >= 16, weighted, 0.0), axis=1, keepdims=True).astype(jnp.int32)[0, 0]

    def tile_selected(t):
        return ((jnp.where(t < 16, bits_lo, bits_hi) >> (t & 15)) & 1) == 1

    pieces_per_tile = NSA_TK // tq
    t_own = qi // pieces_per_tile
    n_pieces = qi - t_own * pieces_per_tile

    def sweep(use):
        def piece(p, carry):
            k0 = pl.multiple_of(t_own * NSA_TK + p * tq, tq)
            use(_dot_t(qaug_ref[...], ksel_ref[0, pl.ds(k0, tq), :]), k0, tq)
            return carry

        def tile(t, carry):
            def visit():
                k0 = pl.multiple_of(t * NSA_TK, NSA_TK)
                use(_dot_t(qaug_ref[...], ksel_ref[0, pl.ds(k0, NSA_TK), :]), k0, NSA_TK)
            pl.when(tile_selected(t))(visit)
            return carry

        lax.fori_loop(0, n_pieces, piece, 0)
        lax.fori_loop(0, t_own, tile, 0)

    @pl.when(jnp.logical_not(bound_ok))
    def _():
        mx_ref[...] = jnp.full(mx_ref.shape, NEG_INF, F32)
        sweep(lambda sc, k0, n: fold(sc))
        fold(own_scores())
        mx_ref[...] = jnp.broadcast_to(jnp.max(mx_ref[...], axis=-1, keepdims=True), mx_ref.shape)
        reset_sums()
        accumulate(own_scores(), vsel_ref[0, pl.ds(own0, tq), :])

    sweep(lambda sc, k0, n: accumulate(sc, vsel_ref[0, pl.ds(k0, n), :]))
    osel = acc_ref[...] * (1.0 / jnp.sum(sm_ref[...], axis=-1, keepdims=True))

    gate = _sigmoid(ag_ref[0])
    for u in range(nsub):
        q_rows = slice(u * NSA_SUB, (u + 1) * NSA_SUB)
        comb = []
        for h in range(N_HEADS):
            head_rows = slice(row0(u, h), row0(u, h) + NSA_SUB)
            comb.append(gate[q_rows, 3 * h:3 * h + 1] * ocmp[h][q_rows, :]
                        + gate[q_rows, 3 * h + 1:3 * h + 2] * osel[head_rows, :]
                        + gate[q_rows, 3 * h + 2:3 * h + 3] * owin_ref[head_rows, :])
        for g in range(2):
            ya_ref[0, q_rows, 128 * g:128 * (g + 1)] = jnp.where(
                lane128 < HEAD_DIM, pltpu.roll(comb[2 * g], 64, 1), comb[2 * g + 1])


def _nsa(aq, ag, kc, vc, ksel, akv, kwin, tq):
    b, s, _ = aq.shape
    nc = kc.shape[1]
    rows = N_HEADS * tq
    whole = lambda w, blk: pl.BlockSpec((1, s, w), lambda i, j, blk=blk: (i, 0, blk))
    return pl.pallas_call(
        functools.partial(_nsa_kernel, tq=tq, seq=s),
        grid=(b, s // tq),
        in_specs=[
            pl.BlockSpec((1, tq, 256), lambda i, j: (i, j, 0)),
            pl.BlockSpec((1, tq, 128), lambda i, j: (i, j, 0)),
            pl.BlockSpec((1, nc, 128), lambda i, j: (i, 0, 0)),
            pl.BlockSpec((1, nc, 128), lambda i, j: (i, 0, 0)),
            whole(256, 0),
            whole(128, 0),
            whole(128, 0),
            whole(128, 1),
            pl.BlockSpec((4, N_HEADS * NSA_SUB, NSA_SUB), lambda i, j: (0, 0, 0)),
        ],
        out_specs=pl.BlockSpec((1, tq, 256), lambda i, j: (i, j, 0)),
        out_shape=jax.ShapeDtypeStruct((b, s, 256), F32),
        scratch_shapes=[
            pltpu.VMEM((rows, 256), BF16),
            pltpu.VMEM((rows, 128), F32),
            pltpu.VMEM((rows, 128), F32),
            pltpu.VMEM((rows, 128), F32),
            pltpu.VMEM((rows, 128), F32),
            pltpu.SMEM((1,), F32),
        ],
        compiler_params=_params("arbitrary", "arbitrary"),
        name="nsa_attention",
    )(aq, ag, kc, vc, ksel, akv, kwin, akv, _window_mask_table())


def _window_mask_table():
    r = np.arange(N_HEADS * NSA_SUB)[:, None] % NSA_SUB
    j = np.arange(NSA_SUB)[None, :]
    keep = np.stack([np.zeros_like(j <= r), j <= r, np.ones_like(j <= r), j > r])
    return jnp.asarray(np.where(keep, 0.0, NEG_INF), F32)


def _swa_kernel(sink_ref, bq_ref, bkv_ref, bias_ref, yb_ref, *, tq, seq):
    qi = pl.program_id(1)
    sub = SWA_WINDOW
    nk = 2 * sub
    lane128 = lax.broadcasted_iota(jnp.int32, (1, 128), 1)
    for u in range(tq // sub):
        qstart = qi * tq + u * sub
        k0 = pl.multiple_of(jnp.maximum(qstart - sub, 0), sub)
        at_start = jnp.where(qstart == 0, 1, 0)
        kk = bkv_ref[0, pl.ds(k0, nk), 0:128]
        vv = bkv_ref[0, pl.ds(k0, nk), 128:256]
        for g in range(2):
            qg = bq_ref[0, u * sub:(u + 1) * sub, 128 * g:128 * (g + 1)].astype(F32) * QK_SCALE
            outs = []
            for p in range(2):
                h = SWA_HEAD_ORDER[2 * g + p]
                in_half = (lane128 >= HEAD_DIM * p) & (lane128 < HEAD_DIM * (p + 1))
                qh = jnp.where(in_half, qg, 0.0).astype(BF16)
                s = _dot_t(kk, qh) + bias_ref[h, at_start]
                sink = sink_ref[h]
                m = jnp.maximum(jnp.max(s, axis=0, keepdims=True), sink)
                e = jnp.exp(s - m)
                den = jnp.sum(e, axis=0, keepdims=True) + jnp.exp(sink - m)
                pr = e * (1.0 / jnp.maximum(den, TINY))
                outs.append(lax.dot_general(pr.astype(BF16), vv, (((0,), (0,)), ((), ())),
                                            preferred_element_type=F32))
            yb_ref[0, u * sub:(u + 1) * sub, 128 * g:128 * (g + 1)] = jnp.where(
                lane128 < HEAD_DIM, outs[0], outs[1])


def _swa_bias_table():
    sub = SWA_WINDOW
    j = np.arange(2 * sub)[:, None]
    r = np.arange(sub)[None, :]
    rel = np.stack([r + sub - j, r - j])
    ok = (rel >= 0) & (rel < sub)
    slopes = np.asarray(SLOPES, np.float32)[:, None, None, None]
    return jnp.asarray(np.where(ok[None], -slopes * rel[None].astype(np.float32), NEG_INF), F32)


def _swa(sink, bq, bkv, tq):
    b, s, _ = bq.shape
    return pl.pallas_call(
        functools.partial(_swa_kernel, tq=tq, seq=s),
        grid=(b, s // tq),
        in_specs=[
            pl.BlockSpec(memory_space=pltpu.SMEM),
            pl.BlockSpec((1, tq, 256), lambda i, j: (i, j, 0)),
            pl.BlockSpec((1, s, 256), lambda i, j: (i, 0, 0)),
            pl.BlockSpec((N_HEADS, 2, 2 * SWA_WINDOW, SWA_WINDOW), lambda i, j: (0, 0, 0, 0)),
        ],
        out_specs=pl.BlockSpec((1, tq, 256), lambda i, j: (i, j, 0)),
        out_shape=jax.ShapeDtypeStruct((b, s, 256), F32),
        compiler_params=_params("arbitrary", "arbitrary"),
        name="swa_attention",
    )(sink, bq, bkv, _swa_bias_table())


def _stick_kernel(cq_ref, ck_ref, cv_ref, yc_ref, qm_ref, carry_ref, o_ref, *, tq, nsub, seq):
    lane256 = lax.broadcasted_iota(jnp.int32, (1, 256), 1)
    head_of_lane = lane256 >> 6
    rows = N_HEADS * tq
    jj = lax.broadcasted_iota(jnp.int32, (tq, 2 * tq), 0)
    ss = lax.broadcasted_iota(jnp.int32, (tq, 2 * tq), 1)
    upper = ((jj > ss) | (ss >= tq)).astype(BF16)
    upper2 = jnp.concatenate([upper, upper], axis=0)
    t_in = lax.broadcasted_iota(jnp.int32, (rows, tq), 0) & (tq - 1)
    strict = lax.broadcasted_iota(jnp.int32, (rows, tq), 1) < t_in

    def tile(u, k0, carry, acc, valid):
        diagonal = carry is None
        kt = ck_ref[0, pl.ds(k0, tq), :]
        vt = cv_ref[0, pl.ds(k0, tq), :]
        z = _dot_t(qm_ref[u], kt)
        soft = jnp.log(1.0 + jnp.exp(-jnp.abs(z)))
        log_beta = jnp.minimum(z, 0.0) - soft
        log_1m = log_beta - z
        if diagonal:
            log_1m = jnp.where(strict, log_1m, 0.0)
        hi = log_1m.astype(BF16)
        lo = (log_1m - hi.astype(F32)).astype(BF16)
        sums = _dot(jnp.concatenate([hi, lo], axis=1), upper2)
        suffix = sums[:, :tq] if diagonal else sums[:, :tq] + carry
        a = jnp.exp(log_beta + suffix)
        if diagonal:
            a = jnp.where(strict, a, 0.0)
        if valid is not None:
            a = jnp.where(valid, a, 0.0)
        a_b = a.astype(BF16)
        a_cat = jnp.concatenate([a_b[h * tq:(h + 1) * tq, :] for h in range(N_HEADS)], axis=1)
        v_bd = jnp.concatenate([jnp.where(head_of_lane == h, vt, jnp.zeros_like(vt))
                                for h in range(N_HEADS)], axis=0)
        acc = acc + _dot(a_cat, v_bd)
        return (sums[:, tq:] if diagonal else carry + sums[:, tq:]), acc

    worst = []
    for u in range(nsub):
        qi = pl.program_id(1) * nsub + u
        q = cq_ref[0, u * tq:(u + 1) * tq, :].astype(F32) * QK_SCALE
        for h in range(N_HEADS):
            qm_ref[u, h * tq:(h + 1) * tq, :] = jnp.where(head_of_lane == h, q, 0.0).astype(BF16)
        carry, acc = tile(u, pl.multiple_of(qi * tq, tq), None, jnp.zeros((tq, 256), F32), None)
        for d in range(1, STICK_EAGER_TILES + 1):
            j = qi - d
            carry, acc = tile(u, pl.multiple_of(jnp.maximum(j, 0) * tq, tq), carry, acc, j >= 0)
        carry_ref[u] = carry
        o_ref[u] = acc
        worst.append(jnp.max(carry, axis=0, keepdims=True)[0, 0])

    for u in range(nsub):
        qi = pl.program_id(1) * nsub + u

        def cond(state):
            j, worst_carry = state
            return (j >= 0) & (worst_carry > STICK_EXIT)

        def body(state, u=u):
            j, _ = state
            new_carry, new_acc = tile(u, pl.multiple_of(j * tq, tq), carry_ref[u], o_ref[u], None)
            carry_ref[u] = new_carry
            o_ref[u] = new_acc
            return j - 1, jnp.max(new_carry, axis=0, keepdims=True)[0, 0]

        lax.while_loop(cond, body, (qi - 1 - STICK_EAGER_TILES, worst[u]))
        yc_ref[0, u * tq:(u + 1) * tq, :] = o_ref[u]


def _stick(cq, ck, cv, tq, nsub):
    b, s, _ = cq.shape
    step = tq * nsub
    return pl.pallas_call(
        functools.partial(_stick_kernel, tq=tq, nsub=nsub, seq=s),
        grid=(b, s // step),
        in_specs=[
            pl.BlockSpec((1, step, 256), lambda i, j: (i, j, 0)),
            pl.BlockSpec((1, s, 256), lambda i, j: (i, 0, 0)),
            pl.BlockSpec((1, s, 256), lambda i, j: (i, 0, 0)),
        ],
        out_specs=pl.BlockSpec((1, step, 256), lambda i, j: (i, j, 0)),
        out_shape=jax.ShapeDtypeStruct((b, s, 256), F32),
        scratch_shapes=[
            pltpu.VMEM((nsub, N_HEADS * tq, 256), BF16),
            pltpu.VMEM((nsub, N_HEADS * tq, tq), F32),
            pltpu.VMEM((nsub, tq, 256), F32),
        ],
        compiler_params=_params("arbitrary", "arbitrary"),
        name="stick_breaking",
    )(cq, ck, cv)


def _ret_kernel(dq_ref, dk_ref, dv_ref, dmat_ref, xi_ref, zeta_ref, decay_ref, bd_ref, yd_ref, r_ref):
    n = pl.program_id(1)

    @pl.when(n == 0)
    def _():
        r_ref[...] = jnp.zeros(r_ref.shape, F32)

    c = RET_CHUNK
    lane256 = lax.broadcasted_iota(jnp.int32, (1, 256), 1)
    head_of_lane = lane256 >> 6
    r = r_ref[...]
    for u in range(dq_ref.shape[1] // c):
        rows = slice(u * c, (u + 1) * c)
        q = dq_ref[0, rows, :].astype(F32) * QK_SCALE
        k = dk_ref[0, rows, :]
        v = dv_ref[0, rows, :]
        o = _dot(q.astype(BF16), r.astype(BF16)) * xi_ref[...]
        for h in range(N_HEADS):
            qh = jnp.where(head_of_lane == h, q, 0.0).astype(BF16)
            s = _dot_t(qh, k) * dmat_ref[h]
            o = o + jnp.where(head_of_lane == h, _dot(s.astype(BF16), v), 0.0)
        mu = jnp.zeros((c, 256), F32)
        for h in range(N_HEADS):
            in_h = head_of_lane == h
            mu = mu + jnp.where(in_h, jnp.sum(jnp.where(in_h, o, 0.0), axis=-1, keepdims=True), 0.0)
        cen = o - mu * (1.0 / HEAD_DIM)
        var = jnp.zeros((c, 256), F32)
        for h in range(N_HEADS):
            in_h = head_of_lane == h
            var = var + jnp.where(in_h, jnp.sum(jnp.where(in_h, cen * cen, 0.0), axis=-1, keepdims=True), 0.0)
        yd_ref[0, rows, :] = cen * lax.rsqrt(var * (1.0 / HEAD_DIM) + LN_EPS)
        kz = (k.astype(F32) * zeta_ref[...]).astype(BF16)
        upd = lax.dot_general(kz, v, (((0,), (0,)), ((), ())), preferred_element_type=F32)
        r = r * decay_ref[...] + upd * bd_ref[...]
    r_ref[...] = r


def _retention(dq, dk, dv, consts):
    b, s, _ = dq.shape
    c = RET_CHUNK
    dmat, xi, zeta, decay, bd = consts
    step = RET_CHUNKS_PER_STEP * c
    tile = pl.BlockSpec((1, step, 256), lambda i, j: (i, j, 0))
    return pl.pallas_call(
        _ret_kernel,
        grid=(b, s // step),
        in_specs=[
            tile, tile, tile,
            pl.BlockSpec((N_HEADS, c, c), lambda i, j: (0, 0, 0)),
            pl.BlockSpec((c, 256), lambda i, j: (0, 0)),
            pl.BlockSpec((c, 256), lambda i, j: (0, 0)),
            pl.BlockSpec((256, 256), lambda i, j: (0, 0)),
            pl.BlockSpec((256, 256), lambda i, j: (0, 0)),
        ],
        out_specs=tile,
        out_shape=jax.ShapeDtypeStruct((b, s, 256), F32),
        scratch_shapes=[pltpu.VMEM((256, 256), F32)],
        compiler_params=_params("arbitrary", "arbitrary"),
        name="retention",
    )(dq, dk, dv, dmat, xi, zeta, decay, bd)


def _retention_consts():
    c = RET_CHUNK
    log_g = jnp.log(1.0 - jnp.asarray(2.0 ** (-5.0 - np.arange(N_HEADS)), dtype=F32))
    i = jnp.arange(c)
    diff = (i[:, None] - i[None, :]).astype(F32)
    dmat = jnp.where(diff >= 0, jnp.exp(log_g[:, None, None] * jnp.maximum(diff, 0.0)), 0.0)
    zeta = jnp.exp(log_g[:, None] * (c - 1 - i)[None, :].astype(F32))
    xi = jnp.exp(log_g[:, None] * (i + 1)[None, :].astype(F32))
    g_chunk = jnp.exp(log_g * c)
    lanes = lambda hc: jnp.repeat(hc.T, HEAD_DIM, axis=1)
    head = np.arange(256) // HEAD_DIM
    bd = jnp.asarray(head[:, None] == head[None, :], F32)
    decay = jnp.repeat(g_chunk, HEAD_DIM)[:, None] * jnp.ones((1, 256), F32)
    return dmat, lanes(xi), lanes(zeta), decay, bd


def _merge_kernel(x_ref, mod_ref, g_ref, ya_ref, yb_ref, yc_ref, yd_ref, z_ref,
                  wm_ref, wbr_ref, wout_ref, fg_ref, o_ref, *, d, final):
    mod = mod_ref[0]
    tm = x_ref.shape[1]
    part = tm // MERGE_ROW_PARTS
    for r in range(MERGE_ROW_PARTS):
        rows = slice(r * part, (r + 1) * part)
        xf = x_ref[0, rows, :]
        h = _modulated_norm(xf, g_ref[...], mod, d).astype(BF16)
        merged = None
        for i, y_ref in enumerate((ya_ref, yb_ref, yc_ref, yd_ref)):
            yi = (y_ref[0, rows, :] * _silu(z_ref[0, rows, 256 * i:256 * (i + 1)])).astype(BF16)
            term = _sigmoid(_dot(h, wm_ref[i])) * _dot(yi, wbr_ref[i])
            merged = term if merged is None else merged + term
        out = xf + mod[:, 2 * d:3 * d] * _dot(merged.astype(BF16), wout_ref[...])
        if final:
            out = out * lax.rsqrt(jnp.mean(out * out, axis=-1, keepdims=True) + RMS_EPS) * fg_ref[...]
        o_ref[0, rows, :] = out


def _merge(x, mod, g, ys, z, wm, wbr, wout, final_g, final, tm):
    b, s, d = x.shape
    ytile = pl.BlockSpec((1, tm, 256), lambda i, j: (i, j, 0))
    return pl.pallas_call(
        functools.partial(_merge_kernel, d=d, final=final),
        grid=(b, s // tm),
        in_specs=[
            pl.BlockSpec((1, tm, d), lambda i, j: (i, j, 0)),
            pl.BlockSpec((1, 1, 3 * d), lambda i, j: (i, 0, 0)),
            pl.BlockSpec((1, d), lambda i, j: (0, 0)),
            ytile, ytile, ytile, ytile,
            pl.BlockSpec((1, tm, 4 * 256), lambda i, j: (i, j, 0)),
            pl.BlockSpec((4, d, d), lambda i, j: (0, 0, 0)),
            pl.BlockSpec((4, 256, d), lambda i, j: (0, 0, 0)),
            pl.BlockSpec((d, d), lambda i, j: (0, 0)),
            pl.BlockSpec((1, d), lambda i, j: (0, 0)),
        ],
        out_specs=pl.BlockSpec((1, tm, d), lambda i, j: (i, j, 0)),
        out_shape=jax.ShapeDtypeStruct((b, s, d), F32),
        compiler_params=_params("arbitrary", "arbitrary"),
        name="merge_out",
    )(x, mod, g.reshape(1, d), *ys, z, wm, wbr, wout, final_g.reshape(1, d))


def _proj_columns():
    a, bb, cc, dd = 0, 908, 1676, 2700
    perm = np.concatenate([np.arange(64) + 64 * h for h in SWA_HEAD_ORDER])
    pad = lambda n: np.full((n,), -1)
    cols = np.concatenate([
        np.arange(a, a + 256),
        np.arange(a + 256, a + 384),
        np.arange(a + 384, a + 640),
        np.arange(a + 640, a + 652), pad(116),
        np.arange(a + 652, a + 908),
        bb + 512 + perm,
        np.arange(cc + 768, cc + 1024),
        np.arange(dd + 768, dd + 1024),
        bb + perm,
        np.arange(bb + 256, bb + 512),
        np.arange(cc, cc + 768),
        np.arange(dd, dd + 768),
    ])
    assert cols.shape[0] == PROJ_W
    return cols, perm


def _take_columns(w, cols):
    pieces, i = [], 0
    while i < len(cols):
        j = i + 1
        if cols[i] < 0:
            while j < len(cols) and cols[j] < 0:
                j += 1
            pieces.append(jnp.zeros(w.shape[:-1] + (j - i,), w.dtype))
        else:
            while j < len(cols) and cols[j] == cols[j - 1] + 1:
                j += 1
            pieces.append(w[..., int(cols[i]):int(cols[j - 1]) + 1])
        i = j
    return jnp.concatenate(pieces, axis=-1)


def _position_features(s):
    pos = np.arange(s)
    feat = np.zeros((s, 64), np.float32)
    feat[:, 0] = SEL_BLOCK * (pos // SEL_BLOCK)
    feat[:, 1] = pos % SEL_BLOCK
    onehot = (pos[:, None] // SEL_BLOCK == np.arange(128)[None, :]).astype(np.float32)
    return jnp.asarray(feat, BF16), jnp.asarray(onehot, BF16)


def kernel(x, c, w_ada, b_ada, norm_g, w_in, cmp_pos, cmp_w1, cmp_w2, sink, w_merge, w_br, w_out, final_g):
    b, s, d = x.shape
    depth = w_ada.shape[0]
    tm = min(ROW_TILE, s)

    cols, perm = _proj_columns()
    w_in_p = _take_columns(w_in, cols).astype(BF16)
    wm_b = w_merge.astype(BF16)
    wbr_b = w_br.at[:, 1].set(w_br[:, 1][:, perm, :]).astype(BF16)
    wout_b = w_out.astype(BF16)
    pos_l, w1bd, w2bd = _compress_weights(cmp_pos, cmp_w1, cmp_w2)
    feat, onehot = _position_features(s)
    kconst = jnp.concatenate([jnp.zeros((s, HEAD_DIM), BF16), feat, onehot], axis=-1)
    ret_consts = _retention_consts()
    mods = _modulation(c, w_ada, b_ada)

    for l in range(depth):
        p = _project(x, mods[l], norm_g[l], w_in_p[l], kconst, tm)
        kc, vc = _compress(p["acmp"], pos_l[l], w1bd[l], w2bd[l])
        y_a = _nsa(p["aq"], p["ag"], kc, vc, p["ksel"], p["akv"], p["kwin"], NSA_TQ)
        y_b = _swa(sink[l], p["bq"], p["bkv"], SWA_TQ)
        y_c = _stick(p["cq"], p["ck"], p["cv"], STICK_TQ, STICK_SUBTILES)
        y_d = _retention(p["dq"], p["dk"], p["dv"], ret_consts)
        x = _merge(x, mods[l], norm_g[l], (y_a, y_b, y_c, y_d), p["z"], wm_b[l], wbr_b[l], wout_b[l],
                   final_g, l == depth - 1, tm)
    return x
```

```python
import functools

import numpy as np
import jax
import jax.numpy as jnp
from jax import lax
from jax.experimental import pallas as pl
from jax.experimental.pallas import tpu as pltpu

F32 = jnp.float32
BF16 = jnp.bfloat16

HEAD_DIM = 64
N_HEADS = 4
BRANCH_W = N_HEADS * HEAD_DIM
CMP_BLOCK = 32
SEL_BLOCK = 64
SEL_TOPK = 16
NSA_WINDOW = 512
NSA_TK = 512
SWA_WINDOW = 128
RET_CHUNK = 128
RMS_EPS = 1e-6
LN_EPS = 1e-5
NEG_INF = -1e30
TINY = 1e-30
FORCED_SCORE = 1e4
QK_SCALE = HEAD_DIM ** -0.5
STICK_EXIT = -110.0
STICK_EAGER_TILES = 2
MAX_BOUND_GAP = 60.0
BOUND_SLACK = 1.001

ROW_TILE = 512
MERGE_ROW_PARTS = 2
NSA_TQ = 256
NSA_SUB = 128
SWA_TQ = 1024
RET_CHUNKS_PER_STEP = 8
STICK_TQ = 128
STICK_SUBTILES = 4

VMEM_LIMIT = 56 * 1024 * 1024
SLOPES = tuple(float(2.0 ** (-8.0 * (h + 1) / N_HEADS)) for h in range(N_HEADS))

PROJ_OUTS = (
    ("aq", 0, 256, BF16),
    ("acmp", 256, 128, F32),
    ("akv", 384, 256, BF16),
    ("ag", 640, 128, F32),
    ("z", 768, 1024, F32),
    ("bq", 1792, 256, BF16),
    ("bkv", 2048, 256, BF16),
    ("cq", 2304, 256, BF16),
    ("ck", 2560, 256, BF16),
    ("cv", 2816, 256, BF16),
    ("dq", 3072, 256, BF16),
    ("dk", 3328, 256, BF16),
    ("dv", 3584, 256, BF16),
)
PROJ_W = 3840
SWA_HEAD_ORDER = (0, 2, 1, 3)


def _dot(a, b):
    return jnp.dot(a, b, preferred_element_type=F32)


def _dot_t(a, b):
    return lax.dot_general(a, b, (((1,), (1,)), ((), ())), preferred_element_type=F32)


def _dot_split(a, b):
    hi = a.astype(BF16)
    lo = (a - hi.astype(F32)).astype(BF16)
    return _dot(hi, b) + _dot(lo, b)


def _sigmoid(v):
    return 1.0 / (1.0 + jnp.exp(-v))


def _silu(v):
    return v * _sigmoid(v)


def _params(*sem):
    return pltpu.CompilerParams(dimension_semantics=sem, vmem_limit_bytes=VMEM_LIMIT)


def _mod_kernel(c_ref, w_ref, b_ref, o_ref):
    cc = c_ref[...]
    o_ref[0] = _dot(_silu(cc).astype(BF16), w_ref[0].astype(BF16)) + b_ref[0]


def _modulation(c, w_ada, b_ada):
    depth, d, n = w_ada.shape
    b = c.shape[0]
    rows = 8
    cp = jnp.zeros((rows, d), F32).at[:b].set(c)
    tn = 512
    out = pl.pallas_call(
        _mod_kernel,
        grid=(depth, n // tn),
        in_specs=[
            pl.BlockSpec((rows, d), lambda l, j: (0, 0)),
            pl.BlockSpec((1, d, tn), lambda l, j: (l, 0, j)),
            pl.BlockSpec((1, 1, tn), lambda l, j: (l, 0, j)),
        ],
        out_specs=pl.BlockSpec((1, rows, tn), lambda l, j: (l, 0, j)),
        out_shape=jax.ShapeDtypeStruct((depth, rows, n), F32),
        compiler_params=_params("arbitrary", "arbitrary"),
        name="adaln_mod",
    )(cp, w_ada, b_ada.reshape(depth, 1, n))
    return out[:, :b].reshape(depth, b, 1, n)


def _modulated_norm(xf, g, mod, d):
    ms = jnp.mean(xf * xf, axis=-1, keepdims=True)
    y = xf * lax.rsqrt(ms + RMS_EPS) * g
    return y * (1.0 + mod[:, d:2 * d]) + mod[:, 0:d]


def _proj_kernel(x_ref, mod_ref, g_ref, w_ref, kconst_ref, *out_refs, d):
    h = _modulated_norm(x_ref[0], g_ref[...], mod_ref[0], d).astype(BF16)
    ksel_ref, kwin_ref = out_refs[len(PROJ_OUTS):]
    for (name, start, width, dtype), o_ref in zip(PROJ_OUTS, out_refs):
        for c0 in range(0, width, 256):
            cw = min(256, width - c0)
            acc = _dot(h, w_ref[:, start + c0:start + c0 + cw])
            o_ref[0, :, c0:c0 + cw] = acc.astype(dtype)
            if name == "akv":
                ksel_ref[0] = kconst_ref[...]
                ksel_ref[0, :, 0:HEAD_DIM] = acc[:, 0:HEAD_DIM].astype(BF16)
                kwin_ref[0] = kconst_ref[:, 0:128]
                kwin_ref[0, :, 0:HEAD_DIM] = acc[:, 128:128 + HEAD_DIM].astype(BF16)


def _project(x, mod, g, w_p, kconst, tm):
    b, s, d = x.shape
    widths = [(wd, dt) for (_, _, wd, dt) in PROJ_OUTS] + [(256, BF16), (128, BF16)]
    out_shapes = [jax.ShapeDtypeStruct((b, s, wd), dt) for wd, dt in widths]
    out_specs = [pl.BlockSpec((1, tm, wd), lambda i, j: (i, j, 0)) for wd, _ in widths]
    outs = pl.pallas_call(
        functools.partial(_proj_kernel, d=d),
        grid=(b, s // tm),
        in_specs=[
            pl.BlockSpec((1, tm, d), lambda i, j: (i, j, 0)),
            pl.BlockSpec((1, 1, 3 * d), lambda i, j: (i, 0, 0)),
            pl.BlockSpec((1, d), lambda i, j: (0, 0)),
            pl.BlockSpec((d, PROJ_W), lambda i, j: (0, 0)),
            pl.BlockSpec((tm, 256), lambda i, j: (j, 0)),
        ],
        out_specs=out_specs,
        out_shape=out_shapes,
        compiler_params=_params("arbitrary", "arbitrary"),
        name="in_proj",
    )(x, mod, g.reshape(1, d), w_p, kconst)
    names = [name for (name, _, _, _) in PROJ_OUTS] + ["ksel", "kwin"]
    return dict(zip(names, outs))


def _compress_kernel(acmp_ref, pos_ref, w1_ref, w2_ref, kc_ref, vc_ref, *, nc):
    half = nc // 2
    for parity in range(2):
        hid = jnp.zeros((half, 128), F32)
        for i in range(CMP_BLOCK):
            z = acmp_ref[0, pl.ds(parity * CMP_BLOCK + i, half, stride=2 * CMP_BLOCK), :]
            hid = hid + _dot((z + pos_ref[i:i + 1, :]).astype(BF16), w1_ref[i])
        act = _silu(hid).astype(BF16)
        out_rows = slice(parity * half, (parity + 1) * half)
        kc_ref[0, out_rows, :] = _dot(act, w2_ref[0]).astype(BF16)
        vc_ref[0, out_rows, :] = _dot(act, w2_ref[1]).astype(BF16)


def _compress(acmp, pos, w1bd, w2bd):
    b, s, _ = acmp.shape
    nc = s // CMP_BLOCK
    spec_o = pl.BlockSpec((1, nc, 128), lambda i: (i, 0, 0))
    return pl.pallas_call(
        functools.partial(_compress_kernel, nc=nc),
        grid=(b,),
        in_specs=[
            pl.BlockSpec((1, s, 128), lambda i: (i, 0, 0)),
            pl.BlockSpec((CMP_BLOCK, 128), lambda i: (0, 0)),
            pl.BlockSpec((CMP_BLOCK, 128, 128), lambda i: (0, 0, 0)),
            pl.BlockSpec((2, 128, 128), lambda i: (0, 0, 0)),
        ],
        out_specs=[spec_o, spec_o],
        out_shape=[jax.ShapeDtypeStruct((b, nc, 128), BF16)] * 2,
        compiler_params=_params("arbitrary"),
        name="nsa_compress",
    )(acmp, pos, w1bd, w2bd)


def _compress_weights(cmp_pos, cmp_w1, cmp_w2):
    depth = cmp_pos.shape[0]
    hd = HEAD_DIM
    pos = jnp.concatenate([cmp_pos[:, 0], cmp_pos[:, 1]], axis=-1)
    w1 = cmp_w1.reshape(depth, 2, CMP_BLOCK, hd, hd)
    z = jnp.zeros((depth, CMP_BLOCK, hd, hd), F32)
    w1bd = jnp.concatenate([jnp.concatenate([w1[:, 0], z], axis=-1),
                            jnp.concatenate([z, w1[:, 1]], axis=-1)], axis=-2).astype(BF16)
    z2 = jnp.zeros((depth, hd, hd), F32)
    w2k = jnp.concatenate([jnp.concatenate([cmp_w2[:, 0], z2], axis=-1),
                           jnp.concatenate([z2, z2], axis=-1)], axis=-2)
    w2v = jnp.concatenate([jnp.concatenate([z2, z2], axis=-1),
                           jnp.concatenate([z2, cmp_w2[:, 1]], axis=-1)], axis=-2)
    return pos, w1bd, jnp.stack([w2k, w2v], axis=1).astype(BF16)


def _nsa_kernel(aq_ref, ag_ref, kc_ref, vc_ref, ksel_ref, vsel_ref, kwin_ref, vwin_ref, wmask_ref, gsel_ref, rowc_ref, ya_ref,
                qaug_ref, mx_ref, sm_ref, acc_ref, owin_ref, kmax_ref, *, tq, seq):
    qi = pl.program_id(1)
    qstart = qi * tq
    nc = seq // CMP_BLOCK
    half = nc // 2
    ns = seq // SEL_BLOCK
    t_lane = lax.broadcasted_iota(jnp.int32, (1, tq), 1) + qstart
    lane128 = lax.broadcasted_iota(jnp.int32, (1, 128), 1)
    nsub = tq // NSA_SUB
    row0 = lambda u, h: (u * N_HEADS + h) * NSA_SUB
    rows = N_HEADS * tq

    @pl.when(qi == 0)
    def _():
        def chunk(c, best):
            kk = ksel_ref[0, pl.ds(pl.multiple_of(c * NSA_TK, NSA_TK), NSA_TK), 0:128].astype(F32)
            sq = jnp.sum(jnp.where(lane128 < HEAD_DIM, kk * kk, 0.0), axis=-1, keepdims=True)
            return jnp.maximum(best, jnp.max(sq, axis=0, keepdims=True))
        best = lax.fori_loop(0, seq // NSA_TK, chunk, jnp.zeros((1, 1), F32))
        kmax_ref[0] = jnp.sqrt(best)[0, 0]

    row_c = lax.broadcasted_iota(jnp.int32, (nc, 1), 0)
    blk_c = jnp.where(row_c < half, 2 * row_c, 2 * (row_c - half) + 1)
    dist = t_lane - (blk_c * CMP_BLOCK + (CMP_BLOCK - 1))
    dist_f = dist.astype(F32)
    mask_c = dist >= 0
    kc = kc_ref[0]
    vc = vc_ref[0]
    psum = jnp.zeros((nc, tq), F32)
    ocmp = []
    for h in range(N_HEADS):
        grp = aq_ref[0, :, 128 * (h // 2):128 * (h // 2) + 128].astype(F32)
        if h % 2 == 1:
            grp = pltpu.roll(grp, 64, 1)
        left = jnp.where(lane128 < HEAD_DIM, grp * QK_SCALE,
                         jnp.where(lane128 < HEAD_DIM + 2, SLOPES[h], 0.0)).astype(BF16)
        for u in range(nsub):
            qaug_ref[row0(u, h):row0(u, h) + NSA_SUB, 0:128] = left[u * NSA_SUB:(u + 1) * NSA_SUB]
        s = _dot_t(kc, left) - SLOPES[h] * dist_f
        s = jnp.where(mask_c, s, NEG_INF)
        m = jnp.max(s, axis=0, keepdims=True)
        e = jnp.where(mask_c, jnp.exp(s - m), 0.0)
        den = jnp.sum(e, axis=0, keepdims=True)
        p = e * (1.0 / jnp.maximum(den, TINY))
        psum = psum + p
        ocmp.append(lax.dot_general(p.astype(BF16), vc, (((0,), (0,)), ((), ())),
                                    preferred_element_type=F32))

    everything = slice(0, rows)

    def fold(sc, at=everything):
        m = mx_ref[at, :]
        for g in range(sc.shape[1] // 128):
            m = jnp.maximum(m, sc[:, 128 * g:128 * (g + 1)])
        mx_ref[at, :] = m

    def accumulate(sc, v, at=everything):
        m = mx_ref[at, :]
        e = jnp.exp(sc - jnp.concatenate([m] * (sc.shape[1] // 128), axis=1))
        part = sm_ref[at, :]
        for g in range(sc.shape[1] // 128):
            part = part + e[:, 128 * g:128 * (g + 1)]
        sm_ref[at, :] = part
        acc_ref[at, :] += _dot(e.astype(BF16), v)

    own0 = pl.multiple_of(qstart, tq)

    def own_keys(use):
        for u in range(nsub):
            at = slice(row0(u, 0), row0(u, 0) + N_HEADS * NSA_SUB)
            n = (u + 1) * NSA_SUB
            sc = _dot_t(qaug_ref[at, 0:128], ksel_ref[0, pl.ds(own0, n), 0:128])
            last = sc[:, u * NSA_SUB:] + wmask_ref[1]
            sc = jnp.concatenate([sc[:, 0:u * NSA_SUB], last], axis=1) if u else last
            use(sc, vsel_ref[0, pl.ds(own0, n), :], at)

    def reset_sums():
        sm_ref[...] = jnp.zeros(sm_ref.shape, F32)
        acc_ref[...] = jnp.zeros(acc_ref.shape, F32)

    qpart = qaug_ref[:, 0:128].astype(F32)
    qnorm = jnp.sqrt(jnp.sum(jnp.where(lane128 < HEAD_DIM, qpart * qpart, 0.0), axis=-1, keepdims=True))
    reach = qnorm * (kmax_ref[0] * BOUND_SLACK)
    bound_ok = jnp.max(reach, axis=0, keepdims=True)[0, 0] * 2.0 <= MAX_BOUND_GAP
    mx_ref[...] = reach + (rowc_ref[1] + rowc_ref[0] * qstart.astype(F32))
    reset_sums()
    own_keys(accumulate)

    n_win = NSA_WINDOW + NSA_SUB
    full_chunks = NSA_WINDOW // NSA_SUB
    for u in range(nsub):
        q0 = qstart + u * NSA_SUB
        w0 = pl.multiple_of(jnp.clip(q0 - NSA_WINDOW, 0, seq - n_win), NSA_SUB)
        sub_rows = slice(row0(u, 0), row0(u, 0) + N_HEADS * NSA_SUB)
        sw = _dot_t(qaug_ref[sub_rows, 0:128], kwin_ref[0, pl.ds(w0, n_win), :])
        diag_chunk = (q0 - w0) >> (NSA_SUB.bit_length() - 1)
        pieces = []
        for c in range(n_win // NSA_SUB):
            d = diag_chunk - c
            pattern = jnp.where(d == 0, 1, jnp.where((d > 0) & (d < full_chunks), 2,
                                                     jnp.where(d == full_chunks, 3, 0)))
            pieces.append(sw[:, c * NSA_SUB:(c + 1) * NSA_SUB] + wmask_ref[pattern])
        sw = jnp.concatenate(pieces, axis=1)
        ew = jnp.exp(sw - jnp.max(sw, axis=-1, keepdims=True))
        owin_ref[sub_rows, :] = (_dot(ew.astype(BF16), vwin_ref[0, pl.ds(w0, n_win), :])
                                 * (1.0 / jnp.sum(ew, axis=-1, keepdims=True)))

    imp = psum[:half] + psum[half:]
    blk = lax.broadcasted_iota(jnp.int32, (ns, 1), 0)
    cur = t_lane >> 6
    future = blk * SEL_BLOCK > t_lane
    forced = (blk == 0) | (blk == cur) | (blk == cur - 1)
    score = jnp.where(forced, FORCED_SCORE, jnp.where(future, -1.0, imp))
    blk_f = blk.astype(F32)
    sel_t = jnp.zeros((ns, tq), F32)
    for _ in range(min(SEL_TOPK, ns)):
        mx = jnp.max(score, axis=0, keepdims=True)
        first = jnp.min(jnp.where(score == mx, blk_f, float(ns)), axis=0, keepdims=True)
        pick = blk_f == first
        sel_t = jnp.where(pick, 1.0, sel_t)
        score = jnp.where(pick, -jnp.inf, score)
    sel_f = jnp.transpose(sel_t)
    negmask = jnp.where(sel_f > 0.5, 0.0, NEG_INF).astype(BF16)
    if ns < 128:
        negmask = jnp.concatenate([negmask, jnp.zeros((tq, 128 - ns), BF16)], axis=1)
    for u in range(nsub):
        for h in range(N_HEADS):
            qaug_ref[row0(u, h):row0(u, h) + NSA_SUB, 128:256] = negmask[u * NSA_SUB:(u + 1) * NSA_SUB]

    if ns < 128:
        sel_f = jnp.concatenate([sel_f, jnp.zeros((tq, 128 - ns), F32)], axis=1)
    any_blk = jnp.max(sel_f, axis=0, keepdims=True)
    blocks_per_tile = NSA_TK // SEL_BLOCK
    shift = 1
    while shift < blocks_per_tile:
        any_blk = jnp.maximum(any_blk, pltpu.roll(any_blk, 128 - shift, 1))
        shift *= 2
    tile_of_lane = lane128 >> (blocks_per_tile.bit_length() - 1)
    first_of_tile = (lane128 & (blocks_per_tile - 1)) == 0
    pow2 = lax.bitcast_convert_type(((tile_of_lane & 15) + 127) << 23, F32)
    weighted = jnp.where(first_of_tile, any_blk * pow2, 0.0)
    bits_lo = jnp.sum(jnp.where(tile_of_lane < 16, weighted, 0.0), axis=1, keepdims=True).astype(jnp.int32)[0, 0]
    bits_hi = jnp.sum(jnp.where(tile_of_lane >= 16, weighted, 0.0), axis=1, keepdims=True).astype(jnp.int32)[0, 0]

    def tile_selected(t):
        return ((jnp.where(t < 16, bits_lo, bits_hi) >> (t & 15)) & 1) == 1

    pieces_per_tile = NSA_TK // tq
    t_own = qi // pieces_per_tile
    n_pieces = qi - t_own * pieces_per_tile

    def sweep(use):
        def piece(p, carry):
            k0 = pl.multiple_of(t_own * NSA_TK + p * tq, tq)
            use(_dot_t(qaug_ref[...], ksel_ref[0, pl.ds(k0, tq), :]), k0, tq)
            return carry

        def tile(t, carry):
            def visit():
                k0 = pl.multiple_of(t * NSA_TK, NSA_TK)
                use(_dot_t(qaug_ref[...], ksel_ref[0, pl.ds(k0, NSA_TK), :]), k0, NSA_TK)
            pl.when(tile_selected(t))(visit)
            return carry

        lax.fori_loop(0, n_pieces, piece, 0)
        lax.fori_loop(0, t_own, tile, 0)

    @pl.when(jnp.logical_not(bound_ok))
    def _():
        mx_ref[...] = jnp.full(mx_ref.shape, NEG_INF, F32)
        sweep(lambda sc, k0, n: fold(sc))
        own_keys(lambda sc, v, at: fold(sc, at))
        mx_ref[...] = jnp.broadcast_to(jnp.max(mx_ref[...], axis=-1, keepdims=True), mx_ref.shape)
        reset_sums()
        own_keys(accumulate)

    sweep(lambda sc, k0, n: accumulate(sc, vsel_ref[0, pl.ds(k0, n), :]))
    osel = acc_ref[...] * (1.0 / jnp.sum(sm_ref[...], axis=-1, keepdims=True))

    gate = _sigmoid(ag_ref[0])
    gate_hi = gate.astype(BF16)
    gate_lo = (gate - gate_hi.astype(F32)).astype(BF16)
    for u in range(nsub):
        q_rows = slice(u * NSA_SUB, (u + 1) * NSA_SUB)
        spread = _dot(jnp.concatenate([gate_hi[q_rows, :], gate_lo[q_rows, :]], axis=1), gsel_ref[...])
        comb = []
        for h in range(N_HEADS):
            head_rows = slice(row0(u, h), row0(u, h) + NSA_SUB)
            g_cmp, g_sel, g_win = (spread[:, 128 * (3 * h + br):128 * (3 * h + br + 1)] for br in range(3))
            comb.append(g_cmp * ocmp[h][q_rows, :] + g_sel * osel[head_rows, :]
                        + g_win * owin_ref[head_rows, :])
        for g in range(2):
            ya_ref[0, q_rows, 128 * g:128 * (g + 1)] = jnp.where(
                lane128 < HEAD_DIM, pltpu.roll(comb[2 * g], 64, 1), comb[2 * g + 1])


def _nsa(aq, ag, kc, vc, ksel, akv, kwin, tq):
    b, s, _ = aq.shape
    nc = kc.shape[1]
    rows = N_HEADS * tq
    whole = lambda w, blk: pl.BlockSpec((1, s, w), lambda i, j, blk=blk: (i, 0, blk))
    return pl.pallas_call(
        functools.partial(_nsa_kernel, tq=tq, seq=s),
        grid=(b, s // tq),
        in_specs=[
            pl.BlockSpec((1, tq, 256), lambda i, j: (i, j, 0)),
            pl.BlockSpec((1, tq, 128), lambda i, j: (i, j, 0)),
            pl.BlockSpec((1, nc, 128), lambda i, j: (i, 0, 0)),
            pl.BlockSpec((1, nc, 128), lambda i, j: (i, 0, 0)),
            whole(256, 0),
            whole(128, 0),
            whole(128, 0),
            whole(128, 1),
            pl.BlockSpec((4, N_HEADS * NSA_SUB, NSA_SUB), lambda i, j: (0, 0, 0)),
            pl.BlockSpec((256, 3 * N_HEADS * 128), lambda i, j: (0, 0)),
            pl.BlockSpec((2, rows, 128), lambda i, j: (0, 0, 0)),
        ],
        out_specs=pl.BlockSpec((1, tq, 256), lambda i, j: (i, j, 0)),
        out_shape=jax.ShapeDtypeStruct((b, s, 256), F32),
        scratch_shapes=[
            pltpu.VMEM((rows, 256), BF16),
            pltpu.VMEM((rows, 128), F32),
            pltpu.VMEM((rows, 128), F32),
            pltpu.VMEM((rows, 128), F32),
            pltpu.VMEM((rows, 128), F32),
            pltpu.SMEM((1,), F32),
        ],
        compiler_params=_params("arbitrary", "arbitrary"),
        name="nsa_attention",
    )(aq, ag, kc, vc, ksel, akv, kwin, akv, _window_mask_table(), _gate_selector(), _nsa_row_tables(tq))


def _nsa_row_tables(tq):
    row = np.arange(N_HEADS * tq)
    u, h, r = row // (N_HEADS * NSA_SUB), (row // NSA_SUB) % N_HEADS, row % NSA_SUB
    slope = np.asarray(SLOPES, np.float32)[h]
    tab = np.stack([slope, slope * (u * NSA_SUB + r).astype(np.float32)])
    return jnp.asarray(np.repeat(tab[:, :, None], 128, axis=2), F32)


def _gate_selector():
    k = np.arange(256)[:, None] % 128
    col = np.arange(3 * N_HEADS * 128)[None, :] // 128
    return jnp.asarray(k == col, BF16)


def _window_mask_table():
    r = np.arange(N_HEADS * NSA_SUB)[:, None] % NSA_SUB
    j = np.arange(NSA_SUB)[None, :]
    keep = np.stack([np.zeros_like(j <= r), j <= r, np.ones_like(j <= r), j > r])
    return jnp.asarray(np.where(keep, 0.0, NEG_INF), F32)


def _swa_kernel(sink_ref, bq_ref, bkv_ref, bias_ref, yb_ref, *, tq, seq):
    qi = pl.program_id(1)
    sub = SWA_WINDOW
    nk = 2 * sub
    lane128 = lax.broadcasted_iota(jnp.int32, (1, 128), 1)
    for u in range(tq // sub):
        qstart = qi * tq + u * sub
        k0 = pl.multiple_of(jnp.maximum(qstart - sub, 0), sub)
        at_start = jnp.where(qstart == 0, 1, 0)
        kk = bkv_ref[0, pl.ds(k0, nk), 0:128]
        vv = bkv_ref[0, pl.ds(k0, nk), 128:256]
        for g in range(2):
            qg = bq_ref[0, u * sub:(u + 1) * sub, 128 * g:128 * (g + 1)].astype(F32) * QK_SCALE
            outs = []
            for p in range(2):
                h = SWA_HEAD_ORDER[2 * g + p]
                in_half = (lane128 >= HEAD_DIM * p) & (lane128 < HEAD_DIM * (p + 1))
                qh = jnp.where(in_half, qg, 0.0).astype(BF16)
                s = _dot_t(kk, qh) + bias_ref[h, at_start]
                sink = sink_ref[h]
                m = jnp.maximum(jnp.max(s, axis=0, keepdims=True), sink)
                e = jnp.exp(s - m)
                den = jnp.sum(e, axis=0, keepdims=True) + jnp.exp(sink - m)
                pr = e * (1.0 / jnp.maximum(den, TINY))
                outs.append(lax.dot_general(pr.astype(BF16), vv, (((0,), (0,)), ((), ())),
                                            preferred_element_type=F32))
            yb_ref[0, u * sub:(u + 1) * sub, 128 * g:128 * (g + 1)] = jnp.where(
                lane128 < HEAD_DIM, outs[0], outs[1])


def _swa_bias_table():
    sub = SWA_WINDOW
    j = np.arange(2 * sub)[:, None]
    r = np.arange(sub)[None, :]
    rel = np.stack([r + sub - j, r - j])
    ok = (rel >= 0) & (rel < sub)
    slopes = np.asarray(SLOPES, np.float32)[:, None, None, None]
    return jnp.asarray(np.where(ok[None], -slopes * rel[None].astype(np.float32), NEG_INF), F32)


def _swa(sink, bq, bkv, tq):
    b, s, _ = bq.shape
    return pl.pallas_call(
        functools.partial(_swa_kernel, tq=tq, seq=s),
        grid=(b, s // tq),
        in_specs=[
            pl.BlockSpec(memory_space=pltpu.SMEM),
            pl.BlockSpec((1, tq, 256), lambda i, j: (i, j, 0)),
            pl.BlockSpec((1, s, 256), lambda i, j: (i, 0, 0)),
            pl.BlockSpec((N_HEADS, 2, 2 * SWA_WINDOW, SWA_WINDOW), lambda i, j: (0, 0, 0, 0)),
        ],
        out_specs=pl.BlockSpec((1, tq, 256), lambda i, j: (i, j, 0)),
        out_shape=jax.ShapeDtypeStruct((b, s, 256), F32),
        compiler_params=_params("arbitrary", "arbitrary"),
        name="swa_attention",
    )(sink, bq, bkv, _swa_bias_table())


def _stick_kernel(cq_ref, ck_ref, cv_ref, yc_ref, qm_ref, carry_ref, o_ref, *, tq, nsub, seq):
    lane256 = lax.broadcasted_iota(jnp.int32, (1, 256), 1)
    head_of_lane = lane256 >> 6
    rows = N_HEADS * tq
    jj = lax.broadcasted_iota(jnp.int32, (tq, 2 * tq), 0)
    ss = lax.broadcasted_iota(jnp.int32, (tq, 2 * tq), 1)
    upper = ((jj > ss) | (ss >= tq)).astype(BF16)
    upper2 = jnp.concatenate([upper, upper], axis=0)
    t_in = lax.broadcasted_iota(jnp.int32, (rows, tq), 0) & (tq - 1)
    strict = lax.broadcasted_iota(jnp.int32, (rows, tq), 1) < t_in

    def tile(u, k0, carry, acc, valid):
        diagonal = carry is None
        kt = ck_ref[0, pl.ds(k0, tq), :]
        vt = cv_ref[0, pl.ds(k0, tq), :]
        z = _dot_t(qm_ref[u], kt)
        soft = jnp.log(1.0 + jnp.exp(-jnp.abs(z)))
        log_beta = jnp.minimum(z, 0.0) - soft
        log_1m = log_beta - z
        if diagonal:
            log_1m = jnp.where(strict, log_1m, 0.0)
        hi = log_1m.astype(BF16)
        lo = (log_1m - hi.astype(F32)).astype(BF16)
        sums = _dot(jnp.concatenate([hi, lo], axis=1), upper2)
        suffix = sums[:, :tq] if diagonal else sums[:, :tq] + carry
        a = jnp.exp(log_beta + suffix)
        if diagonal:
            a = jnp.where(strict, a, 0.0)
        if valid is not None:
            a = jnp.where(valid, a, 0.0)
        a_b = a.astype(BF16)
        a_cat = jnp.concatenate([a_b[h * tq:(h + 1) * tq, :] for h in range(N_HEADS)], axis=1)
        v_bd = jnp.concatenate([jnp.where(head_of_lane == h, vt, jnp.zeros_like(vt))
                                for h in range(N_HEADS)], axis=0)
        acc = acc + _dot(a_cat, v_bd)
        return (sums[:, tq:] if diagonal else carry + sums[:, tq:]), acc

    worst = []
    for u in range(nsub):
        qi = pl.program_id(1) * nsub + u
        q = cq_ref[0, u * tq:(u + 1) * tq, :].astype(F32) * QK_SCALE
        for h in range(N_HEADS):
            qm_ref[u, h * tq:(h + 1) * tq, :] = jnp.where(head_of_lane == h, q, 0.0).astype(BF16)
        carry, acc = tile(u, pl.multiple_of(qi * tq, tq), None, jnp.zeros((tq, 256), F32), None)
        for d in range(1, STICK_EAGER_TILES + 1):
            j = qi - d
            carry, acc = tile(u, pl.multiple_of(jnp.maximum(j, 0) * tq, tq), carry, acc, j >= 0)
        carry_ref[u] = carry
        o_ref[u] = acc
        worst.append(jnp.max(carry, axis=0, keepdims=True)[0, 0])

    for u in range(nsub):
        qi = pl.program_id(1) * nsub + u

        def cond(state):
            j, worst_carry = state
            return (j >= 0) & (worst_carry > STICK_EXIT)

        def body(state, u=u):
            j, _ = state
            new_carry, new_acc = tile(u, pl.multiple_of(j * tq, tq), carry_ref[u], o_ref[u], None)
            carry_ref[u] = new_carry
            o_ref[u] = new_acc
            return j - 1, jnp.max(new_carry, axis=0, keepdims=True)[0, 0]

        lax.while_loop(cond, body, (qi - 1 - STICK_EAGER_TILES, worst[u]))
        yc_ref[0, u * tq:(u + 1) * tq, :] = o_ref[u]


def _stick(cq, ck, cv, tq, nsub):
    b, s, _ = cq.shape
    step = tq * nsub
    return pl.pallas_call(
        functools.partial(_stick_kernel, tq=tq, nsub=nsub, seq=s),
        grid=(b, s // step),
        in_specs=[
            pl.BlockSpec((1, step, 256), lambda i, j: (i, j, 0)),
            pl.BlockSpec((1, s, 256), lambda i, j: (i, 0, 0)),
            pl.BlockSpec((1, s, 256), lambda i, j: (i, 0, 0)),
        ],
        out_specs=pl.BlockSpec((1, step, 256), lambda i, j: (i, j, 0)),
        out_shape=jax.ShapeDtypeStruct((b, s, 256), F32),
        scratch_shapes=[
            pltpu.VMEM((nsub, N_HEADS * tq, 256), BF16),
            pltpu.VMEM((nsub, N_HEADS * tq, tq), F32),
            pltpu.VMEM((nsub, tq, 256), F32),
        ],
        compiler_params=_params("arbitrary", "arbitrary"),
        name="stick_breaking",
    )(cq, ck, cv)


def _ret_kernel(dq_ref, dk_ref, dv_ref, dmat_ref, xi_ref, zeta_ref, decay_ref, bd_ref, yd_ref, r_ref):
    n = pl.program_id(1)

    @pl.when(n == 0)
    def _():
        r_ref[...] = jnp.zeros(r_ref.shape, F32)

    c = RET_CHUNK
    lane256 = lax.broadcasted_iota(jnp.int32, (1, 256), 1)
    head_of_lane = lane256 >> 6
    r = r_ref[...]
    for u in range(dq_ref.shape[1] // c):
        rows = slice(u * c, (u + 1) * c)
        q = dq_ref[0, rows, :].astype(F32) * QK_SCALE
        k = dk_ref[0, rows, :]
        v = dv_ref[0, rows, :]
        o = _dot(q.astype(BF16), r.astype(BF16)) * xi_ref[...]
        for h in range(N_HEADS):
            qh = jnp.where(head_of_lane == h, q, 0.0).astype(BF16)
            s = _dot_t(qh, k) * dmat_ref[h]
            o = o + jnp.where(head_of_lane == h, _dot(s.astype(BF16), v), 0.0)
        mu = jnp.zeros((c, 256), F32)
        for h in range(N_HEADS):
            in_h = head_of_lane == h
            mu = mu + jnp.where(in_h, jnp.sum(jnp.where(in_h, o, 0.0), axis=-1, keepdims=True), 0.0)
        cen = o - mu * (1.0 / HEAD_DIM)
        var = jnp.zeros((c, 256), F32)
        for h in range(N_HEADS):
            in_h = head_of_lane == h
            var = var + jnp.where(in_h, jnp.sum(jnp.where(in_h, cen * cen, 0.0), axis=-1, keepdims=True), 0.0)
        yd_ref[0, rows, :] = cen * lax.rsqrt(var * (1.0 / HEAD_DIM) + LN_EPS)
        kz = (k.astype(F32) * zeta_ref[...]).astype(BF16)
        upd = lax.dot_general(kz, v, (((0,), (0,)), ((), ())), preferred_element_type=F32)
        r = r * decay_ref[...] + upd * bd_ref[...]
    r_ref[...] = r


def _retention(dq, dk, dv, consts):
    b, s, _ = dq.shape
    c = RET_CHUNK
    dmat, xi, zeta, decay, bd = consts
    step = RET_CHUNKS_PER_STEP * c
    tile = pl.BlockSpec((1, step, 256), lambda i, j: (i, j, 0))
    return pl.pallas_call(
        _ret_kernel,
        grid=(b, s // step),
        in_specs=[
            tile, tile, tile,
            pl.BlockSpec((N_HEADS, c, c), lambda i, j: (0, 0, 0)),
            pl.BlockSpec((c, 256), lambda i, j: (0, 0)),
            pl.BlockSpec((c, 256), lambda i, j: (0, 0)),
            pl.BlockSpec((256, 256), lambda i, j: (0, 0)),
            pl.BlockSpec((256, 256), lambda i, j: (0, 0)),
        ],
        out_specs=tile,
        out_shape=jax.ShapeDtypeStruct((b, s, 256), F32),
        scratch_shapes=[pltpu.VMEM((256, 256), F32)],
        compiler_params=_params("arbitrary", "arbitrary"),
        name="retention",
    )(dq, dk, dv, dmat, xi, zeta, decay, bd)


def _retention_consts():
    c = RET_CHUNK
    log_g = jnp.log(1.0 - jnp.asarray(2.0 ** (-5.0 - np.arange(N_HEADS)), dtype=F32))
    i = jnp.arange(c)
    diff = (i[:, None] - i[None, :]).astype(F32)
    dmat = jnp.where(diff >= 0, jnp.exp(log_g[:, None, None] * jnp.maximum(diff, 0.0)), 0.0)
    zeta = jnp.exp(log_g[:, None] * (c - 1 - i)[None, :].astype(F32))
    xi = jnp.exp(log_g[:, None] * (i + 1)[None, :].astype(F32))
    g_chunk = jnp.exp(log_g * c)
    lanes = lambda hc: jnp.repeat(hc.T, HEAD_DIM, axis=1)
    head = np.arange(256) // HEAD_DIM
    bd = jnp.asarray(head[:, None] == head[None, :], F32)
    decay = jnp.repeat(g_chunk, HEAD_DIM)[:, None] * jnp.ones((1, 256), F32)
    return dmat, lanes(xi), lanes(zeta), decay, bd


def _merge_kernel(x_ref, mod_ref, g_ref, ya_ref, yb_ref, yc_ref, yd_ref, z_ref,
                  wm_ref, wbr_ref, wout_ref, fg_ref, o_ref, *, d, final):
    mod = mod_ref[0]
    tm = x_ref.shape[1]
    part = tm // MERGE_ROW_PARTS
    for r in range(MERGE_ROW_PARTS):
        rows = slice(r * part, (r + 1) * part)
        xf = x_ref[0, rows, :]
        h = _modulated_norm(xf, g_ref[...], mod, d).astype(BF16)
        merged = None
        for i, y_ref in enumerate((ya_ref, yb_ref, yc_ref, yd_ref)):
            yi = (y_ref[0, rows, :] * _silu(z_ref[0, rows, 256 * i:256 * (i + 1)])).astype(BF16)
            term = _sigmoid(_dot(h, wm_ref[i])) * _dot(yi, wbr_ref[i])
            merged = term if merged is None else merged + term
        out = xf + mod[:, 2 * d:3 * d] * _dot(merged.astype(BF16), wout_ref[...])
        if final:
            out = out * lax.rsqrt(jnp.mean(out * out, axis=-1, keepdims=True) + RMS_EPS) * fg_ref[...]
        o_ref[0, rows, :] = out


def _merge(x, mod, g, ys, z, wm, wbr, wout, final_g, final, tm):
    b, s, d = x.shape
    ytile = pl.BlockSpec((1, tm, 256), lambda i, j: (i, j, 0))
    return pl.pallas_call(
        functools.partial(_merge_kernel, d=d, final=final),
        grid=(b, s // tm),
        in_specs=[
            pl.BlockSpec((1, tm, d), lambda i, j: (i, j, 0)),
            pl.BlockSpec((1, 1, 3 * d), lambda i, j: (i, 0, 0)),
            pl.BlockSpec((1, d), lambda i, j: (0, 0)),
            ytile, ytile, ytile, ytile,
            pl.BlockSpec((1, tm, 4 * 256), lambda i, j: (i, j, 0)),
            pl.BlockSpec((4, d, d), lambda i, j: (0, 0, 0)),
            pl.BlockSpec((4, 256, d), lambda i, j: (0, 0, 0)),
            pl.BlockSpec((d, d), lambda i, j: (0, 0)),
            pl.BlockSpec((1, d), lambda i, j: (0, 0)),
        ],
        out_specs=pl.BlockSpec((1, tm, d), lambda i, j: (i, j, 0)),
        out_shape=jax.ShapeDtypeStruct((b, s, d), F32),
        compiler_params=_params("arbitrary", "arbitrary"),
        name="merge_out",
    )(x, mod, g.reshape(1, d), *ys, z, wm, wbr, wout, final_g.reshape(1, d))


def _proj_columns():
    a, bb, cc, dd = 0, 908, 1676, 2700
    perm = np.concatenate([np.arange(64) + 64 * h for h in SWA_HEAD_ORDER])
    pad = lambda n: np.full((n,), -1)
    cols = np.concatenate([
        np.arange(a, a + 256),
        np.arange(a + 256, a + 384),
        np.arange(a + 384, a + 640),
        np.arange(a + 640, a + 652), pad(116),
        np.arange(a + 652, a + 908),
        bb + 512 + perm,
        np.arange(cc + 768, cc + 1024),
        np.arange(dd + 768, dd + 1024),
        bb + perm,
        np.arange(bb + 256, bb + 512),
        np.arange(cc, cc + 768),
        np.arange(dd, dd + 768),
    ])
    assert cols.shape[0] == PROJ_W
    return cols, perm


def _take_columns(w, cols):
    pieces, i = [], 0
    while i < len(cols):
        j = i + 1
        if cols[i] < 0:
            while j < len(cols) and cols[j] < 0:
                j += 1
            pieces.append(jnp.zeros(w.shape[:-1] + (j - i,), w.dtype))
        else:
            while j < len(cols) and cols[j] == cols[j - 1] + 1:
                j += 1
            pieces.append(w[..., int(cols[i]):int(cols[j - 1]) + 1])
        i = j
    return jnp.concatenate(pieces, axis=-1)


def _position_features(s):
    pos = np.arange(s)
    feat = np.zeros((s, 64), np.float32)
    feat[:, 0] = SEL_BLOCK * (pos // SEL_BLOCK)
    feat[:, 1] = pos % SEL_BLOCK
    onehot = (pos[:, None] // SEL_BLOCK == np.arange(128)[None, :]).astype(np.float32)
    return jnp.asarray(feat, BF16), jnp.asarray(onehot, BF16)


def kernel(x, c, w_ada, b_ada, norm_g, w_in, cmp_pos, cmp_w1, cmp_w2, sink, w_merge, w_br, w_out, final_g):
    b, s, d = x.shape
    depth = w_ada.shape[0]
    tm = min(ROW_TILE, s)

    cols, perm = _proj_columns()
    w_in_p = _take_columns(w_in, cols).astype(BF16)
    wm_b = w_merge.astype(BF16)
    wbr_b = w_br.at[:, 1].set(w_br[:, 1][:, perm, :]).astype(BF16)
    wout_b = w_out.astype(BF16)
    pos_l, w1bd, w2bd = _compress_weights(cmp_pos, cmp_w1, cmp_w2)
    feat, onehot = _position_features(s)
    kconst = jnp.concatenate([jnp.zeros((s, HEAD_DIM), BF16), feat, onehot], axis=-1)
    ret_consts = _retention_consts()
    mods = _modulation(c, w_ada, b_ada)

    for l in range(depth):
        p = _project(x, mods[l], norm_g[l], w_in_p[l], kconst, tm)
        kc, vc = _compress(p["acmp"], pos_l[l], w1bd[l], w2bd[l])
        y_a = _nsa(p["aq"], p["ag"], kc, vc, p["ksel"], p["akv"], p["kwin"], NSA_TQ)
        y_b = _swa(sink[l], p["bq"], p["bkv"], SWA_TQ)
        y_c = _stick(p["cq"], p["ck"], p["cv"], STICK_TQ, STICK_SUBTILES)
        y_d = _retention(p["dq"], p["dk"], p["dv"], ret_consts)
        x = _merge(x, mods[l], norm_g[l], (y_a, y_b, y_c, y_d), p["z"], wm_b[l], wbr_b[l], wout_b[l],
                   final_g, l == depth - 1, tm)
    return x
```

```python
import functools

import numpy as np
import jax
import jax.numpy as jnp
from jax import lax
from jax.experimental import pallas as pl
from jax.experimental.pallas import tpu as pltpu

F32 = jnp.float32
BF16 = jnp.bfloat16

HEAD_DIM = 64
N_HEADS = 4
BRANCH_W = N_HEADS * HEAD_DIM
CMP_BLOCK = 32
SEL_BLOCK = 64
SEL_TOPK = 16
NSA_WINDOW = 512
NSA_TK = 512
SWA_WINDOW = 128
RET_CHUNK = 128
RMS_EPS = 1e-6
LN_EPS = 1e-5
NEG_INF = -1e30
TINY = 1e-30
FORCED_SCORE = 1e4
QK_SCALE = HEAD_DIM ** -0.5
STICK_EXIT = -110.0
STICK_EAGER_TILES = 2
MAX_BOUND_GAP = 60.0
BOUND_SLACK = 1.001

ROW_TILE = 512
MERGE_ROW_PARTS = 2
NSA_TQ = 256
NSA_SUB = 128
SWA_TQ = 1024
RET_CHUNKS_PER_STEP = 8
STICK_TQ = 128
STICK_SUBTILES = 4

VMEM_LIMIT = 56 * 1024 * 1024
SLOPES = tuple(float(2.0 ** (-8.0 * (h + 1) / N_HEADS)) for h in range(N_HEADS))

PROJ_OUTS = (
    ("aq", 0, 256, BF16),
    ("acmp", 256, 128, F32),
    ("akv", 384, 256, BF16),
    ("ag", 640, 128, F32),
    ("z", 768, 1024, F32),
    ("bq", 1792, 256, BF16),
    ("bkv", 2048, 256, BF16),
    ("cq", 2304, 256, BF16),
    ("ck", 2560, 256, BF16),
    ("cv", 2816, 256, BF16),
    ("dq", 3072, 256, BF16),
    ("dk", 3328, 256, BF16),
    ("dv", 3584, 256, BF16),
)
PROJ_W = 3840
SWA_HEAD_ORDER = (0, 2, 1, 3)


def _dot(a, b):
    return jnp.dot(a, b, preferred_element_type=F32)


def _dot_t(a, b):
    return lax.dot_general(a, b, (((1,), (1,)), ((), ())), preferred_element_type=F32)


def _dot_split(a, b):
    hi = a.astype(BF16)
    lo = (a - hi.astype(F32)).astype(BF16)
    return _dot(hi, b) + _dot(lo, b)


def _sigmoid(v):
    return 1.0 / (1.0 + jnp.exp(-v))


def _silu(v):
    return v * _sigmoid(v)


def _params(*sem):
    return pltpu.CompilerParams(dimension_semantics=sem, vmem_limit_bytes=VMEM_LIMIT)


def _mod_kernel(c_ref, w_ref, b_ref, o_ref):
    cc = c_ref[...]
    o_ref[0] = _dot(_silu(cc).astype(BF16), w_ref[0].astype(BF16)) + b_ref[0]


def _modulation(c, w_ada, b_ada):
    depth, d, n = w_ada.shape
    b = c.shape[0]
    rows = 8
    cp = jnp.zeros((rows, d), F32).at[:b].set(c)
    tn = 512
    out = pl.pallas_call(
        _mod_kernel,
        grid=(depth, n // tn),
        in_specs=[
            pl.BlockSpec((rows, d), lambda l, j: (0, 0)),
            pl.BlockSpec((1, d, tn), lambda l, j: (l, 0, j)),
            pl.BlockSpec((1, 1, tn), lambda l, j: (l, 0, j)),
        ],
        out_specs=pl.BlockSpec((1, rows, tn), lambda l, j: (l, 0, j)),
        out_shape=jax.ShapeDtypeStruct((depth, rows, n), F32),
        compiler_params=_params("arbitrary", "arbitrary"),
        name="adaln_mod",
    )(cp, w_ada, b_ada.reshape(depth, 1, n))
    return out[:, :b].reshape(depth, b, 1, n)


def _modulated_norm(xf, g, mod, d):
    ms = jnp.mean(xf * xf, axis=-1, keepdims=True)
    y = xf * lax.rsqrt(ms + RMS_EPS) * g
    return y * (1.0 + mod[:, d:2 * d]) + mod[:, 0:d]


def _proj_kernel(x_ref, mod_ref, g_ref, w_ref, kconst_ref, *out_refs, d):
    h = _modulated_norm(x_ref[0], g_ref[...], mod_ref[0], d).astype(BF16)
    ksel_ref, kwin_ref = out_refs[len(PROJ_OUTS):]
    for (name, start, width, dtype), o_ref in zip(PROJ_OUTS, out_refs):
        for c0 in range(0, width, 256):
            cw = min(256, width - c0)
            acc = _dot(h, w_ref[:, start + c0:start + c0 + cw])
            o_ref[0, :, c0:c0 + cw] = acc.astype(dtype)
            if name == "akv":
                ksel_ref[0] = kconst_ref[...]
                ksel_ref[0, :, 0:HEAD_DIM] = acc[:, 0:HEAD_DIM].astype(BF16)
                kwin_ref[0] = kconst_ref[:, 0:128]
                kwin_ref[0, :, 0:HEAD_DIM] = acc[:, 128:128 + HEAD_DIM].astype(BF16)


def _project(x, mod, g, w_p, kconst, tm):
    b, s, d = x.shape
    widths = [(wd, dt) for (_, _, wd, dt) in PROJ_OUTS] + [(256, BF16), (128, BF16)]
    out_shapes = [jax.ShapeDtypeStruct((b, s, wd), dt) for wd, dt in widths]
    out_specs = [pl.BlockSpec((1, tm, wd), lambda i, j: (i, j, 0)) for wd, _ in widths]
    outs = pl.pallas_call(
        functools.partial(_proj_kernel, d=d),
        grid=(b, s // tm),
        in_specs=[
            pl.BlockSpec((1, tm, d), lambda i, j: (i, j, 0)),
            pl.BlockSpec((1, 1, 3 * d), lambda i, j: (i, 0, 0)),
            pl.BlockSpec((1, d), lambda i, j: (0, 0)),
            pl.BlockSpec((d, PROJ_W), lambda i, j: (0, 0)),
            pl.BlockSpec((tm, 256), lambda i, j: (j, 0)),
        ],
        out_specs=out_specs,
        out_shape=out_shapes,
        compiler_params=_params("arbitrary", "arbitrary"),
        name="in_proj",
    )(x, mod, g.reshape(1, d), w_p, kconst)
    names = [name for (name, _, _, _) in PROJ_OUTS] + ["ksel", "kwin"]
    return dict(zip(names, outs))


def _compress_kernel(acmp_ref, pos_ref, w1_ref, w2_ref, kc_ref, vc_ref, *, nc):
    half = nc // 2
    for parity in range(2):
        hid = jnp.zeros((half, 128), F32)
        for i in range(CMP_BLOCK):
            z = acmp_ref[0, pl.ds(parity * CMP_BLOCK + i, half, stride=2 * CMP_BLOCK), :]
            hid = hid + _dot((z + pos_ref[i:i + 1, :]).astype(BF16), w1_ref[i])
        act = _silu(hid).astype(BF16)
        out_rows = slice(parity * half, (parity + 1) * half)
        kc_ref[0, out_rows, :] = _dot(act, w2_ref[0]).astype(BF16)
        vc_ref[0, out_rows, :] = _dot(act, w2_ref[1]).astype(BF16)


def _compress(acmp, pos, w1bd, w2bd):
    b, s, _ = acmp.shape
    nc = s // CMP_BLOCK
    spec_o = pl.BlockSpec((1, nc, 128), lambda i: (i, 0, 0))
    return pl.pallas_call(
        functools.partial(_compress_kernel, nc=nc),
        grid=(b,),
        in_specs=[
            pl.BlockSpec((1, s, 128), lambda i: (i, 0, 0)),
            pl.BlockSpec((CMP_BLOCK, 128), lambda i: (0, 0)),
            pl.BlockSpec((CMP_BLOCK, 128, 128), lambda i: (0, 0, 0)),
            pl.BlockSpec((2, 128, 128), lambda i: (0, 0, 0)),
        ],
        out_specs=[spec_o, spec_o],
        out_shape=[jax.ShapeDtypeStruct((b, nc, 128), BF16)] * 2,
        compiler_params=_params("arbitrary"),
        name="nsa_compress",
    )(acmp, pos, w1bd, w2bd)


def _compress_weights(cmp_pos, cmp_w1, cmp_w2):
    depth = cmp_pos.shape[0]
    hd = HEAD_DIM
    pos = jnp.concatenate([cmp_pos[:, 0], cmp_pos[:, 1]], axis=-1)
    w1 = cmp_w1.reshape(depth, 2, CMP_BLOCK, hd, hd)
    z = jnp.zeros((depth, CMP_BLOCK, hd, hd), F32)
    w1bd = jnp.concatenate([jnp.concatenate([w1[:, 0], z], axis=-1),
                            jnp.concatenate([z, w1[:, 1]], axis=-1)], axis=-2).astype(BF16)
    z2 = jnp.zeros((depth, hd, hd), F32)
    w2k = jnp.concatenate([jnp.concatenate([cmp_w2[:, 0], z2], axis=-1),
                           jnp.concatenate([z2, z2], axis=-1)], axis=-2)
    w2v = jnp.concatenate([jnp.concatenate([z2, z2], axis=-1),
                           jnp.concatenate([z2, cmp_w2[:, 1]], axis=-1)], axis=-2)
    return pos, w1bd, jnp.stack([w2k, w2v], axis=1).astype(BF16)


def _nsa_kernel(aq_ref, ag_ref, kc_ref, vc_ref, ksel_ref, vsel_ref, kwin_ref, vwin_ref, wmask_ref, gsel_ref, rowc_ref, ya_ref,
                qaug_ref, mx_ref, sm_ref, acc_ref, owin_ref, kmax_ref, *, tq, seq):
    qi = pl.program_id(1)
    qstart = qi * tq
    nc = seq // CMP_BLOCK
    half = nc // 2
    ns = seq // SEL_BLOCK
    t_lane = lax.broadcasted_iota(jnp.int32, (1, tq), 1) + qstart
    lane128 = lax.broadcasted_iota(jnp.int32, (1, 128), 1)
    nsub = tq // NSA_SUB
    row0 = lambda u, h: (u * N_HEADS + h) * NSA_SUB
    rows = N_HEADS * tq

    @pl.when(qi == 0)
    def _():
        def chunk(c, best):
            kk = ksel_ref[0, pl.ds(pl.multiple_of(c * NSA_TK, NSA_TK), NSA_TK), 0:128].astype(F32)
            sq = jnp.sum(jnp.where(lane128 < HEAD_DIM, kk * kk, 0.0), axis=-1, keepdims=True)
            return jnp.maximum(best, jnp.max(sq, axis=0, keepdims=True))
        best = lax.fori_loop(0, seq // NSA_TK, chunk, jnp.zeros((1, 1), F32))
        kmax_ref[0] = jnp.sqrt(best)[0, 0]

    row_c = lax.broadcasted_iota(jnp.int32, (nc, 1), 0)
    blk_c = jnp.where(row_c < half, 2 * row_c, 2 * (row_c - half) + 1)
    dist = t_lane - (blk_c * CMP_BLOCK + (CMP_BLOCK - 1))
    dist_f = dist.astype(F32)
    mask_c = dist >= 0
    kc = kc_ref[0]
    vc = vc_ref[0]
    psum = jnp.zeros((nc, tq), F32)
    ocmp = []
    for h in range(N_HEADS):
        grp = aq_ref[0, :, 128 * (h // 2):128 * (h // 2) + 128].astype(F32)
        if h % 2 == 1:
            grp = pltpu.roll(grp, 64, 1)
        left = jnp.where(lane128 < HEAD_DIM, grp * QK_SCALE,
                         jnp.where(lane128 < HEAD_DIM + 2, SLOPES[h], 0.0)).astype(BF16)
        for u in range(nsub):
            qaug_ref[row0(u, h):row0(u, h) + NSA_SUB, 0:128] = left[u * NSA_SUB:(u + 1) * NSA_SUB]
        s = _dot_t(kc, left) - SLOPES[h] * dist_f
        s = jnp.where(mask_c, s, NEG_INF)
        m = jnp.max(s, axis=0, keepdims=True)
        e = jnp.where(mask_c, jnp.exp(s - m), 0.0)
        den = jnp.sum(e, axis=0, keepdims=True)
        p = e * (1.0 / jnp.maximum(den, TINY))
        psum = psum + p
        ocmp.append(lax.dot_general(p.astype(BF16), vc, (((0,), (0,)), ((), ())),
                                    preferred_element_type=F32))

    everything = slice(0, rows)

    def fold(sc, at=everything):
        width = sc.shape[1]
        if width < 128:
            mx_ref[at, 0:width] = jnp.maximum(mx_ref[at, 0:width], sc)
            return
        m = mx_ref[at, :]
        for g in range(width // 128):
            m = jnp.maximum(m, sc[:, 128 * g:128 * (g + 1)])
        mx_ref[at, :] = m

    def accumulate(sc, v, at=everything):
        width = sc.shape[1]
        if width < 128:
            e = jnp.exp(sc - mx_ref[at, 0:width])
            sm_ref[at, 0:width] += e
        else:
            m = mx_ref[at, :]
            e = jnp.exp(sc - jnp.concatenate([m] * (width // 128), axis=1))
            part = sm_ref[at, :]
            for g in range(width // 128):
                part = part + e[:, 128 * g:128 * (g + 1)]
            sm_ref[at, :] = part
        acc_ref[at, :] += _dot(e.astype(BF16), v)

    own0 = pl.multiple_of(qstart, tq)

    def own_keys(use):
        for u in range(nsub):
            at = slice(row0(u, 0), row0(u, 0) + N_HEADS * NSA_SUB)
            n = (u + 1) * NSA_SUB
            sc = _dot_t(qaug_ref[at, 0:128], ksel_ref[0, pl.ds(own0, n), 0:128])
            last = sc[:, u * NSA_SUB:] + wmask_ref[1]
            sc = jnp.concatenate([sc[:, 0:u * NSA_SUB], last], axis=1) if u else last
            use(sc, vsel_ref[0, pl.ds(own0, n), :], at)

    def reset_sums():
        sm_ref[...] = jnp.zeros(sm_ref.shape, F32)
        acc_ref[...] = jnp.zeros(acc_ref.shape, F32)

    qpart = qaug_ref[:, 0:128].astype(F32)
    qnorm = jnp.sqrt(jnp.sum(jnp.where(lane128 < HEAD_DIM, qpart * qpart, 0.0), axis=-1, keepdims=True))
    reach = qnorm * (kmax_ref[0] * BOUND_SLACK)
    bound_ok = jnp.max(reach, axis=0, keepdims=True)[0, 0] * 2.0 <= MAX_BOUND_GAP
    mx_ref[...] = reach + (rowc_ref[1] + rowc_ref[0] * qstart.astype(F32))
    reset_sums()
    own_keys(accumulate)

    n_win = NSA_WINDOW + NSA_SUB
    full_chunks = NSA_WINDOW // NSA_SUB
    for u in range(nsub):
        q0 = qstart + u * NSA_SUB
        w0 = pl.multiple_of(jnp.clip(q0 - NSA_WINDOW, 0, seq - n_win), NSA_SUB)
        sub_rows = slice(row0(u, 0), row0(u, 0) + N_HEADS * NSA_SUB)
        sw = _dot_t(qaug_ref[sub_rows, 0:128], kwin_ref[0, pl.ds(w0, n_win), :])
        diag_chunk = (q0 - w0) >> (NSA_SUB.bit_length() - 1)
        pieces = []
        for c in range(n_win // NSA_SUB):
            d = diag_chunk - c
            pattern = jnp.where(d == 0, 1, jnp.where((d > 0) & (d < full_chunks), 2,
                                                     jnp.where(d == full_chunks, 3, 0)))
            pieces.append(sw[:, c * NSA_SUB:(c + 1) * NSA_SUB] + wmask_ref[pattern])
        sw = jnp.concatenate(pieces, axis=1)
        ew = jnp.exp(sw - jnp.max(sw, axis=-1, keepdims=True))
        owin_ref[sub_rows, :] = (_dot(ew.astype(BF16), vwin_ref[0, pl.ds(w0, n_win), :])
                                 * (1.0 / jnp.sum(ew, axis=-1, keepdims=True)))

    imp = psum[:half] + psum[half:]
    blk = lax.broadcasted_iota(jnp.int32, (ns, 1), 0)
    cur = t_lane >> 6
    future = blk * SEL_BLOCK > t_lane
    forced = (blk == 0) | (blk == cur) | (blk == cur - 1)
    score = jnp.where(forced, FORCED_SCORE, jnp.where(future, -1.0, imp))
    blk_f = blk.astype(F32)
    sel_t = jnp.zeros((ns, tq), F32)
    for _ in range(min(SEL_TOPK, ns)):
        mx = jnp.max(score, axis=0, keepdims=True)
        first = jnp.min(jnp.where(score == mx, blk_f, float(ns)), axis=0, keepdims=True)
        pick = blk_f == first
        sel_t = jnp.where(pick, 1.0, sel_t)
        score = jnp.where(pick, -jnp.inf, score)
    sel_f = jnp.transpose(sel_t)
    negmask = jnp.where(sel_f > 0.5, 0.0, NEG_INF).astype(BF16)
    if ns < 128:
        negmask = jnp.concatenate([negmask, jnp.zeros((tq, 128 - ns), BF16)], axis=1)
    for u in range(nsub):
        for h in range(N_HEADS):
            qaug_ref[row0(u, h):row0(u, h) + NSA_SUB, 128:256] = negmask[u * NSA_SUB:(u + 1) * NSA_SUB]

    if ns < 128:
        sel_f = jnp.concatenate([sel_f, jnp.zeros((tq, 128 - ns), F32)], axis=1)
    any_blk = jnp.max(sel_f, axis=0, keepdims=True)
    blocks_per_tile = NSA_TK // SEL_BLOCK
    beyond_first = jnp.max(jnp.where((lane128 >= 1) & (lane128 < blocks_per_tile), any_blk, 0.0),
                           axis=1, keepdims=True)[0, 0] > 0.5
    any_blk = jnp.where(lane128 == 0, 0.0, any_blk)
    shift = 1
    while shift < blocks_per_tile:
        any_blk = jnp.maximum(any_blk, pltpu.roll(any_blk, 128 - shift, 1))
        shift *= 2
    tile_of_lane = lane128 >> (blocks_per_tile.bit_length() - 1)
    first_of_tile = (lane128 & (blocks_per_tile - 1)) == 0
    pow2 = lax.bitcast_convert_type(((tile_of_lane & 15) + 127) << 23, F32)
    weighted = jnp.where(first_of_tile, any_blk * pow2, 0.0)
    bits_lo = jnp.sum(jnp.where(tile_of_lane < 16, weighted, 0.0), axis=1, keepdims=True).astype(jnp.int32)[0, 0]
    bits_hi = jnp.sum(jnp.where(tile_of_lane >= 16, weighted, 0.0), axis=1, keepdims=True).astype(jnp.int32)[0, 0]

    def tile_selected(t):
        return ((jnp.where(t < 16, bits_lo, bits_hi) >> (t & 15)) & 1) == 1

    pieces_per_tile = NSA_TK // tq
    t_own = qi // pieces_per_tile
    n_pieces = qi - t_own * pieces_per_tile

    def sweep(use):
        def piece(p, carry):
            k0 = pl.multiple_of(t_own * NSA_TK + p * tq, tq)
            use(_dot_t(qaug_ref[...], ksel_ref[0, pl.ds(k0, tq), :]), k0, tq)
            return carry

        def tile(t, carry):
            def visit():
                k0 = pl.multiple_of(t * NSA_TK, NSA_TK)
                use(_dot_t(qaug_ref[...], ksel_ref[0, pl.ds(k0, NSA_TK), :]), k0, NSA_TK)
            pl.when(tile_selected(t))(visit)
            return carry

        def first_block():
            use(_dot_t(qaug_ref[:, 0:128], ksel_ref[0, 0:SEL_BLOCK, 0:128]), 0, SEL_BLOCK)

        lax.fori_loop(0, n_pieces, piece, 0)
        lax.fori_loop(0, t_own, tile, 0)
        pl.when((t_own >= 1) & jnp.logical_not(beyond_first))(first_block)

    @pl.when(jnp.logical_not(bound_ok))
    def _():
        mx_ref[...] = jnp.full(mx_ref.shape, NEG_INF, F32)
        sweep(lambda sc, k0, n: fold(sc))
        own_keys(lambda sc, v, at: fold(sc, at))
        mx_ref[...] = jnp.broadcast_to(jnp.max(mx_ref[...], axis=-1, keepdims=True), mx_ref.shape)
        reset_sums()
        own_keys(accumulate)

    sweep(lambda sc, k0, n: accumulate(sc, vsel_ref[0, pl.ds(k0, n), :]))
    osel = acc_ref[...] * (1.0 / jnp.sum(sm_ref[...], axis=-1, keepdims=True))

    gate = _sigmoid(ag_ref[0])
    gate_hi = gate.astype(BF16)
    gate_lo = (gate - gate_hi.astype(F32)).astype(BF16)
    for u in range(nsub):
        q_rows = slice(u * NSA_SUB, (u + 1) * NSA_SUB)
        spread = _dot(jnp.concatenate([gate_hi[q_rows, :], gate_lo[q_rows, :]], axis=1), gsel_ref[...])
        comb = []
        for h in range(N_HEADS):
            head_rows = slice(row0(u, h), row0(u, h) + NSA_SUB)
            g_cmp, g_sel, g_win = (spread[:, 128 * (3 * h + br):128 * (3 * h + br + 1)] for br in range(3))
            comb.append(g_cmp * ocmp[h][q_rows, :] + g_sel * osel[head_rows, :]
                        + g_win * owin_ref[head_rows, :])
        for g in range(2):
            ya_ref[0, q_rows, 128 * g:128 * (g + 1)] = jnp.where(
                lane128 < HEAD_DIM, pltpu.roll(comb[2 * g], 64, 1), comb[2 * g + 1])


def _nsa(aq, ag, kc, vc, ksel, akv, kwin, tq):
    b, s, _ = aq.shape
    nc = kc.shape[1]
    rows = N_HEADS * tq
    whole = lambda w, blk: pl.BlockSpec((1, s, w), lambda i, j, blk=blk: (i, 0, blk))
    return pl.pallas_call(
        functools.partial(_nsa_kernel, tq=tq, seq=s),
        grid=(b, s // tq),
        in_specs=[
            pl.BlockSpec((1, tq, 256), lambda i, j: (i, j, 0)),
            pl.BlockSpec((1, tq, 128), lambda i, j: (i, j, 0)),
            pl.BlockSpec((1, nc, 128), lambda i, j: (i, 0, 0)),
            pl.BlockSpec((1, nc, 128), lambda i, j: (i, 0, 0)),
            whole(256, 0),
            whole(128, 0),
            whole(128, 0),
            whole(128, 1),
            pl.BlockSpec((4, N_HEADS * NSA_SUB, NSA_SUB), lambda i, j: (0, 0, 0)),
            pl.BlockSpec((256, 3 * N_HEADS * 128), lambda i, j: (0, 0)),
            pl.BlockSpec((2, rows, 128), lambda i, j: (0, 0, 0)),
        ],
        out_specs=pl.BlockSpec((1, tq, 256), lambda i, j: (i, j, 0)),
        out_shape=jax.ShapeDtypeStruct((b, s, 256), F32),
        scratch_shapes=[
            pltpu.VMEM((rows, 256), BF16),
            pltpu.VMEM((rows, 128), F32),
            pltpu.VMEM((rows, 128), F32),
            pltpu.VMEM((rows, 128), F32),
            pltpu.VMEM((rows, 128), F32),
            pltpu.SMEM((1,), F32),
        ],
        compiler_params=_params("arbitrary", "arbitrary"),
        name="nsa_attention",
    )(aq, ag, kc, vc, ksel, akv, kwin, akv, _window_mask_table(), _gate_selector(), _nsa_row_tables(tq))


def _nsa_row_tables(tq):
    row = np.arange(N_HEADS * tq)
    u, h, r = row // (N_HEADS * NSA_SUB), (row // NSA_SUB) % N_HEADS, row % NSA_SUB
    slope = np.asarray(SLOPES, np.float32)[h]
    tab = np.stack([slope, slope * (u * NSA_SUB + r).astype(np.float32)])
    return jnp.asarray(np.repeat(tab[:, :, None], 128, axis=2), F32)


def _gate_selector():
    k = np.arange(256)[:, None] % 128
    col = np.arange(3 * N_HEADS * 128)[None, :] // 128
    return jnp.asarray(k == col, BF16)


def _window_mask_table():
    r = np.arange(N_HEADS * NSA_SUB)[:, None] % NSA_SUB
    j = np.arange(NSA_SUB)[None, :]
    keep = np.stack([np.zeros_like(j <= r), j <= r, np.ones_like(j <= r), j > r])
    return jnp.asarray(np.where(keep, 0.0, NEG_INF), F32)


def _swa_kernel(sink_ref, bq_ref, bkv_ref, bias_ref, yb_ref, *, tq, seq):
    qi = pl.program_id(1)
    sub = SWA_WINDOW
    nk = 2 * sub
    lane128 = lax.broadcasted_iota(jnp.int32, (1, 128), 1)
    for u in range(tq // sub):
        qstart = qi * tq + u * sub
        k0 = pl.multiple_of(jnp.maximum(qstart - sub, 0), sub)
        at_start = jnp.where(qstart == 0, 1, 0)
        kk = bkv_ref[0, pl.ds(k0, nk), 0:128]
        vv = bkv_ref[0, pl.ds(k0, nk), 128:256]
        for g in range(2):
            qg = bq_ref[0, u * sub:(u + 1) * sub, 128 * g:128 * (g + 1)].astype(F32) * QK_SCALE
            outs = []
            for p in range(2):
                h = SWA_HEAD_ORDER[2 * g + p]
                in_half = (lane128 >= HEAD_DIM * p) & (lane128 < HEAD_DIM * (p + 1))
                qh = jnp.where(in_half, qg, 0.0).astype(BF16)
                s = _dot_t(kk, qh) + bias_ref[h, at_start]
                sink = sink_ref[h]
                m = jnp.maximum(jnp.max(s, axis=0, keepdims=True), sink)
                e = jnp.exp(s - m)
                den = jnp.sum(e, axis=0, keepdims=True) + jnp.exp(sink - m)
                pr = e * (1.0 / jnp.maximum(den, TINY))
                outs.append(lax.dot_general(pr.astype(BF16), vv, (((0,), (0,)), ((), ())),
                                            preferred_element_type=F32))
            yb_ref[0, u * sub:(u + 1) * sub, 128 * g:128 * (g + 1)] = jnp.where(
                lane128 < HEAD_DIM, outs[0], outs[1])


def _swa_bias_table():
    sub = SWA_WINDOW
    j = np.arange(2 * sub)[:, None]
    r = np.arange(sub)[None, :]
    rel = np.stack([r + sub - j, r - j])
    ok = (rel >= 0) & (rel < sub)
    slopes = np.asarray(SLOPES, np.float32)[:, None, None, None]
    return jnp.asarray(np.where(ok[None], -slopes * rel[None].astype(np.float32), NEG_INF), F32)


def _swa(sink, bq, bkv, tq):
    b, s, _ = bq.shape
    return pl.pallas_call(
        functools.partial(_swa_kernel, tq=tq, seq=s),
        grid=(b, s // tq),
        in_specs=[
            pl.BlockSpec(memory_space=pltpu.SMEM),
            pl.BlockSpec((1, tq, 256), lambda i, j: (i, j, 0)),
            pl.BlockSpec((1, s, 256), lambda i, j: (i, 0, 0)),
            pl.BlockSpec((N_HEADS, 2, 2 * SWA_WINDOW, SWA_WINDOW), lambda i, j: (0, 0, 0, 0)),
        ],
        out_specs=pl.BlockSpec((1, tq, 256), lambda i, j: (i, j, 0)),
        out_shape=jax.ShapeDtypeStruct((b, s, 256), F32),
        compiler_params=_params("arbitrary", "arbitrary"),
        name="swa_attention",
    )(sink, bq, bkv, _swa_bias_table())


def _stick_kernel(cq_ref, ck_ref, cv_ref, yc_ref, qm_ref, carry_ref, o_ref, *, tq, nsub, seq):
    lane256 = lax.broadcasted_iota(jnp.int32, (1, 256), 1)
    head_of_lane = lane256 >> 6
    rows = N_HEADS * tq
    jj = lax.broadcasted_iota(jnp.int32, (tq, 2 * tq), 0)
    ss = lax.broadcasted_iota(jnp.int32, (tq, 2 * tq), 1)
    upper = ((jj > ss) | (ss >= tq)).astype(BF16)
    upper2 = jnp.concatenate([upper, upper], axis=0)
    t_in = lax.broadcasted_iota(jnp.int32, (rows, tq), 0) & (tq - 1)
    strict = lax.broadcasted_iota(jnp.int32, (rows, tq), 1) < t_in

    def tile(u, k0, carry, acc, valid):
        diagonal = carry is None
        kt = ck_ref[0, pl.ds(k0, tq), :]
        vt = cv_ref[0, pl.ds(k0, tq), :]
        z = _dot_t(qm_ref[u], kt)
        soft = jnp.log(1.0 + jnp.exp(-jnp.abs(z)))
        log_beta = jnp.minimum(z, 0.0) - soft
        log_1m = log_beta - z
        if diagonal:
            log_1m = jnp.where(strict, log_1m, 0.0)
        hi = log_1m.astype(BF16)
        lo = (log_1m - hi.astype(F32)).astype(BF16)
        sums = _dot(jnp.concatenate([hi, lo], axis=1), upper2)
        suffix = sums[:, :tq] if diagonal else sums[:, :tq] + carry
        a = jnp.exp(log_beta + suffix)
        if diagonal:
            a = jnp.where(strict, a, 0.0)
        if valid is not None:
            a = jnp.where(valid, a, 0.0)
        a_b = a.astype(BF16)
        a_cat = jnp.concatenate([a_b[h * tq:(h + 1) * tq, :] for h in range(N_HEADS)], axis=1)
        v_bd = jnp.concatenate([jnp.where(head_of_lane == h, vt, jnp.zeros_like(vt))
                                for h in range(N_HEADS)], axis=0)
        acc = acc + _dot(a_cat, v_bd)
        return (sums[:, tq:] if diagonal else carry + sums[:, tq:]), acc

    worst = []
    for u in range(nsub):
        qi = pl.program_id(1) * nsub + u
        q = cq_ref[0, u * tq:(u + 1) * tq, :].astype(F32) * QK_SCALE
        for h in range(N_HEADS):
            qm_ref[u, h * tq:(h + 1) * tq, :] = jnp.where(head_of_lane == h, q, 0.0).astype(BF16)
        carry, acc = tile(u, pl.multiple_of(qi * tq, tq), None, jnp.zeros((tq, 256), F32), None)
        for d in range(1, STICK_EAGER_TILES + 1):
            j = qi - d
            carry, acc = tile(u, pl.multiple_of(jnp.maximum(j, 0) * tq, tq), carry, acc, j >= 0)
        carry_ref[u] = carry
        o_ref[u] = acc
        worst.append(jnp.max(carry, axis=0, keepdims=True)[0, 0])

    for u in range(nsub):
        qi = pl.program_id(1) * nsub + u

        def cond(state):
            j, worst_carry = state
            return (j >= 0) & (worst_carry > STICK_EXIT)

        def body(state, u=u):
            j, _ = state
            new_carry, new_acc = tile(u, pl.multiple_of(j * tq, tq), carry_ref[u], o_ref[u], None)
            carry_ref[u] = new_carry
            o_ref[u] = new_acc
            return j - 1, jnp.max(new_carry, axis=0, keepdims=True)[0, 0]

        lax.while_loop(cond, body, (qi - 1 - STICK_EAGER_TILES, worst[u]))
        yc_ref[0, u * tq:(u + 1) * tq, :] = o_ref[u]


def _stick(cq, ck, cv, tq, nsub):
    b, s, _ = cq.shape
    step = tq * nsub
    return pl.pallas_call(
        functools.partial(_stick_kernel, tq=tq, nsub=nsub, seq=s),
        grid=(b, s // step),
        in_specs=[
            pl.BlockSpec((1, step, 256), lambda i, j: (i, j, 0)),
            pl.BlockSpec((1, s, 256), lambda i, j: (i, 0, 0)),
            pl.BlockSpec((1, s, 256), lambda i, j: (i, 0, 0)),
        ],
        out_specs=pl.BlockSpec((1, step, 256), lambda i, j: (i, j, 0)),
        out_shape=jax.ShapeDtypeStruct((b, s, 256), F32),
        scratch_shapes=[
            pltpu.VMEM((nsub, N_HEADS * tq, 256), BF16),
            pltpu.VMEM((nsub, N_HEADS * tq, tq), F32),
            pltpu.VMEM((nsub, tq, 256), F32),
        ],
        compiler_params=_params("arbitrary", "arbitrary"),
        name="stick_breaking",
    )(cq, ck, cv)


def _ret_kernel(dq_ref, dk_ref, dv_ref, dmat_ref, xi_ref, zeta_ref, decay_ref, bd_ref, yd_ref, r_ref):
    n = pl.program_id(1)

    @pl.when(n == 0)
    def _():
        r_ref[...] = jnp.zeros(r_ref.shape, F32)

    c = RET_CHUNK
    lane256 = lax.broadcasted_iota(jnp.int32, (1, 256), 1)
    head_of_lane = lane256 >> 6
    r = r_ref[...]
    for u in range(dq_ref.shape[1] // c):
        rows = slice(u * c, (u + 1) * c)
        q = dq_ref[0, rows, :].astype(F32) * QK_SCALE
        k = dk_ref[0, rows, :]
        v = dv_ref[0, rows, :]
        o = _dot(q.astype(BF16), r.astype(BF16)) * xi_ref[...]
        for h in range(N_HEADS):
            qh = jnp.where(head_of_lane == h, q, 0.0).astype(BF16)
            s = _dot_t(qh, k) * dmat_ref[h]
            o = o + jnp.where(head_of_lane == h, _dot(s.astype(BF16), v), 0.0)
        mu = jnp.zeros((c, 256), F32)
        for h in range(N_HEADS):
            in_h = head_of_lane == h
            mu = mu + jnp.where(in_h, jnp.sum(jnp.where(in_h, o, 0.0), axis=-1, keepdims=True), 0.0)
        cen = o - mu * (1.0 / HEAD_DIM)
        var = jnp.zeros((c, 256), F32)
        for h in range(N_HEADS):
            in_h = head_of_lane == h
            var = var + jnp.where(in_h, jnp.sum(jnp.where(in_h, cen * cen, 0.0), axis=-1, keepdims=True), 0.0)
        yd_ref[0, rows, :] = cen * lax.rsqrt(var * (1.0 / HEAD_DIM) + LN_EPS)
        kz = (k.astype(F32) * zeta_ref[...]).astype(BF16)
        upd = lax.dot_general(kz, v, (((0,), (0,)), ((), ())), preferred_element_type=F32)
        r = r * decay_ref[...] + upd * bd_ref[...]
    r_ref[...] = r


def _retention(dq, dk, dv, consts):
    b, s, _ = dq.shape
    c = RET_CHUNK
    dmat, xi, zeta, decay, bd = consts
    step = RET_CHUNKS_PER_STEP * c
    tile = pl.BlockSpec((1, step, 256), lambda i, j: (i, j, 0))
    return pl.pallas_call(
        _ret_kernel,
        grid=(b, s // step),
        in_specs=[
            tile, tile, tile,
            pl.BlockSpec((N_HEADS, c, c), lambda i, j: (0, 0, 0)),
            pl.BlockSpec((c, 256), lambda i, j: (0, 0)),
            pl.BlockSpec((c, 256), lambda i, j: (0, 0)),
            pl.BlockSpec((256, 256), lambda i, j: (0, 0)),
            pl.BlockSpec((256, 256), lambda i, j: (0, 0)),
        ],
        out_specs=tile,
        out_shape=jax.ShapeDtypeStruct((b, s, 256), F32),
        scratch_shapes=[pltpu.VMEM((256, 256), F32)],
        compiler_params=_params("arbitrary", "arbitrary"),
        name="retention",
    )(dq, dk, dv, dmat, xi, zeta, decay, bd)


def _retention_consts():
    c = RET_CHUNK
    log_g = jnp.log(1.0 - jnp.asarray(2.0 ** (-5.0 - np.arange(N_HEADS)), dtype=F32))
    i = jnp.arange(c)
    diff = (i[:, None] - i[None, :]).astype(F32)
    dmat = jnp.where(diff >= 0, jnp.exp(log_g[:, None, None] * jnp.maximum(diff, 0.0)), 0.0)
    zeta = jnp.exp(log_g[:, None] * (c - 1 - i)[None, :].astype(F32))
    xi = jnp.exp(log_g[:, None] * (i + 1)[None, :].astype(F32))
    g_chunk = jnp.exp(log_g * c)
    lanes = lambda hc: jnp.repeat(hc.T, HEAD_DIM, axis=1)
    head = np.arange(256) // HEAD_DIM
    bd = jnp.asarray(head[:, None] == head[None, :], F32)
    decay = jnp.repeat(g_chunk, HEAD_DIM)[:, None] * jnp.ones((1, 256), F32)
    return dmat, lanes(xi), lanes(zeta), decay, bd


def _merge_kernel(x_ref, mod_ref, g_ref, ya_ref, yb_ref, yc_ref, yd_ref, z_ref,
                  wm_ref, wbr_ref, wout_ref, fg_ref, o_ref, *, d, final):
    mod = mod_ref[0]
    tm = x_ref.shape[1]
    part = tm // MERGE_ROW_PARTS
    for r in range(MERGE_ROW_PARTS):
        rows = slice(r * part, (r + 1) * part)
        xf = x_ref[0, rows, :]
        h = _modulated_norm(xf, g_ref[...], mod, d).astype(BF16)
        merged = None
        for i, y_ref in enumerate((ya_ref, yb_ref, yc_ref, yd_ref)):
            yi = (y_ref[0, rows, :] * _silu(z_ref[0, rows, 256 * i:256 * (i + 1)])).astype(BF16)
            term = _sigmoid(_dot(h, wm_ref[i])) * _dot(yi, wbr_ref[i])
            merged = term if merged is None else merged + term
        out = xf + mod[:, 2 * d:3 * d] * _dot(merged.astype(BF16), wout_ref[...])
        if final:
            out = out * lax.rsqrt(jnp.mean(out * out, axis=-1, keepdims=True) + RMS_EPS) * fg_ref[...]
        o_ref[0, rows, :] = out


def _merge(x, mod, g, ys, z, wm, wbr, wout, final_g, final, tm):
    b, s, d = x.shape
    ytile = pl.BlockSpec((1, tm, 256), lambda i, j: (i, j, 0))
    return pl.pallas_call(
        functools.partial(_merge_kernel, d=d, final=final),
        grid=(b, s // tm),
        in_specs=[
            pl.BlockSpec((1, tm, d), lambda i, j: (i, j, 0)),
            pl.BlockSpec((1, 1, 3 * d), lambda i, j: (i, 0, 0)),
            pl.BlockSpec((1, d), lambda i, j: (0, 0)),
            ytile, ytile, ytile, ytile,
            pl.BlockSpec((1, tm, 4 * 256), lambda i, j: (i, j, 0)),
            pl.BlockSpec((4, d, d), lambda i, j: (0, 0, 0)),
            pl.BlockSpec((4, 256, d), lambda i, j: (0, 0, 0)),
            pl.BlockSpec((d, d), lambda i, j: (0, 0)),
            pl.BlockSpec((1, d), lambda i, j: (0, 0)),
        ],
        out_specs=pl.BlockSpec((1, tm, d), lambda i, j: (i, j, 0)),
        out_shape=jax.ShapeDtypeStruct((b, s, d), F32),
        compiler_params=_params("arbitrary", "arbitrary"),
        name="merge_out",
    )(x, mod, g.reshape(1, d), *ys, z, wm, wbr, wout, final_g.reshape(1, d))


def _proj_columns():
    a, bb, cc, dd = 0, 908, 1676, 2700
    perm = np.concatenate([np.arange(64) + 64 * h for h in SWA_HEAD_ORDER])
    pad = lambda n: np.full((n,), -1)
    cols = np.concatenate([
        np.arange(a, a + 256),
        np.arange(a + 256, a + 384),
        np.arange(a + 384, a + 640),
        np.arange(a + 640, a + 652), pad(116),
        np.arange(a + 652, a + 908),
        bb + 512 + perm,
        np.arange(cc + 768, cc + 1024),
        np.arange(dd + 768, dd + 1024),
        bb + perm,
        np.arange(bb + 256, bb + 512),
        np.arange(cc, cc + 768),
        np.arange(dd, dd + 768),
    ])
    assert cols.shape[0] == PROJ_W
    return cols, perm


def _take_columns(w, cols):
    pieces, i = [], 0
    while i < len(cols):
        j = i + 1
        if cols[i] < 0:
            while j < len(cols) and cols[j] < 0:
                j += 1
            pieces.append(jnp.zeros(w.shape[:-1] + (j - i,), w.dtype))
        else:
            while j < len(cols) and cols[j] == cols[j - 1] + 1:
                j += 1
            pieces.append(w[..., int(cols[i]):int(cols[j - 1]) + 1])
        i = j
    return jnp.concatenate(pieces, axis=-1)


def _position_features(s):
    pos = np.arange(s)
    feat = np.zeros((s, 64), np.float32)
    feat[:, 0] = SEL_BLOCK * (pos // SEL_BLOCK)
    feat[:, 1] = pos % SEL_BLOCK
    onehot = (pos[:, None] // SEL_BLOCK == np.arange(128)[None, :]).astype(np.float32)
    return jnp.asarray(feat, BF16), jnp.asarray(onehot, BF16)


def kernel(x, c, w_ada, b_ada, norm_g, w_in, cmp_pos, cmp_w1, cmp_w2, sink, w_merge, w_br, w_out, final_g):
    b, s, d = x.shape
    depth = w_ada.shape[0]
    tm = min(ROW_TILE, s)

    cols, perm = _proj_columns()
    w_in_p = _take_columns(w_in, cols).astype(BF16)
    wm_b = w_merge.astype(BF16)
    wbr_b = w_br.at[:, 1].set(w_br[:, 1][:, perm, :]).astype(BF16)
    wout_b = w_out.astype(BF16)
    pos_l, w1bd, w2bd = _compress_weights(cmp_pos, cmp_w1, cmp_w2)
    feat, onehot = _position_features(s)
    kconst = jnp.concatenate([jnp.zeros((s, HEAD_DIM), BF16), feat, onehot], axis=-1)
    ret_consts = _retention_consts()
    mods = _modulation(c, w_ada, b_ada)

    for l in range(depth):
        p = _project(x, mods[l], norm_g[l], w_in_p[l], kconst, tm)
        kc, vc = _compress(p["acmp"], pos_l[l], w1bd[l], w2bd[l])
        y_a = _nsa(p["aq"], p["ag"], kc, vc, p["ksel"], p["akv"], p["kwin"], NSA_TQ)
        y_b = _swa(sink[l], p["bq"], p["bkv"], SWA_TQ)
        y_c = _stick(p["cq"], p["ck"], p["cv"], STICK_TQ, STICK_SUBTILES)
        y_d = _retention(p["dq"], p["dk"], p["dv"], ret_consts)
        x = _merge(x, mods[l], norm_g[l], (y_a, y_b, y_c, y_d), p["z"], wm_b[l], wbr_b[l], wout_b[l],
                   final_g, l == depth - 1, tm)
    return x
```

```python
import functools

import numpy as np
import jax
import jax.numpy as jnp
from jax import lax
from jax.experimental import pallas as pl
from jax.experimental.pallas import tpu as pltpu

F32 = jnp.float32
BF16 = jnp.bfloat16

HEAD_DIM = 64
N_HEADS = 4
BRANCH_W = N_HEADS * HEAD_DIM
CMP_BLOCK = 32
SEL_BLOCK = 64
SEL_TOPK = 16
NSA_WINDOW = 512
NSA_TK = 512
SWA_WINDOW = 128
RET_CHUNK = 128
RMS_EPS = 1e-6
LN_EPS = 1e-5
NEG_INF = -1e30
TINY = 1e-30
FORCED_SCORE = 1e4
QK_SCALE = HEAD_DIM ** -0.5
STICK_EXIT = -110.0
STICK_EAGER_TILES = 2
MAX_BOUND_GAP = 60.0
BOUND_SLACK = 1.001

ROW_TILE = 512
MERGE_ROW_PARTS = 2
NSA_TQ = 256
NSA_SUB = 128
SWA_TQ = 1024
RET_CHUNKS_PER_STEP = 8
STICK_TQ = 128
STICK_SUBTILES = 4

VMEM_LIMIT = 56 * 1024 * 1024
SLOPES = tuple(float(2.0 ** (-8.0 * (h + 1) / N_HEADS)) for h in range(N_HEADS))

PROJ_OUTS = (
    ("aq", 0, 256, BF16),
    ("acmp", 256, 128, F32),
    ("akv", 384, 256, BF16),
    ("ag", 640, 128, F32),
    ("z", 768, 1024, F32),
    ("bq", 1792, 256, BF16),
    ("bkv", 2048, 256, BF16),
    ("cq", 2304, 256, BF16),
    ("ck", 2560, 256, BF16),
    ("cv", 2816, 256, BF16),
    ("dq", 3072, 256, BF16),
    ("dk", 3328, 256, BF16),
    ("dv", 3584, 256, BF16),
)
PROJ_W = 3840
SWA_HEAD_ORDER = (0, 2, 1, 3)


def _dot(a, b):
    return jnp.dot(a, b, preferred_element_type=F32)


def _dot_t(a, b):
    return lax.dot_general(a, b, (((1,), (1,)), ((), ())), preferred_element_type=F32)


def _dot_split(a, b):
    hi = a.astype(BF16)
    lo = (a - hi.astype(F32)).astype(BF16)
    return _dot(hi, b) + _dot(lo, b)


def _sigmoid(v):
    return 1.0 / (1.0 + jnp.exp(-v))


def _silu(v):
    return v * _sigmoid(v)


def _params(*sem):
    return pltpu.CompilerParams(dimension_semantics=sem, vmem_limit_bytes=VMEM_LIMIT)


def _mod_kernel(c_ref, w_ref, b_ref, o_ref):
    cc = c_ref[...]
    o_ref[0] = _dot(_silu(cc).astype(BF16), w_ref[0].astype(BF16)) + b_ref[0]


def _modulation(c, w_ada, b_ada):
    depth, d, n = w_ada.shape
    b = c.shape[0]
    rows = 8
    cp = jnp.zeros((rows, d), F32).at[:b].set(c)
    tn = 512
    out = pl.pallas_call(
        _mod_kernel,
        grid=(depth, n // tn),
        in_specs=[
            pl.BlockSpec((rows, d), lambda l, j: (0, 0)),
            pl.BlockSpec((1, d, tn), lambda l, j: (l, 0, j)),
            pl.BlockSpec((1, 1, tn), lambda l, j: (l, 0, j)),
        ],
        out_specs=pl.BlockSpec((1, rows, tn), lambda l, j: (l, 0, j)),
        out_shape=jax.ShapeDtypeStruct((depth, rows, n), F32),
        compiler_params=_params("arbitrary", "arbitrary"),
        name="adaln_mod",
    )(cp, w_ada, b_ada.reshape(depth, 1, n))
    return out[:, :b].reshape(depth, b, 1, n)


def _modulated_norm(xf, g, mod, d):
    ms = jnp.mean(xf * xf, axis=-1, keepdims=True)
    y = xf * lax.rsqrt(ms + RMS_EPS) * g
    return y * (1.0 + mod[:, d:2 * d]) + mod[:, 0:d]


def _proj_kernel(x_ref, mod_ref, g_ref, w_ref, kconst_ref, *out_refs, d):
    h = _modulated_norm(x_ref[0], g_ref[...], mod_ref[0], d).astype(BF16)
    ksel_ref, kwin_ref = out_refs[len(PROJ_OUTS):]
    for (name, start, width, dtype), o_ref in zip(PROJ_OUTS, out_refs):
        for c0 in range(0, width, 256):
            cw = min(256, width - c0)
            acc = _dot(h, w_ref[:, start + c0:start + c0 + cw])
            o_ref[0, :, c0:c0 + cw] = acc.astype(dtype)
            if name == "akv":
                ksel_ref[0] = kconst_ref[...]
                ksel_ref[0, :, 0:HEAD_DIM] = acc[:, 0:HEAD_DIM].astype(BF16)
                kwin_ref[0] = kconst_ref[:, 0:128]
                kwin_ref[0, :, 0:HEAD_DIM] = acc[:, 128:128 + HEAD_DIM].astype(BF16)


def _project(x, mod, g, w_p, kconst, tm):
    b, s, d = x.shape
    widths = [(wd, dt) for (_, _, wd, dt) in PROJ_OUTS] + [(256, BF16), (128, BF16)]
    out_shapes = [jax.ShapeDtypeStruct((b, s, wd), dt) for wd, dt in widths]
    out_specs = [pl.BlockSpec((1, tm, wd), lambda i, j: (i, j, 0)) for wd, _ in widths]
    outs = pl.pallas_call(
        functools.partial(_proj_kernel, d=d),
        grid=(b, s // tm),
        in_specs=[
            pl.BlockSpec((1, tm, d), lambda i, j: (i, j, 0)),
            pl.BlockSpec((1, 1, 3 * d), lambda i, j: (i, 0, 0)),
            pl.BlockSpec((1, d), lambda i, j: (0, 0)),
            pl.BlockSpec((d, PROJ_W), lambda i, j: (0, 0)),
            pl.BlockSpec((tm, 256), lambda i, j: (j, 0)),
        ],
        out_specs=out_specs,
        out_shape=out_shapes,
        compiler_params=_params("arbitrary", "arbitrary"),
        name="in_proj",
    )(x, mod, g.reshape(1, d), w_p, kconst)
    names = [name for (name, _, _, _) in PROJ_OUTS] + ["ksel", "kwin"]
    return dict(zip(names, outs))


def _compress_kernel(acmp_ref, pos_ref, w1_ref, w2_ref, cfeat_ref, kc_ref, vc_ref, *, nc):
    half = nc // 2
    for parity in range(2):
        hid = jnp.zeros((half, 128), F32)
        for i in range(CMP_BLOCK):
            z = acmp_ref[0, pl.ds(parity * CMP_BLOCK + i, half, stride=2 * CMP_BLOCK), :]
            hid = hid + _dot((z + pos_ref[i:i + 1, :]).astype(BF16), w1_ref[i])
        act = _silu(hid).astype(BF16)
        out_rows = slice(parity * half, (parity + 1) * half)
        kc_ref[0, out_rows, :] = (_dot(act, w2_ref[0]) + cfeat_ref[out_rows, :]).astype(BF16)
        vc_ref[0, out_rows, :] = _dot(act, w2_ref[1]).astype(BF16)


def _compress(acmp, pos, w1bd, w2bd, cfeat):
    b, s, _ = acmp.shape
    nc = s // CMP_BLOCK
    spec_o = pl.BlockSpec((1, nc, 128), lambda i: (i, 0, 0))
    return pl.pallas_call(
        functools.partial(_compress_kernel, nc=nc),
        grid=(b,),
        in_specs=[
            pl.BlockSpec((1, s, 128), lambda i: (i, 0, 0)),
            pl.BlockSpec((CMP_BLOCK, 128), lambda i: (0, 0)),
            pl.BlockSpec((CMP_BLOCK, 128, 128), lambda i: (0, 0, 0)),
            pl.BlockSpec((2, 128, 128), lambda i: (0, 0, 0)),
            pl.BlockSpec((nc, 128), lambda i: (0, 0)),
        ],
        out_specs=[spec_o, spec_o],
        out_shape=[jax.ShapeDtypeStruct((b, nc, 128), BF16)] * 2,
        compiler_params=_params("arbitrary"),
        name="nsa_compress",
    )(acmp, pos, w1bd, w2bd, cfeat)


def _compress_features(s):
    nc = s // CMP_BLOCK
    r = np.arange(nc)
    blk = np.where(r < nc // 2, 2 * r, 2 * (r - nc // 2) + 1)
    end = blk * CMP_BLOCK + CMP_BLOCK - 1
    feat = np.zeros((nc, 128), np.float32)
    feat[:, HEAD_DIM] = SEL_BLOCK * (end // SEL_BLOCK)
    feat[:, HEAD_DIM + 1] = end % SEL_BLOCK
    return jnp.asarray(feat)


def _compress_weights(cmp_pos, cmp_w1, cmp_w2):
    depth = cmp_pos.shape[0]
    hd = HEAD_DIM
    pos = jnp.concatenate([cmp_pos[:, 0], cmp_pos[:, 1]], axis=-1)
    w1 = cmp_w1.reshape(depth, 2, CMP_BLOCK, hd, hd)
    z = jnp.zeros((depth, CMP_BLOCK, hd, hd), F32)
    w1bd = jnp.concatenate([jnp.concatenate([w1[:, 0], z], axis=-1),
                            jnp.concatenate([z, w1[:, 1]], axis=-1)], axis=-2).astype(BF16)
    z2 = jnp.zeros((depth, hd, hd), F32)
    w2k = jnp.concatenate([jnp.concatenate([cmp_w2[:, 0], z2], axis=-1),
                           jnp.concatenate([z2, z2], axis=-1)], axis=-2)
    w2v = jnp.concatenate([jnp.concatenate([z2, z2], axis=-1),
                           jnp.concatenate([z2, cmp_w2[:, 1]], axis=-1)], axis=-2)
    return pos, w1bd, jnp.stack([w2k, w2v], axis=1).astype(BF16)


def _nsa_kernel(aq_ref, ag_ref, kc_ref, vc_ref, ksel_ref, vsel_ref, kwin_ref, vwin_ref, wmask_ref, gsel_ref, rowc_ref, ya_ref,
                qaug_ref, mx_ref, sm_ref, acc_ref, owin_ref, kmax_ref, *, tq, seq):
    qi = pl.program_id(1)
    qstart = qi * tq
    nc = seq // CMP_BLOCK
    half = nc // 2
    ns = seq // SEL_BLOCK
    t_lane = lax.broadcasted_iota(jnp.int32, (1, tq), 1) + qstart
    lane128 = lax.broadcasted_iota(jnp.int32, (1, 128), 1)
    nsub = tq // NSA_SUB
    row0 = lambda u, h: (u * N_HEADS + h) * NSA_SUB
    rows = N_HEADS * tq

    @pl.when(qi == 0)
    def _():
        def chunk(c, best):
            kk = ksel_ref[0, pl.ds(pl.multiple_of(c * NSA_TK, NSA_TK), NSA_TK), 0:128].astype(F32)
            sq = jnp.sum(jnp.where(lane128 < HEAD_DIM, kk * kk, 0.0), axis=-1, keepdims=True)
            return jnp.maximum(best, jnp.max(sq, axis=0, keepdims=True))
        best = lax.fori_loop(0, seq // NSA_TK, chunk, jnp.zeros((1, 1), F32))
        kmax_ref[0] = jnp.sqrt(best)[0, 0]

    row_c = lax.broadcasted_iota(jnp.int32, (nc, 1), 0)
    blk_c = jnp.where(row_c < half, 2 * row_c, 2 * (row_c - half) + 1)
    mask_c = t_lane >= blk_c * CMP_BLOCK + (CMP_BLOCK - 1)
    kc = kc_ref[0]
    vc = vc_ref[0]
    psum = jnp.zeros((nc, tq), F32)
    ocmp = []
    for h in range(N_HEADS):
        grp = aq_ref[0, :, 128 * (h // 2):128 * (h // 2) + 128].astype(F32)
        if h % 2 == 1:
            grp = pltpu.roll(grp, 64, 1)
        left = jnp.where(lane128 < HEAD_DIM, grp * QK_SCALE,
                         jnp.where(lane128 < HEAD_DIM + 2, SLOPES[h], 0.0)).astype(BF16)
        for u in range(nsub):
            qaug_ref[row0(u, h):row0(u, h) + NSA_SUB, 0:128] = left[u * NSA_SUB:(u + 1) * NSA_SUB]
        s = _dot_t(kc, left)
        s = jnp.where(mask_c, s, NEG_INF)
        m = jnp.max(s, axis=0, keepdims=True)
        e = jnp.where(mask_c, jnp.exp(s - m), 0.0)
        den = jnp.sum(e, axis=0, keepdims=True)
        p = e * (1.0 / jnp.maximum(den, TINY))
        psum = psum + p
        ocmp.append(lax.dot_general(p.astype(BF16), vc, (((0,), (0,)), ((), ())),
                                    preferred_element_type=F32))

    everything = slice(0, rows)

    def fold(sc, at=everything):
        width = sc.shape[1]
        if width < 128:
            mx_ref[at, 0:width] = jnp.maximum(mx_ref[at, 0:width], sc)
            return
        m = mx_ref[at, :]
        for g in range(width // 128):
            m = jnp.maximum(m, sc[:, 128 * g:128 * (g + 1)])
        mx_ref[at, :] = m

    def accumulate(sc, v, at=everything):
        width = sc.shape[1]
        if width < 128:
            e = jnp.exp(sc - mx_ref[at, 0:width])
            sm_ref[at, 0:width] += e
        else:
            m = mx_ref[at, :]
            e = jnp.exp(sc - jnp.concatenate([m] * (width // 128), axis=1))
            part = sm_ref[at, :]
            for g in range(width // 128):
                part = part + e[:, 128 * g:128 * (g + 1)]
            sm_ref[at, :] = part
        acc_ref[at, :] += _dot(e.astype(BF16), v)

    own0 = pl.multiple_of(qstart, tq)

    def own_keys(use):
        for u in range(nsub):
            at = slice(row0(u, 0), row0(u, 0) + N_HEADS * NSA_SUB)
            n = (u + 1) * NSA_SUB
            sc = _dot_t(qaug_ref[at, 0:128], ksel_ref[0, pl.ds(own0, n), 0:128])
            last = sc[:, u * NSA_SUB:] + wmask_ref[1]
            sc = jnp.concatenate([sc[:, 0:u * NSA_SUB], last], axis=1) if u else last
            use(sc, vsel_ref[0, pl.ds(own0, n), :], at)

    def reset_sums():
        sm_ref[...] = jnp.zeros(sm_ref.shape, F32)
        acc_ref[...] = jnp.zeros(acc_ref.shape, F32)

    qpart = qaug_ref[:, 0:128].astype(F32)
    qnorm = jnp.sqrt(jnp.sum(jnp.where(lane128 < HEAD_DIM, qpart * qpart, 0.0), axis=-1, keepdims=True))
    reach = qnorm * (kmax_ref[0] * BOUND_SLACK)
    bound_ok = jnp.max(reach, axis=0, keepdims=True)[0, 0] * 2.0 <= MAX_BOUND_GAP
    mx_ref[...] = reach + (rowc_ref[1] + rowc_ref[0] * qstart.astype(F32))
    reset_sums()
    own_keys(accumulate)

    n_win = NSA_WINDOW + NSA_SUB
    full_chunks = NSA_WINDOW // NSA_SUB
    for u in range(nsub):
        q0 = qstart + u * NSA_SUB
        w0 = pl.multiple_of(jnp.clip(q0 - NSA_WINDOW, 0, seq - n_win), NSA_SUB)
        sub_rows = slice(row0(u, 0), row0(u, 0) + N_HEADS * NSA_SUB)
        sw = _dot_t(qaug_ref[sub_rows, 0:128], kwin_ref[0, pl.ds(w0, n_win), :])
        diag_chunk = (q0 - w0) >> (NSA_SUB.bit_length() - 1)
        pieces = []
        for c in range(n_win // NSA_SUB):
            d = diag_chunk - c
            pattern = jnp.where(d == 0, 1, jnp.where((d > 0) & (d < full_chunks), 2,
                                                     jnp.where(d == full_chunks, 3, 0)))
            pieces.append(sw[:, c * NSA_SUB:(c + 1) * NSA_SUB] + wmask_ref[pattern])
        sw = jnp.concatenate(pieces, axis=1)
        ew = jnp.exp(sw - jnp.max(sw, axis=-1, keepdims=True))
        owin_ref[sub_rows, :] = (_dot(ew.astype(BF16), vwin_ref[0, pl.ds(w0, n_win), :])
                                 * (1.0 / jnp.sum(ew, axis=-1, keepdims=True)))

    imp = psum[:half] + psum[half:]
    blk = lax.broadcasted_iota(jnp.int32, (ns, 1), 0)
    cur = t_lane >> 6
    future = blk * SEL_BLOCK > t_lane
    forced = (blk == 0) | (blk == cur) | (blk == cur - 1)
    score = jnp.where(forced, FORCED_SCORE, jnp.where(future, -1.0, imp))
    blk_f = blk.astype(F32)
    for _ in range(min(SEL_TOPK, ns)):
        mx = jnp.max(score, axis=0, keepdims=True)
        first = jnp.min(jnp.where(score == mx, blk_f, float(ns)), axis=0, keepdims=True)
        score = jnp.where(blk_f == first, -jnp.inf, score)
    sel_f = jnp.transpose(jnp.where(score == -jnp.inf, 1.0, 0.0))
    negmask = jnp.where(sel_f > 0.5, 0.0, NEG_INF).astype(BF16)
    if ns < 128:
        negmask = jnp.concatenate([negmask, jnp.zeros((tq, 128 - ns), BF16)], axis=1)
    for u in range(nsub):
        for h in range(N_HEADS):
            qaug_ref[row0(u, h):row0(u, h) + NSA_SUB, 128:256] = negmask[u * NSA_SUB:(u + 1) * NSA_SUB]

    if ns < 128:
        sel_f = jnp.concatenate([sel_f, jnp.zeros((tq, 128 - ns), F32)], axis=1)
    any_blk = jnp.max(sel_f, axis=0, keepdims=True)
    blocks_per_tile = NSA_TK // SEL_BLOCK
    beyond_first = jnp.max(jnp.where((lane128 >= 1) & (lane128 < blocks_per_tile), any_blk, 0.0),
                           axis=1, keepdims=True)[0, 0] > 0.5
    any_blk = jnp.where(lane128 == 0, 0.0, any_blk)
    shift = 1
    while shift < blocks_per_tile:
        any_blk = jnp.maximum(any_blk, pltpu.roll(any_blk, 128 - shift, 1))
        shift *= 2
    tile_of_lane = lane128 >> (blocks_per_tile.bit_length() - 1)
    first_of_tile = (lane128 & (blocks_per_tile - 1)) == 0
    pow2 = lax.bitcast_convert_type(((tile_of_lane & 15) + 127) << 23, F32)
    weighted = jnp.where(first_of_tile, any_blk * pow2, 0.0)
    bits_lo = jnp.sum(jnp.where(tile_of_lane < 16, weighted, 0.0), axis=1, keepdims=True).astype(jnp.int32)[0, 0]
    bits_hi = jnp.sum(jnp.where(tile_of_lane >= 16, weighted, 0.0), axis=1, keepdims=True).astype(jnp.int32)[0, 0]

    def tile_selected(t):
        return ((jnp.where(t < 16, bits_lo, bits_hi) >> (t & 15)) & 1) == 1

    pieces_per_tile = NSA_TK // tq
    t_own = qi // pieces_per_tile
    n_pieces = qi - t_own * pieces_per_tile

    def sweep(use):
        def piece(p, carry):
            k0 = pl.multiple_of(t_own * NSA_TK + p * tq, tq)
            use(_dot_t(qaug_ref[...], ksel_ref[0, pl.ds(k0, tq), :]), k0, tq)
            return carry

        def tile(t, carry):
            def visit():
                k0 = pl.multiple_of(t * NSA_TK, NSA_TK)
                use(_dot_t(qaug_ref[...], ksel_ref[0, pl.ds(k0, NSA_TK), :]), k0, NSA_TK)
            pl.when(tile_selected(t))(visit)
            return carry

        def first_block():
            use(_dot_t(qaug_ref[:, 0:128], ksel_ref[0, 0:SEL_BLOCK, 0:128]), 0, SEL_BLOCK)

        lax.fori_loop(0, n_pieces, piece, 0)
        lax.fori_loop(0, t_own, tile, 0)
        pl.when((t_own >= 1) & jnp.logical_not(beyond_first))(first_block)

    @pl.when(jnp.logical_not(bound_ok))
    def _():
        mx_ref[...] = jnp.full(mx_ref.shape, NEG_INF, F32)
        sweep(lambda sc, k0, n: fold(sc))
        own_keys(lambda sc, v, at: fold(sc, at))
        mx_ref[...] = jnp.broadcast_to(jnp.max(mx_ref[...], axis=-1, keepdims=True), mx_ref.shape)
        reset_sums()
        own_keys(accumulate)

    sweep(lambda sc, k0, n: accumulate(sc, vsel_ref[0, pl.ds(k0, n), :]))
    osel = acc_ref[...] * (1.0 / jnp.sum(sm_ref[...], axis=-1, keepdims=True))

    gate = _sigmoid(ag_ref[0])
    gate_hi = gate.astype(BF16)
    gate_lo = (gate - gate_hi.astype(F32)).astype(BF16)
    for u in range(nsub):
        q_rows = slice(u * NSA_SUB, (u + 1) * NSA_SUB)
        spread = _dot(jnp.concatenate([gate_hi[q_rows, :], gate_lo[q_rows, :]], axis=1), gsel_ref[...])
        comb = []
        for h in range(N_HEADS):
            head_rows = slice(row0(u, h), row0(u, h) + NSA_SUB)
            g_cmp, g_sel, g_win = (spread[:, 128 * (3 * h + br):128 * (3 * h + br + 1)] for br in range(3))
            comb.append(g_cmp * ocmp[h][q_rows, :] + g_sel * osel[head_rows, :]
                        + g_win * owin_ref[head_rows, :])
        for g in range(2):
            ya_ref[0, q_rows, 128 * g:128 * (g + 1)] = jnp.where(
                lane128 < HEAD_DIM, pltpu.roll(comb[2 * g], 64, 1), comb[2 * g + 1])


def _nsa(aq, ag, kc, vc, ksel, akv, kwin, tq):
    b, s, _ = aq.shape
    nc = kc.shape[1]
    rows = N_HEADS * tq
    whole = lambda w, blk: pl.BlockSpec((1, s, w), lambda i, j, blk=blk: (i, 0, blk))
    return pl.pallas_call(
        functools.partial(_nsa_kernel, tq=tq, seq=s),
        grid=(b, s // tq),
        in_specs=[
            pl.BlockSpec((1, tq, 256), lambda i, j: (i, j, 0)),
            pl.BlockSpec((1, tq, 128), lambda i, j: (i, j, 0)),
            pl.BlockSpec((1, nc, 128), lambda i, j: (i, 0, 0)),
            pl.BlockSpec((1, nc, 128), lambda i, j: (i, 0, 0)),
            whole(256, 0),
            whole(128, 0),
            whole(128, 0),
            whole(128, 1),
            pl.BlockSpec((4, N_HEADS * NSA_SUB, NSA_SUB), lambda i, j: (0, 0, 0)),
            pl.BlockSpec((256, 3 * N_HEADS * 128), lambda i, j: (0, 0)),
            pl.BlockSpec((2, rows, 128), lambda i, j: (0, 0, 0)),
        ],
        out_specs=pl.BlockSpec((1, tq, 256), lambda i, j: (i, j, 0)),
        out_shape=jax.ShapeDtypeStruct((b, s, 256), F32),
        scratch_shapes=[
            pltpu.VMEM((rows, 256), BF16),
            pltpu.VMEM((rows, 128), F32),
            pltpu.VMEM((rows, 128), F32),
            pltpu.VMEM((rows, 128), F32),
            pltpu.VMEM((rows, 128), F32),
            pltpu.SMEM((1,), F32),
        ],
        compiler_params=_params("arbitrary", "arbitrary"),
        name="nsa_attention",
    )(aq, ag, kc, vc, ksel, akv, kwin, akv, _window_mask_table(), _gate_selector(), _nsa_row_tables(tq))


def _nsa_row_tables(tq):
    row = np.arange(N_HEADS * tq)
    u, h, r = row // (N_HEADS * NSA_SUB), (row // NSA_SUB) % N_HEADS, row % NSA_SUB
    slope = np.asarray(SLOPES, np.float32)[h]
    tab = np.stack([slope, slope * (u * NSA_SUB + r).astype(np.float32)])
    return jnp.asarray(np.repeat(tab[:, :, None], 128, axis=2), F32)


def _gate_selector():
    k = np.arange(256)[:, None] % 128
    col = np.arange(3 * N_HEADS * 128)[None, :] // 128
    return jnp.asarray(k == col, BF16)


def _window_mask_table():
    r = np.arange(N_HEADS * NSA_SUB)[:, None] % NSA_SUB
    j = np.arange(NSA_SUB)[None, :]
    keep = np.stack([np.zeros_like(j <= r), j <= r, np.ones_like(j <= r), j > r])
    return jnp.asarray(np.where(keep, 0.0, NEG_INF), F32)


def _swa_kernel(sink_ref, bq_ref, bkv_ref, bias_ref, yb_ref, *, tq, seq):
    qi = pl.program_id(1)
    sub = SWA_WINDOW
    nk = 2 * sub
    lane128 = lax.broadcasted_iota(jnp.int32, (1, 128), 1)
    for u in range(tq // sub):
        qstart = qi * tq + u * sub
        k0 = pl.multiple_of(jnp.maximum(qstart - sub, 0), sub)
        at_start = jnp.where(qstart == 0, 1, 0)
        kk = bkv_ref[0, pl.ds(k0, nk), 0:128]
        vv = bkv_ref[0, pl.ds(k0, nk), 128:256]
        for g in range(2):
            qg = bq_ref[0, u * sub:(u + 1) * sub, 128 * g:128 * (g + 1)].astype(F32) * QK_SCALE
            outs = []
            for p in range(2):
                h = SWA_HEAD_ORDER[2 * g + p]
                in_half = (lane128 >= HEAD_DIM * p) & (lane128 < HEAD_DIM * (p + 1))
                qh = jnp.where(in_half, qg, 0.0).astype(BF16)
                s = _dot_t(kk, qh) + bias_ref[h, at_start]
                sink = sink_ref[h]
                m = jnp.maximum(jnp.max(s, axis=0, keepdims=True), sink)
                e = jnp.exp(s - m)
                den = jnp.sum(e, axis=0, keepdims=True) + jnp.exp(sink - m)
                pr = e * (1.0 / jnp.maximum(den, TINY))
                outs.append(lax.dot_general(pr.astype(BF16), vv, (((0,), (0,)), ((), ())),
                                            preferred_element_type=F32))
            yb_ref[0, u * sub:(u + 1) * sub, 128 * g:128 * (g + 1)] = jnp.where(
                lane128 < HEAD_DIM, outs[0], outs[1])


def _swa_bias_table():
    sub = SWA_WINDOW
    j = np.arange(2 * sub)[:, None]
    r = np.arange(sub)[None, :]
    rel = np.stack([r + sub - j, r - j])
    ok = (rel >= 0) & (rel < sub)
    slopes = np.asarray(SLOPES, np.float32)[:, None, None, None]
    return jnp.asarray(np.where(ok[None], -slopes * rel[None].astype(np.float32), NEG_INF), F32)


def _swa(sink, bq, bkv, tq):
    b, s, _ = bq.shape
    return pl.pallas_call(
        functools.partial(_swa_kernel, tq=tq, seq=s),
        grid=(b, s // tq),
        in_specs=[
            pl.BlockSpec(memory_space=pltpu.SMEM),
            pl.BlockSpec((1, tq, 256), lambda i, j: (i, j, 0)),
            pl.BlockSpec((1, s, 256), lambda i, j: (i, 0, 0)),
            pl.BlockSpec((N_HEADS, 2, 2 * SWA_WINDOW, SWA_WINDOW), lambda i, j: (0, 0, 0, 0)),
        ],
        out_specs=pl.BlockSpec((1, tq, 256), lambda i, j: (i, j, 0)),
        out_shape=jax.ShapeDtypeStruct((b, s, 256), F32),
        compiler_params=_params("arbitrary", "arbitrary"),
        name="swa_attention",
    )(sink, bq, bkv, _swa_bias_table())


def _stick_kernel(cq_ref, ck_ref, cv_ref, yc_ref, qm_ref, carry_ref, o_ref, *, tq, nsub, seq):
    lane256 = lax.broadcasted_iota(jnp.int32, (1, 256), 1)
    head_of_lane = lane256 >> 6
    rows = N_HEADS * tq
    jj = lax.broadcasted_iota(jnp.int32, (tq, 2 * tq), 0)
    ss = lax.broadcasted_iota(jnp.int32, (tq, 2 * tq), 1)
    upper = ((jj > ss) | (ss >= tq)).astype(BF16)
    upper2 = jnp.concatenate([upper, upper], axis=0)
    t_in = lax.broadcasted_iota(jnp.int32, (rows, tq), 0) & (tq - 1)
    strict = lax.broadcasted_iota(jnp.int32, (rows, tq), 1) < t_in

    def tile(u, k0, carry, acc, valid):
        diagonal = carry is None
        kt = ck_ref[0, pl.ds(k0, tq), :]
        vt = cv_ref[0, pl.ds(k0, tq), :]
        z = _dot_t(qm_ref[u], kt)
        soft = jnp.log(1.0 + jnp.exp(-jnp.abs(z)))
        log_beta = jnp.minimum(z, 0.0) - soft
        log_1m = log_beta - z
        if diagonal:
            log_1m = jnp.where(strict, log_1m, 0.0)
        hi = log_1m.astype(BF16)
        lo = (log_1m - hi.astype(F32)).astype(BF16)
        sums = _dot(jnp.concatenate([hi, lo], axis=1), upper2)
        suffix = sums[:, :tq] if diagonal else sums[:, :tq] + carry
        a = jnp.exp(log_beta + suffix)
        if diagonal:
            a = jnp.where(strict, a, 0.0)
        if valid is not None:
            a = jnp.where(valid, a, 0.0)
        a_b = a.astype(BF16)
        a_cat = jnp.concatenate([a_b[h * tq:(h + 1) * tq, :] for h in range(N_HEADS)], axis=1)
        v_bd = jnp.concatenate([jnp.where(head_of_lane == h, vt, jnp.zeros_like(vt))
                                for h in range(N_HEADS)], axis=0)
        acc = acc + _dot(a_cat, v_bd)
        return (sums[:, tq:] if diagonal else carry + sums[:, tq:]), acc

    worst = []
    for u in range(nsub):
        qi = pl.program_id(1) * nsub + u
        q = cq_ref[0, u * tq:(u + 1) * tq, :].astype(F32) * QK_SCALE
        for h in range(N_HEADS):
            qm_ref[u, h * tq:(h + 1) * tq, :] = jnp.where(head_of_lane == h, q, 0.0).astype(BF16)
        carry, acc = tile(u, pl.multiple_of(qi * tq, tq), None, jnp.zeros((tq, 256), F32), None)
        for d in range(1, STICK_EAGER_TILES + 1):
            j = qi - d
            carry, acc = tile(u, pl.multiple_of(jnp.maximum(j, 0) * tq, tq), carry, acc, j >= 0)
        carry_ref[u] = carry
        o_ref[u] = acc
        worst.append(jnp.max(carry, axis=0, keepdims=True)[0, 0])

    for u in range(nsub):
        qi = pl.program_id(1) * nsub + u

        def cond(state):
            j, worst_carry = state
            return (j >= 0) & (worst_carry > STICK_EXIT)

        def body(state, u=u):
            j, _ = state
            new_carry, new_acc = tile(u, pl.multiple_of(j * tq, tq), carry_ref[u], o_ref[u], None)
            carry_ref[u] = new_carry
            o_ref[u] = new_acc
            return j - 1, jnp.max(new_carry, axis=0, keepdims=True)[0, 0]

        lax.while_loop(cond, body, (qi - 1 - STICK_EAGER_TILES, worst[u]))
        yc_ref[0, u * tq:(u + 1) * tq, :] = o_ref[u]


def _stick(cq, ck, cv, tq, nsub):
    b, s, _ = cq.shape
    step = tq * nsub
    return pl.pallas_call(
        functools.partial(_stick_kernel, tq=tq, nsub=nsub, seq=s),
        grid=(b, s // step),
        in_specs=[
            pl.BlockSpec((1, step, 256), lambda i, j: (i, j, 0)),
            pl.BlockSpec((1, s, 256), lambda i, j: (i, 0, 0)),
            pl.BlockSpec((1, s, 256), lambda i, j: (i, 0, 0)),
        ],
        out_specs=pl.BlockSpec((1, step, 256), lambda i, j: (i, j, 0)),
        out_shape=jax.ShapeDtypeStruct((b, s, 256), F32),
        scratch_shapes=[
            pltpu.VMEM((nsub, N_HEADS * tq, 256), BF16),
            pltpu.VMEM((nsub, N_HEADS * tq, tq), F32),
            pltpu.VMEM((nsub, tq, 256), F32),
        ],
        compiler_params=_params("arbitrary", "arbitrary"),
        name="stick_breaking",
    )(cq, ck, cv)


def _ret_kernel(dq_ref, dk_ref, dv_ref, dmat_ref, xi_ref, zeta_ref, decay_ref, bd_ref, yd_ref, r_ref):
    n = pl.program_id(1)

    @pl.when(n == 0)
    def _():
        r_ref[...] = jnp.zeros(r_ref.shape, F32)

    c = RET_CHUNK
    lane256 = lax.broadcasted_iota(jnp.int32, (1, 256), 1)
    head_of_lane = lane256 >> 6
    r = r_ref[...]
    for u in range(dq_ref.shape[1] // c):
        rows = slice(u * c, (u + 1) * c)
        q = dq_ref[0, rows, :].astype(F32) * QK_SCALE
        k = dk_ref[0, rows, :]
        v = dv_ref[0, rows, :]
        o = _dot(q.astype(BF16), r.astype(BF16)) * xi_ref[...]
        for h in range(N_HEADS):
            qh = jnp.where(head_of_lane == h, q, 0.0).astype(BF16)
            s = _dot_t(qh, k) * dmat_ref[h]
            o = o + jnp.where(head_of_lane == h, _dot(s.astype(BF16), v), 0.0)
        mu = jnp.zeros((c, 256), F32)
        for h in range(N_HEADS):
            in_h = head_of_lane == h
            mu = mu + jnp.where(in_h, jnp.sum(jnp.where(in_h, o, 0.0), axis=-1, keepdims=True), 0.0)
        cen = o - mu * (1.0 / HEAD_DIM)
        var = jnp.zeros((c, 256), F32)
        for h in range(N_HEADS):
            in_h = head_of_lane == h
            var = var + jnp.where(in_h, jnp.sum(jnp.where(in_h, cen * cen, 0.0), axis=-1, keepdims=True), 0.0)
        yd_ref[0, rows, :] = cen * lax.rsqrt(var * (1.0 / HEAD_DIM) + LN_EPS)
        kz = (k.astype(F32) * zeta_ref[...]).astype(BF16)
        upd = lax.dot_general(kz, v, (((0,), (0,)), ((), ())), preferred_element_type=F32)
        r = r * decay_ref[...] + upd * bd_ref[...]
    r_ref[...] = r


def _retention(dq, dk, dv, consts):
    b, s, _ = dq.shape
    c = RET_CHUNK
    dmat, xi, zeta, decay, bd = consts
    step = RET_CHUNKS_PER_STEP * c
    tile = pl.BlockSpec((1, step, 256), lambda i, j: (i, j, 0))
    return pl.pallas_call(
        _ret_kernel,
        grid=(b, s // step),
        in_specs=[
            tile, tile, tile,
            pl.BlockSpec((N_HEADS, c, c), lambda i, j: (0, 0, 0)),
            pl.BlockSpec((c, 256), lambda i, j: (0, 0)),
            pl.BlockSpec((c, 256), lambda i, j: (0, 0)),
            pl.BlockSpec((256, 256), lambda i, j: (0, 0)),
            pl.BlockSpec((256, 256), lambda i, j: (0, 0)),
        ],
        out_specs=tile,
        out_shape=jax.ShapeDtypeStruct((b, s, 256), F32),
        scratch_shapes=[pltpu.VMEM((256, 256), F32)],
        compiler_params=_params("arbitrary", "arbitrary"),
        name="retention",
    )(dq, dk, dv, dmat, xi, zeta, decay, bd)


def _retention_consts():
    c = RET_CHUNK
    log_g = jnp.log(1.0 - jnp.asarray(2.0 ** (-5.0 - np.arange(N_HEADS)), dtype=F32))
    i = jnp.arange(c)
    diff = (i[:, None] - i[None, :]).astype(F32)
    dmat = jnp.where(diff >= 0, jnp.exp(log_g[:, None, None] * jnp.maximum(diff, 0.0)), 0.0)
    zeta = jnp.exp(log_g[:, None] * (c - 1 - i)[None, :].astype(F32))
    xi = jnp.exp(log_g[:, None] * (i + 1)[None, :].astype(F32))
    g_chunk = jnp.exp(log_g * c)
    lanes = lambda hc: jnp.repeat(hc.T, HEAD_DIM, axis=1)
    head = np.arange(256) // HEAD_DIM
    bd = jnp.asarray(head[:, None] == head[None, :], F32)
    decay = jnp.repeat(g_chunk, HEAD_DIM)[:, None] * jnp.ones((1, 256), F32)
    return dmat, lanes(xi), lanes(zeta), decay, bd


def _merge_kernel(x_ref, mod_ref, g_ref, ya_ref, yb_ref, yc_ref, yd_ref, z_ref,
                  wm_ref, wbr_ref, wout_ref, fg_ref, o_ref, *, d, final):
    mod = mod_ref[0]
    tm = x_ref.shape[1]
    part = tm // MERGE_ROW_PARTS
    for r in range(MERGE_ROW_PARTS):
        rows = slice(r * part, (r + 1) * part)
        xf = x_ref[0, rows, :]
        h = _modulated_norm(xf, g_ref[...], mod, d).astype(BF16)
        merged = None
        for i, y_ref in enumerate((ya_ref, yb_ref, yc_ref, yd_ref)):
            yi = (y_ref[0, rows, :] * _silu(z_ref[0, rows, 256 * i:256 * (i + 1)])).astype(BF16)
            term = _sigmoid(_dot(h, wm_ref[i])) * _dot(yi, wbr_ref[i])
            merged = term if merged is None else merged + term
        out = xf + mod[:, 2 * d:3 * d] * _dot(merged.astype(BF16), wout_ref[...])
        if final:
            out = out * lax.rsqrt(jnp.mean(out * out, axis=-1, keepdims=True) + RMS_EPS) * fg_ref[...]
        o_ref[0, rows, :] = out


def _merge(x, mod, g, ys, z, wm, wbr, wout, final_g, final, tm):
    b, s, d = x.shape
    ytile = pl.BlockSpec((1, tm, 256), lambda i, j: (i, j, 0))
    return pl.pallas_call(
        functools.partial(_merge_kernel, d=d, final=final),
        grid=(b, s // tm),
        in_specs=[
            pl.BlockSpec((1, tm, d), lambda i, j: (i, j, 0)),
            pl.BlockSpec((1, 1, 3 * d), lambda i, j: (i, 0, 0)),
            pl.BlockSpec((1, d), lambda i, j: (0, 0)),
            ytile, ytile, ytile, ytile,
            pl.BlockSpec((1, tm, 4 * 256), lambda i, j: (i, j, 0)),
            pl.BlockSpec((4, d, d), lambda i, j: (0, 0, 0)),
            pl.BlockSpec((4, 256, d), lambda i, j: (0, 0, 0)),
            pl.BlockSpec((d, d), lambda i, j: (0, 0)),
            pl.BlockSpec((1, d), lambda i, j: (0, 0)),
        ],
        out_specs=pl.BlockSpec((1, tm, d), lambda i, j: (i, j, 0)),
        out_shape=jax.ShapeDtypeStruct((b, s, d), F32),
        compiler_params=_params("arbitrary", "arbitrary"),
        name="merge_out",
    )(x, mod, g.reshape(1, d), *ys, z, wm, wbr, wout, final_g.reshape(1, d))


def _proj_columns():
    a, bb, cc, dd = 0, 908, 1676, 2700
    perm = np.concatenate([np.arange(64) + 64 * h for h in SWA_HEAD_ORDER])
    pad = lambda n: np.full((n,), -1)
    cols = np.concatenate([
        np.arange(a, a + 256),
        np.arange(a + 256, a + 384),
        np.arange(a + 384, a + 640),
        np.arange(a + 640, a + 652), pad(116),
        np.arange(a + 652, a + 908),
        bb + 512 + perm,
        np.arange(cc + 768, cc + 1024),
        np.arange(dd + 768, dd + 1024),
        bb + perm,
        np.arange(bb + 256, bb + 512),
        np.arange(cc, cc + 768),
        np.arange(dd, dd + 768),
    ])
    assert cols.shape[0] == PROJ_W
    return cols, perm


def _take_columns(w, cols):
    pieces, i = [], 0
    while i < len(cols):
        j = i + 1
        if cols[i] < 0:
            while j < len(cols) and cols[j] < 0:
                j += 1
            pieces.append(jnp.zeros(w.shape[:-1] + (j - i,), w.dtype))
        else:
            while j < len(cols) and cols[j] == cols[j - 1] + 1:
                j += 1
            pieces.append(w[..., int(cols[i]):int(cols[j - 1]) + 1])
        i = j
    return jnp.concatenate(pieces, axis=-1)


def _position_features(s):
    pos = np.arange(s)
    feat = np.zeros((s, 64), np.float32)
    feat[:, 0] = SEL_BLOCK * (pos // SEL_BLOCK)
    feat[:, 1] = pos % SEL_BLOCK
    onehot = (pos[:, None] // SEL_BLOCK == np.arange(128)[None, :]).astype(np.float32)
    return jnp.asarray(feat, BF16), jnp.asarray(onehot, BF16)


def kernel(x, c, w_ada, b_ada, norm_g, w_in, cmp_pos, cmp_w1, cmp_w2, sink, w_merge, w_br, w_out, final_g):
    b, s, d = x.shape
    depth = w_ada.shape[0]
    tm = min(ROW_TILE, s)

    cols, perm = _proj_columns()
    w_in_p = _take_columns(w_in, cols).astype(BF16)
    wm_b = w_merge.astype(BF16)
    wbr_b = w_br.at[:, 1].set(w_br[:, 1][:, perm, :]).astype(BF16)
    wout_b = w_out.astype(BF16)
    pos_l, w1bd, w2bd = _compress_weights(cmp_pos, cmp_w1, cmp_w2)
    feat, onehot = _position_features(s)
    kconst = jnp.concatenate([jnp.zeros((s, HEAD_DIM), BF16), feat, onehot], axis=-1)
    cfeat = _compress_features(s)
    ret_consts = _retention_consts()
    mods = _modulation(c, w_ada, b_ada)

    for l in range(depth):
        p = _project(x, mods[l], norm_g[l], w_in_p[l], kconst, tm)
        kc, vc = _compress(p["acmp"], pos_l[l], w1bd[l], w2bd[l], cfeat)
        y_a = _nsa(p["aq"], p["ag"], kc, vc, p["ksel"], p["akv"], p["kwin"], NSA_TQ)
        y_b = _swa(sink[l], p["bq"], p["bkv"], SWA_TQ)
        y_c = _stick(p["cq"], p["ck"], p["cv"], STICK_TQ, STICK_SUBTILES)
        y_d = _retention(p["dq"], p["dk"], p["dv"], ret_consts)
        x = _merge(x, mods[l], norm_g[l], (y_a, y_b, y_c, y_d), p["z"], wm_b[l], wbr_b[l], wout_b[l],
                   final_g, l == depth - 1, tm)
    return x
```

```python
import functools

import numpy as np
import jax
import jax.numpy as jnp
from jax import lax
from jax.experimental import pallas as pl
from jax.experimental.pallas import tpu as pltpu

F32 = jnp.float32
BF16 = jnp.bfloat16

HEAD_DIM = 64
N_HEADS = 4
BRANCH_W = N_HEADS * HEAD_DIM
CMP_BLOCK = 32
SEL_BLOCK = 64
SEL_TOPK = 16
NSA_WINDOW = 512
NSA_TK = 512
SWA_WINDOW = 128
RET_CHUNK = 128
RMS_EPS = 1e-6
LN_EPS = 1e-5
NEG_INF = -1e30
TINY = 1e-30
FORCED_SCORE = 1e4
QK_SCALE = HEAD_DIM ** -0.5
STICK_EXIT = -110.0
STICK_EAGER_TILES = 2
MAX_BOUND_GAP = 60.0
BOUND_SLACK = 1.001

ROW_TILE = 512
MERGE_ROW_PARTS = 2
NSA_TQ = 256
NSA_SUB = 128
SWA_TQ = 2048
RET_CHUNKS_PER_STEP = 16
STICK_TQ = 128
STICK_SUBTILES = 8

VMEM_LIMIT = 56 * 1024 * 1024
SLOPES = tuple(float(2.0 ** (-8.0 * (h + 1) / N_HEADS)) for h in range(N_HEADS))

PROJ_OUTS = (
    ("aq", 0, 256, BF16),
    ("acmp", 256, 128, F32),
    ("akv", 384, 256, BF16),
    ("ag", 640, 128, F32),
    ("z", 768, 1024, F32),
    ("bq", 1792, 256, BF16),
    ("bkv", 2048, 256, BF16),
    ("cq", 2304, 256, BF16),
    ("ck", 2560, 256, BF16),
    ("cv", 2816, 256, BF16),
    ("dq", 3072, 256, BF16),
    ("dk", 3328, 256, BF16),
    ("dv", 3584, 256, BF16),
)
PROJ_W = 3840
SWA_HEAD_ORDER = (0, 2, 1, 3)


def _dot(a, b):
    return jnp.dot(a, b, preferred_element_type=F32)


def _dot_t(a, b):
    return lax.dot_general(a, b, (((1,), (1,)), ((), ())), preferred_element_type=F32)


def _dot_split(a, b):
    hi = a.astype(BF16)
    lo = (a - hi.astype(F32)).astype(BF16)
    return _dot(hi, b) + _dot(lo, b)


def _sigmoid(v):
    return 1.0 / (1.0 + jnp.exp(-v))


def _silu(v):
    return v * _sigmoid(v)


def _params(*sem):
    return pltpu.CompilerParams(dimension_semantics=sem, vmem_limit_bytes=VMEM_LIMIT)


def _mod_kernel(c_ref, w_ref, b_ref, o_ref):
    cc = c_ref[...]
    o_ref[0] = _dot(_silu(cc).astype(BF16), w_ref[0].astype(BF16)) + b_ref[0]


def _modulation(c, w_ada, b_ada):
    depth, d, n = w_ada.shape
    b = c.shape[0]
    rows = 8
    cp = jnp.zeros((rows, d), F32).at[:b].set(c)
    tn = 512
    out = pl.pallas_call(
        _mod_kernel,
        grid=(depth, n // tn),
        in_specs=[
            pl.BlockSpec((rows, d), lambda l, j: (0, 0)),
            pl.BlockSpec((1, d, tn), lambda l, j: (l, 0, j)),
            pl.BlockSpec((1, 1, tn), lambda l, j: (l, 0, j)),
        ],
        out_specs=pl.BlockSpec((1, rows, tn), lambda l, j: (l, 0, j)),
        out_shape=jax.ShapeDtypeStruct((depth, rows, n), F32),
        compiler_params=_params("arbitrary", "arbitrary"),
        name="adaln_mod",
    )(cp, w_ada, b_ada.reshape(depth, 1, n))
    return out[:, :b].reshape(depth, b, 1, n)


def _modulated_norm(xf, g, mod, d):
    ms = jnp.mean(xf * xf, axis=-1, keepdims=True)
    y = xf * lax.rsqrt(ms + RMS_EPS) * g
    return y * (1.0 + mod[:, d:2 * d]) + mod[:, 0:d]


def _proj_kernel(x_ref, mod_ref, g_ref, w_ref, kconst_ref, *out_refs, d):
    h = _modulated_norm(x_ref[0], g_ref[...], mod_ref[0], d).astype(BF16)
    ksel_ref, kwin_ref = out_refs[len(PROJ_OUTS):]
    for (name, start, width, dtype), o_ref in zip(PROJ_OUTS, out_refs):
        for c0 in range(0, width, 256):
            cw = min(256, width - c0)
            acc = _dot(h, w_ref[:, start + c0:start + c0 + cw])
            o_ref[0, :, c0:c0 + cw] = acc.astype(dtype)
            if name == "akv":
                ksel_ref[0] = kconst_ref[...]
                ksel_ref[0, :, 0:HEAD_DIM] = acc[:, 0:HEAD_DIM].astype(BF16)
                kwin_ref[0] = kconst_ref[:, 0:128]
                kwin_ref[0, :, 0:HEAD_DIM] = acc[:, 128:128 + HEAD_DIM].astype(BF16)


def _project(x, mod, g, w_p, kconst, tm):
    b, s, d = x.shape
    widths = [(wd, dt) for (_, _, wd, dt) in PROJ_OUTS] + [(256, BF16), (128, BF16)]
    out_shapes = [jax.ShapeDtypeStruct((b, s, wd), dt) for wd, dt in widths]
    out_specs = [pl.BlockSpec((1, tm, wd), lambda i, j: (i, j, 0)) for wd, _ in widths]
    outs = pl.pallas_call(
        functools.partial(_proj_kernel, d=d),
        grid=(b, s // tm),
        in_specs=[
            pl.BlockSpec((1, tm, d), lambda i, j: (i, j, 0)),
            pl.BlockSpec((1, 1, 3 * d), lambda i, j: (i, 0, 0)),
            pl.BlockSpec((1, d), lambda i, j: (0, 0)),
            pl.BlockSpec((d, PROJ_W), lambda i, j: (0, 0)),
            pl.BlockSpec((tm, 256), lambda i, j: (j, 0)),
        ],
        out_specs=out_specs,
        out_shape=out_shapes,
        compiler_params=_params("arbitrary", "arbitrary"),
        name="in_proj",
    )(x, mod, g.reshape(1, d), w_p, kconst)
    names = [name for (name, _, _, _) in PROJ_OUTS] + ["ksel", "kwin"]
    return dict(zip(names, outs))


def _compress_kernel(acmp_ref, pos_ref, w1_ref, w2_ref, cfeat_ref, kc_ref, vc_ref, *, nc):
    half = nc // 2
    for parity in range(2):
        hid = jnp.zeros((half, 128), F32)
        for i in range(CMP_BLOCK):
            z = acmp_ref[0, pl.ds(parity * CMP_BLOCK + i, half, stride=2 * CMP_BLOCK), :]
            hid = hid + _dot((z + pos_ref[i:i + 1, :]).astype(BF16), w1_ref[i])
        act = _silu(hid).astype(BF16)
        out_rows = slice(parity * half, (parity + 1) * half)
        kc_ref[0, out_rows, :] = (_dot(act, w2_ref[0]) + cfeat_ref[out_rows, :]).astype(BF16)
        vc_ref[0, out_rows, :] = _dot(act, w2_ref[1]).astype(BF16)


def _compress(acmp, pos, w1bd, w2bd, cfeat):
    b, s, _ = acmp.shape
    nc = s // CMP_BLOCK
    spec_o = pl.BlockSpec((1, nc, 128), lambda i: (i, 0, 0))
    return pl.pallas_call(
        functools.partial(_compress_kernel, nc=nc),
        grid=(b,),
        in_specs=[
            pl.BlockSpec((1, s, 128), lambda i: (i, 0, 0)),
            pl.BlockSpec((CMP_BLOCK, 128), lambda i: (0, 0)),
            pl.BlockSpec((CMP_BLOCK, 128, 128), lambda i: (0, 0, 0)),
            pl.BlockSpec((2, 128, 128), lambda i: (0, 0, 0)),
            pl.BlockSpec((nc, 128), lambda i: (0, 0)),
        ],
        out_specs=[spec_o, spec_o],
        out_shape=[jax.ShapeDtypeStruct((b, nc, 128), BF16)] * 2,
        compiler_params=_params("arbitrary"),
        name="nsa_compress",
    )(acmp, pos, w1bd, w2bd, cfeat)


def _compress_features(s):
    nc = s // CMP_BLOCK
    r = np.arange(nc)
    blk = np.where(r < nc // 2, 2 * r, 2 * (r - nc // 2) + 1)
    end = blk * CMP_BLOCK + CMP_BLOCK - 1
    feat = np.zeros((nc, 128), np.float32)
    feat[:, HEAD_DIM] = SEL_BLOCK * (end // SEL_BLOCK)
    feat[:, HEAD_DIM + 1] = end % SEL_BLOCK
    return jnp.asarray(feat)


def _compress_weights(cmp_pos, cmp_w1, cmp_w2):
    depth = cmp_pos.shape[0]
    hd = HEAD_DIM
    pos = jnp.concatenate([cmp_pos[:, 0], cmp_pos[:, 1]], axis=-1)
    w1 = cmp_w1.reshape(depth, 2, CMP_BLOCK, hd, hd)
    z = jnp.zeros((depth, CMP_BLOCK, hd, hd), F32)
    w1bd = jnp.concatenate([jnp.concatenate([w1[:, 0], z], axis=-1),
                            jnp.concatenate([z, w1[:, 1]], axis=-1)], axis=-2).astype(BF16)
    z2 = jnp.zeros((depth, hd, hd), F32)
    w2k = jnp.concatenate([jnp.concatenate([cmp_w2[:, 0], z2], axis=-1),
                           jnp.concatenate([z2, z2], axis=-1)], axis=-2)
    w2v = jnp.concatenate([jnp.concatenate([z2, z2], axis=-1),
                           jnp.concatenate([z2, cmp_w2[:, 1]], axis=-1)], axis=-2)
    return pos, w1bd, jnp.stack([w2k, w2v], axis=1).astype(BF16)


def _nsa_kernel(aq_ref, ag_ref, kc_ref, vc_ref, ksel_ref, vsel_ref, kwin_ref, vwin_ref, wmask_ref, gsel_ref, rowc_ref, ya_ref,
                qaug_ref, mx_ref, sm_ref, acc_ref, owin_ref, kmax_ref, *, tq, seq):
    qi = pl.program_id(1)
    qstart = qi * tq
    nc = seq // CMP_BLOCK
    half = nc // 2
    ns = seq // SEL_BLOCK
    t_lane = lax.broadcasted_iota(jnp.int32, (1, tq), 1) + qstart
    lane128 = lax.broadcasted_iota(jnp.int32, (1, 128), 1)
    nsub = tq // NSA_SUB
    row0 = lambda u, h: (u * N_HEADS + h) * NSA_SUB
    rows = N_HEADS * tq

    @pl.when(qi == 0)
    def _():
        def chunk(c, best):
            kk = ksel_ref[0, pl.ds(pl.multiple_of(c * NSA_TK, NSA_TK), NSA_TK), 0:128].astype(F32)
            sq = jnp.sum(jnp.where(lane128 < HEAD_DIM, kk * kk, 0.0), axis=-1, keepdims=True)
            return jnp.maximum(best, jnp.max(sq, axis=0, keepdims=True))
        best = lax.fori_loop(0, seq // NSA_TK, chunk, jnp.zeros((1, 1), F32))
        kmax_ref[0] = jnp.sqrt(best)[0, 0]

    row_c = lax.broadcasted_iota(jnp.int32, (nc, 1), 0)
    blk_c = jnp.where(row_c < half, 2 * row_c, 2 * (row_c - half) + 1)
    mask_c = t_lane >= blk_c * CMP_BLOCK + (CMP_BLOCK - 1)
    kc = kc_ref[0]
    vc = vc_ref[0]
    psum = jnp.zeros((nc, tq), F32)
    ocmp = []
    for h in range(N_HEADS):
        grp = aq_ref[0, :, 128 * (h // 2):128 * (h // 2) + 128].astype(F32)
        if h % 2 == 1:
            grp = pltpu.roll(grp, 64, 1)
        left = jnp.where(lane128 < HEAD_DIM, grp * QK_SCALE,
                         jnp.where(lane128 < HEAD_DIM + 2, SLOPES[h], 0.0)).astype(BF16)
        for u in range(nsub):
            qaug_ref[row0(u, h):row0(u, h) + NSA_SUB, 0:128] = left[u * NSA_SUB:(u + 1) * NSA_SUB]
        s = _dot_t(kc, left)
        s = jnp.where(mask_c, s, NEG_INF)
        m = jnp.max(s, axis=0, keepdims=True)
        e = jnp.where(mask_c, jnp.exp(s - m), 0.0)
        den = jnp.sum(e, axis=0, keepdims=True)
        p = e * (1.0 / jnp.maximum(den, TINY))
        psum = psum + p
        ocmp.append(lax.dot_general(p.astype(BF16), vc, (((0,), (0,)), ((), ())),
                                    preferred_element_type=F32))

    everything = slice(0, rows)

    def fold(sc, at=everything):
        width = sc.shape[1]
        if width < 128:
            mx_ref[at, 0:width] = jnp.maximum(mx_ref[at, 0:width], sc)
            return
        m = mx_ref[at, :]
        for g in range(width // 128):
            m = jnp.maximum(m, sc[:, 128 * g:128 * (g + 1)])
        mx_ref[at, :] = m

    def accumulate(sc, v, at=everything):
        width = sc.shape[1]
        if width < 128:
            e = jnp.exp(sc - mx_ref[at, 0:width])
            sm_ref[at, 0:width] += e
        else:
            m = mx_ref[at, :]
            e = jnp.exp(sc - jnp.concatenate([m] * (width // 128), axis=1))
            part = sm_ref[at, :]
            for g in range(width // 128):
                part = part + e[:, 128 * g:128 * (g + 1)]
            sm_ref[at, :] = part
        acc_ref[at, :] += _dot(e.astype(BF16), v)

    own0 = pl.multiple_of(qstart, tq)

    def own_keys(use):
        for u in range(nsub):
            at = slice(row0(u, 0), row0(u, 0) + N_HEADS * NSA_SUB)
            n = (u + 1) * NSA_SUB
            sc = _dot_t(qaug_ref[at, 0:128], ksel_ref[0, pl.ds(own0, n), 0:128])
            last = sc[:, u * NSA_SUB:] + wmask_ref[1]
            sc = jnp.concatenate([sc[:, 0:u * NSA_SUB], last], axis=1) if u else last
            use(sc, vsel_ref[0, pl.ds(own0, n), :], at)

    def reset_sums():
        sm_ref[...] = jnp.zeros(sm_ref.shape, F32)
        acc_ref[...] = jnp.zeros(acc_ref.shape, F32)

    qpart = qaug_ref[:, 0:128].astype(F32)
    qnorm = jnp.sqrt(jnp.sum(jnp.where(lane128 < HEAD_DIM, qpart * qpart, 0.0), axis=-1, keepdims=True))
    reach = qnorm * (kmax_ref[0] * BOUND_SLACK)
    bound_ok = jnp.max(reach, axis=0, keepdims=True)[0, 0] * 2.0 <= MAX_BOUND_GAP
    mx_ref[...] = reach + (rowc_ref[1] + rowc_ref[0] * qstart.astype(F32))
    reset_sums()
    own_keys(accumulate)

    n_win = NSA_WINDOW + NSA_SUB
    full_chunks = NSA_WINDOW // NSA_SUB
    for u in range(nsub):
        q0 = qstart + u * NSA_SUB
        w0 = pl.multiple_of(jnp.clip(q0 - NSA_WINDOW, 0, seq - n_win), NSA_SUB)
        sub_rows = slice(row0(u, 0), row0(u, 0) + N_HEADS * NSA_SUB)
        sw = _dot_t(qaug_ref[sub_rows, 0:128], kwin_ref[0, pl.ds(w0, n_win), :])
        diag_chunk = (q0 - w0) >> (NSA_SUB.bit_length() - 1)
        pieces = []
        for c in range(n_win // NSA_SUB):
            d = diag_chunk - c
            pattern = jnp.where(d == 0, 1, jnp.where((d > 0) & (d < full_chunks), 2,
                                                     jnp.where(d == full_chunks, 3, 0)))
            pieces.append(sw[:, c * NSA_SUB:(c + 1) * NSA_SUB] + wmask_ref[pattern])
        sw = jnp.concatenate(pieces, axis=1)
        ew = jnp.exp(sw - jnp.max(sw, axis=-1, keepdims=True))
        owin_ref[sub_rows, :] = (_dot(ew.astype(BF16), vwin_ref[0, pl.ds(w0, n_win), :])
                                 * (1.0 / jnp.sum(ew, axis=-1, keepdims=True)))

    imp = psum[:half] + psum[half:]
    blk = lax.broadcasted_iota(jnp.int32, (ns, 1), 0)
    cur = t_lane >> 6
    future = blk * SEL_BLOCK > t_lane
    forced = (blk == 0) | (blk == cur) | (blk == cur - 1)
    score = jnp.where(forced, FORCED_SCORE, jnp.where(future, -1.0, imp))
    blk_f = blk.astype(F32)
    for _ in range(min(SEL_TOPK, ns)):
        mx = jnp.max(score, axis=0, keepdims=True)
        first = jnp.min(jnp.where(score == mx, blk_f, float(ns)), axis=0, keepdims=True)
        score = jnp.where(blk_f == first, -jnp.inf, score)
    sel_f = jnp.transpose(jnp.where(score == -jnp.inf, 1.0, 0.0))
    negmask = jnp.where(sel_f > 0.5, 0.0, NEG_INF).astype(BF16)
    if ns < 128:
        negmask = jnp.concatenate([negmask, jnp.zeros((tq, 128 - ns), BF16)], axis=1)
    for u in range(nsub):
        for h in range(N_HEADS):
            qaug_ref[row0(u, h):row0(u, h) + NSA_SUB, 128:256] = negmask[u * NSA_SUB:(u + 1) * NSA_SUB]

    if ns < 128:
        sel_f = jnp.concatenate([sel_f, jnp.zeros((tq, 128 - ns), F32)], axis=1)
    any_blk = jnp.max(sel_f, axis=0, keepdims=True)
    blocks_per_tile = NSA_TK // SEL_BLOCK
    beyond_first = jnp.max(jnp.where((lane128 >= 1) & (lane128 < blocks_per_tile), any_blk, 0.0),
                           axis=1, keepdims=True)[0, 0] > 0.5
    any_blk = jnp.where(lane128 == 0, 0.0, any_blk)
    shift = 1
    while shift < blocks_per_tile:
        any_blk = jnp.maximum(any_blk, pltpu.roll(any_blk, 128 - shift, 1))
        shift *= 2
    tile_of_lane = lane128 >> (blocks_per_tile.bit_length() - 1)
    first_of_tile = (lane128 & (blocks_per_tile - 1)) == 0
    pow2 = lax.bitcast_convert_type(((tile_of_lane & 15) + 127) << 23, F32)
    weighted = jnp.where(first_of_tile, any_blk * pow2, 0.0)
    bits_lo = jnp.sum(jnp.where(tile_of_lane < 16, weighted, 0.0), axis=1, keepdims=True).astype(jnp.int32)[0, 0]
    bits_hi = jnp.sum(jnp.where(tile_of_lane >= 16, weighted, 0.0), axis=1, keepdims=True).astype(jnp.int32)[0, 0]

    def tile_selected(t):
        return ((jnp.where(t < 16, bits_lo, bits_hi) >> (t & 15)) & 1) == 1

    pieces_per_tile = NSA_TK // tq
    t_own = qi // pieces_per_tile
    n_pieces = qi - t_own * pieces_per_tile

    def sweep(use):
        def piece(p, carry):
            k0 = pl.multiple_of(t_own * NSA_TK + p * tq, tq)
            use(_dot_t(qaug_ref[...], ksel_ref[0, pl.ds(k0, tq), :]), k0, tq)
            return carry

        def tile(t, carry):
            def visit():
                k0 = pl.multiple_of(t * NSA_TK, NSA_TK)
                use(_dot_t(qaug_ref[...], ksel_ref[0, pl.ds(k0, NSA_TK), :]), k0, NSA_TK)
            pl.when(tile_selected(t))(visit)
            return carry

        def first_block():
            use(_dot_t(qaug_ref[:, 0:128], ksel_ref[0, 0:SEL_BLOCK, 0:128]), 0, SEL_BLOCK)

        lax.fori_loop(0, n_pieces, piece, 0)
        lax.fori_loop(0, t_own, tile, 0)
        pl.when((t_own >= 1) & jnp.logical_not(beyond_first))(first_block)

    @pl.when(jnp.logical_not(bound_ok))
    def _():
        mx_ref[...] = jnp.full(mx_ref.shape, NEG_INF, F32)
        sweep(lambda sc, k0, n: fold(sc))
        own_keys(lambda sc, v, at: fold(sc, at))
        mx_ref[...] = jnp.broadcast_to(jnp.max(mx_ref[...], axis=-1, keepdims=True), mx_ref.shape)
        reset_sums()
        own_keys(accumulate)

    sweep(lambda sc, k0, n: accumulate(sc, vsel_ref[0, pl.ds(k0, n), :]))
    osel = acc_ref[...] * (1.0 / jnp.sum(sm_ref[...], axis=-1, keepdims=True))

    gate = _sigmoid(ag_ref[0])
    gate_hi = gate.astype(BF16)
    gate_lo = (gate - gate_hi.astype(F32)).astype(BF16)
    for u in range(nsub):
        q_rows = slice(u * NSA_SUB, (u + 1) * NSA_SUB)
        spread = _dot(jnp.concatenate([gate_hi[q_rows, :], gate_lo[q_rows, :]], axis=1), gsel_ref[...])
        comb = []
        for h in range(N_HEADS):
            head_rows = slice(row0(u, h), row0(u, h) + NSA_SUB)
            g_cmp, g_sel, g_win = (spread[:, 128 * (3 * h + br):128 * (3 * h + br + 1)] for br in range(3))
            comb.append(g_cmp * ocmp[h][q_rows, :] + g_sel * osel[head_rows, :]
                        + g_win * owin_ref[head_rows, :])
        for g in range(2):
            ya_ref[0, q_rows, 128 * g:128 * (g + 1)] = jnp.where(
                lane128 < HEAD_DIM, pltpu.roll(comb[2 * g], 64, 1), comb[2 * g + 1])


def _nsa(aq, ag, kc, vc, ksel, akv, kwin, tq):
    b, s, _ = aq.shape
    nc = kc.shape[1]
    rows = N_HEADS * tq
    whole = lambda w, blk: pl.BlockSpec((1, s, w), lambda i, j, blk=blk: (i, 0, blk))
    return pl.pallas_call(
        functools.partial(_nsa_kernel, tq=tq, seq=s),
        grid=(b, s // tq),
        in_specs=[
            pl.BlockSpec((1, tq, 256), lambda i, j: (i, j, 0)),
            pl.BlockSpec((1, tq, 128), lambda i, j: (i, j, 0)),
            pl.BlockSpec((1, nc, 128), lambda i, j: (i, 0, 0)),
            pl.BlockSpec((1, nc, 128), lambda i, j: (i, 0, 0)),
            whole(256, 0),
            whole(128, 0),
            whole(128, 0),
            whole(128, 1),
            pl.BlockSpec((4, N_HEADS * NSA_SUB, NSA_SUB), lambda i, j: (0, 0, 0)),
            pl.BlockSpec((256, 3 * N_HEADS * 128), lambda i, j: (0, 0)),
            pl.BlockSpec((2, rows, 128), lambda i, j: (0, 0, 0)),
        ],
        out_specs=pl.BlockSpec((1, tq, 256), lambda i, j: (i, j, 0)),
        out_shape=jax.ShapeDtypeStruct((b, s, 256), F32),
        scratch_shapes=[
            pltpu.VMEM((rows, 256), BF16),
            pltpu.VMEM((rows, 128), F32),
            pltpu.VMEM((rows, 128), F32),
            pltpu.VMEM((rows, 128), F32),
            pltpu.VMEM((rows, 128), F32),
            pltpu.SMEM((1,), F32),
        ],
        compiler_params=_params("arbitrary", "arbitrary"),
        name="nsa_attention",
    )(aq, ag, kc, vc, ksel, akv, kwin, akv, _window_mask_table(), _gate_selector(), _nsa_row_tables(tq))


def _nsa_row_tables(tq):
    row = np.arange(N_HEADS * tq)
    u, h, r = row // (N_HEADS * NSA_SUB), (row // NSA_SUB) % N_HEADS, row % NSA_SUB
    slope = np.asarray(SLOPES, np.float32)[h]
    tab = np.stack([slope, slope * (u * NSA_SUB + r).astype(np.float32)])
    return jnp.asarray(np.repeat(tab[:, :, None], 128, axis=2), F32)


def _gate_selector():
    k = np.arange(256)[:, None] % 128
    col = np.arange(3 * N_HEADS * 128)[None, :] // 128
    return jnp.asarray(k == col, BF16)


def _window_mask_table():
    r = np.arange(N_HEADS * NSA_SUB)[:, None] % NSA_SUB
    j = np.arange(NSA_SUB)[None, :]
    keep = np.stack([np.zeros_like(j <= r), j <= r, np.ones_like(j <= r), j > r])
    return jnp.asarray(np.where(keep, 0.0, NEG_INF), F32)


def _swa_kernel(sink_ref, bq_ref, bkv_ref, bias_ref, yb_ref, *, tq, seq):
    qi = pl.program_id(1)
    sub = SWA_WINDOW
    nk = 2 * sub
    lane128 = lax.broadcasted_iota(jnp.int32, (1, 128), 1)
    for u in range(tq // sub):
        qstart = qi * tq + u * sub
        k0 = pl.multiple_of(jnp.maximum(qstart - sub, 0), sub)
        at_start = jnp.where(qstart == 0, 1, 0)
        kk = bkv_ref[0, pl.ds(k0, nk), 0:128]
        vv = bkv_ref[0, pl.ds(k0, nk), 128:256]
        for g in range(2):
            qg = bq_ref[0, u * sub:(u + 1) * sub, 128 * g:128 * (g + 1)].astype(F32) * QK_SCALE
            outs = []
            for p in range(2):
                h = SWA_HEAD_ORDER[2 * g + p]
                in_half = (lane128 >= HEAD_DIM * p) & (lane128 < HEAD_DIM * (p + 1))
                qh = jnp.where(in_half, qg, 0.0).astype(BF16)
                s = _dot_t(kk, qh) + bias_ref[h, at_start]
                sink = sink_ref[h]
                m = jnp.maximum(jnp.max(s, axis=0, keepdims=True), sink)
                e = jnp.exp(s - m)
                den = jnp.sum(e, axis=0, keepdims=True) + jnp.exp(sink - m)
                pr = e * (1.0 / jnp.maximum(den, TINY))
                outs.append(lax.dot_general(pr.astype(BF16), vv, (((0,), (0,)), ((), ())),
                                            preferred_element_type=F32))
            yb_ref[0, u * sub:(u + 1) * sub, 128 * g:128 * (g + 1)] = jnp.where(
                lane128 < HEAD_DIM, outs[0], outs[1])


def _swa_bias_table():
    sub = SWA_WINDOW
    j = np.arange(2 * sub)[:, None]
    r = np.arange(sub)[None, :]
    rel = np.stack([r + sub - j, r - j])
    ok = (rel >= 0) & (rel < sub)
    slopes = np.asarray(SLOPES, np.float32)[:, None, None, None]
    return jnp.asarray(np.where(ok[None], -slopes * rel[None].astype(np.float32), NEG_INF), F32)


def _swa(sink, bq, bkv, tq):
    b, s, _ = bq.shape
    return pl.pallas_call(
        functools.partial(_swa_kernel, tq=tq, seq=s),
        grid=(b, s // tq),
        in_specs=[
            pl.BlockSpec(memory_space=pltpu.SMEM),
            pl.BlockSpec((1, tq, 256), lambda i, j: (i, j, 0)),
            pl.BlockSpec((1, s, 256), lambda i, j: (i, 0, 0)),
            pl.BlockSpec((N_HEADS, 2, 2 * SWA_WINDOW, SWA_WINDOW), lambda i, j: (0, 0, 0, 0)),
        ],
        out_specs=pl.BlockSpec((1, tq, 256), lambda i, j: (i, j, 0)),
        out_shape=jax.ShapeDtypeStruct((b, s, 256), F32),
        compiler_params=_params("arbitrary", "arbitrary"),
        name="swa_attention",
    )(sink, bq, bkv, _swa_bias_table())


def _stick_kernel(cq_ref, ck_ref, cv_ref, yc_ref, qm_ref, carry_ref, o_ref, *, tq, nsub, seq):
    lane256 = lax.broadcasted_iota(jnp.int32, (1, 256), 1)
    head_of_lane = lane256 >> 6
    rows = N_HEADS * tq
    jj = lax.broadcasted_iota(jnp.int32, (tq, 2 * tq), 0)
    ss = lax.broadcasted_iota(jnp.int32, (tq, 2 * tq), 1)
    upper = ((jj > ss) | (ss >= tq)).astype(BF16)
    upper2 = jnp.concatenate([upper, upper], axis=0)
    t_in = lax.broadcasted_iota(jnp.int32, (rows, tq), 0) & (tq - 1)
    strict = lax.broadcasted_iota(jnp.int32, (rows, tq), 1) < t_in

    def tile(u, k0, carry, acc, valid):
        diagonal = carry is None
        kt = ck_ref[0, pl.ds(k0, tq), :]
        vt = cv_ref[0, pl.ds(k0, tq), :]
        z = _dot_t(qm_ref[u], kt)
        soft = jnp.log(1.0 + jnp.exp(-jnp.abs(z)))
        log_beta = jnp.minimum(z, 0.0) - soft
        log_1m = log_beta - z
        if diagonal:
            log_1m = jnp.where(strict, log_1m, 0.0)
        hi = log_1m.astype(BF16)
        lo = (log_1m - hi.astype(F32)).astype(BF16)
        sums = _dot(jnp.concatenate([hi, lo], axis=1), upper2)
        suffix = sums[:, :tq] if diagonal else sums[:, :tq] + carry
        a = jnp.exp(log_beta + suffix)
        if diagonal:
            a = jnp.where(strict, a, 0.0)
        if valid is not None:
            a = jnp.where(valid, a, 0.0)
        a_b = a.astype(BF16)
        a_cat = jnp.concatenate([a_b[h * tq:(h + 1) * tq, :] for h in range(N_HEADS)], axis=1)
        v_bd = jnp.concatenate([jnp.where(head_of_lane == h, vt, jnp.zeros_like(vt))
                                for h in range(N_HEADS)], axis=0)
        acc = acc + _dot(a_cat, v_bd)
        return (sums[:, tq:] if diagonal else carry + sums[:, tq:]), acc

    worst = []
    for u in range(nsub):
        qi = pl.program_id(1) * nsub + u
        q = cq_ref[0, u * tq:(u + 1) * tq, :].astype(F32) * QK_SCALE
        for h in range(N_HEADS):
            qm_ref[u, h * tq:(h + 1) * tq, :] = jnp.where(head_of_lane == h, q, 0.0).astype(BF16)
        carry, acc = tile(u, pl.multiple_of(qi * tq, tq), None, jnp.zeros((tq, 256), F32), None)
        for d in range(1, STICK_EAGER_TILES + 1):
            j = qi - d
            carry, acc = tile(u, pl.multiple_of(jnp.maximum(j, 0) * tq, tq), carry, acc, j >= 0)
        carry_ref[u] = carry
        o_ref[u] = acc
        worst.append(jnp.max(carry, axis=0, keepdims=True)[0, 0])

    for u in range(nsub):
        qi = pl.program_id(1) * nsub + u

        def cond(state):
            j, worst_carry = state
            return (j >= 0) & (worst_carry > STICK_EXIT)

        def body(state, u=u):
            j, _ = state
            new_carry, new_acc = tile(u, pl.multiple_of(j * tq, tq), carry_ref[u], o_ref[u], None)
            carry_ref[u] = new_carry
            o_ref[u] = new_acc
            return j - 1, jnp.max(new_carry, axis=0, keepdims=True)[0, 0]

        lax.while_loop(cond, body, (qi - 1 - STICK_EAGER_TILES, worst[u]))
        yc_ref[0, u * tq:(u + 1) * tq, :] = o_ref[u]


def _stick(cq, ck, cv, tq, nsub):
    b, s, _ = cq.shape
    step = tq * nsub
    return pl.pallas_call(
        functools.partial(_stick_kernel, tq=tq, nsub=nsub, seq=s),
        grid=(b, s // step),
        in_specs=[
            pl.BlockSpec((1, step, 256), lambda i, j: (i, j, 0)),
            pl.BlockSpec((1, s, 256), lambda i, j: (i, 0, 0)),
            pl.BlockSpec((1, s, 256), lambda i, j: (i, 0, 0)),
        ],
        out_specs=pl.BlockSpec((1, step, 256), lambda i, j: (i, j, 0)),
        out_shape=jax.ShapeDtypeStruct((b, s, 256), F32),
        scratch_shapes=[
            pltpu.VMEM((nsub, N_HEADS * tq, 256), BF16),
            pltpu.VMEM((nsub, N_HEADS * tq, tq), F32),
            pltpu.VMEM((nsub, tq, 256), F32),
        ],
        compiler_params=_params("arbitrary", "arbitrary"),
        name="stick_breaking",
    )(cq, ck, cv)


def _ret_kernel(dq_ref, dk_ref, dv_ref, dmat_ref, xi_ref, zeta_ref, decay_ref, bd_ref, yd_ref, r_ref):
    n = pl.program_id(1)

    @pl.when(n == 0)
    def _():
        r_ref[...] = jnp.zeros(r_ref.shape, F32)

    c = RET_CHUNK
    lane256 = lax.broadcasted_iota(jnp.int32, (1, 256), 1)
    head_of_lane = lane256 >> 6
    r = r_ref[...]
    for u in range(dq_ref.shape[1] // c):
        rows = slice(u * c, (u + 1) * c)
        q = dq_ref[0, rows, :].astype(F32) * QK_SCALE
        k = dk_ref[0, rows, :]
        v = dv_ref[0, rows, :]
        o = _dot(q.astype(BF16), r.astype(BF16)) * xi_ref[...]
        for h in range(N_HEADS):
            qh = jnp.where(head_of_lane == h, q, 0.0).astype(BF16)
            s = _dot_t(qh, k) * dmat_ref[h]
            o = o + jnp.where(head_of_lane == h, _dot(s.astype(BF16), v), 0.0)
        mu = jnp.zeros((c, 256), F32)
        for h in range(N_HEADS):
            in_h = head_of_lane == h
            mu = mu + jnp.where(in_h, jnp.sum(jnp.where(in_h, o, 0.0), axis=-1, keepdims=True), 0.0)
        cen = o - mu * (1.0 / HEAD_DIM)
        var = jnp.zeros((c, 256), F32)
        for h in range(N_HEADS):
            in_h = head_of_lane == h
            var = var + jnp.where(in_h, jnp.sum(jnp.where(in_h, cen * cen, 0.0), axis=-1, keepdims=True), 0.0)
        yd_ref[0, rows, :] = cen * lax.rsqrt(var * (1.0 / HEAD_DIM) + LN_EPS)
        kz = (k.astype(F32) * zeta_ref[...]).astype(BF16)
        upd = lax.dot_general(kz, v, (((0,), (0,)), ((), ())), preferred_element_type=F32)
        r = r * decay_ref[...] + upd * bd_ref[...]
    r_ref[...] = r


def _retention(dq, dk, dv, consts):
    b, s, _ = dq.shape
    c = RET_CHUNK
    dmat, xi, zeta, decay, bd = consts
    step = RET_CHUNKS_PER_STEP * c
    tile = pl.BlockSpec((1, step, 256), lambda i, j: (i, j, 0))
    return pl.pallas_call(
        _ret_kernel,
        grid=(b, s // step),
        in_specs=[
            tile, tile, tile,
            pl.BlockSpec((N_HEADS, c, c), lambda i, j: (0, 0, 0)),
            pl.BlockSpec((c, 256), lambda i, j: (0, 0)),
            pl.BlockSpec((c, 256), lambda i, j: (0, 0)),
            pl.BlockSpec((256, 256), lambda i, j: (0, 0)),
            pl.BlockSpec((256, 256), lambda i, j: (0, 0)),
        ],
        out_specs=tile,
        out_shape=jax.ShapeDtypeStruct((b, s, 256), F32),
        scratch_shapes=[pltpu.VMEM((256, 256), F32)],
        compiler_params=_params("arbitrary", "arbitrary"),
        name="retention",
    )(dq, dk, dv, dmat, xi, zeta, decay, bd)


def _retention_consts():
    c = RET_CHUNK
    log_g = jnp.log(1.0 - jnp.asarray(2.0 ** (-5.0 - np.arange(N_HEADS)), dtype=F32))
    i = jnp.arange(c)
    diff = (i[:, None] - i[None, :]).astype(F32)
    dmat = jnp.where(diff >= 0, jnp.exp(log_g[:, None, None] * jnp.maximum(diff, 0.0)), 0.0)
    zeta = jnp.exp(log_g[:, None] * (c - 1 - i)[None, :].astype(F32))
    xi = jnp.exp(log_g[:, None] * (i + 1)[None, :].astype(F32))
    g_chunk = jnp.exp(log_g * c)
    lanes = lambda hc: jnp.repeat(hc.T, HEAD_DIM, axis=1)
    head = np.arange(256) // HEAD_DIM
    bd = jnp.asarray(head[:, None] == head[None, :], F32)
    decay = jnp.repeat(g_chunk, HEAD_DIM)[:, None] * jnp.ones((1, 256), F32)
    return dmat, lanes(xi), lanes(zeta), decay, bd


def _merge_kernel(x_ref, mod_ref, g_ref, ya_ref, yb_ref, yc_ref, yd_ref, z_ref,
                  wm_ref, wbr_ref, wout_ref, fg_ref, o_ref, *, d, final):
    mod = mod_ref[0]
    tm = x_ref.shape[1]
    part = tm // MERGE_ROW_PARTS
    for r in range(MERGE_ROW_PARTS):
        rows = slice(r * part, (r + 1) * part)
        xf = x_ref[0, rows, :]
        h = _modulated_norm(xf, g_ref[...], mod, d).astype(BF16)
        merged = None
        for i, y_ref in enumerate((ya_ref, yb_ref, yc_ref, yd_ref)):
            yi = (y_ref[0, rows, :] * _silu(z_ref[0, rows, 256 * i:256 * (i + 1)])).astype(BF16)
            term = _sigmoid(_dot(h, wm_ref[i])) * _dot(yi, wbr_ref[i])
            merged = term if merged is None else merged + term
        out = xf + mod[:, 2 * d:3 * d] * _dot(merged.astype(BF16), wout_ref[...])
        if final:
            out = out * lax.rsqrt(jnp.mean(out * out, axis=-1, keepdims=True) + RMS_EPS) * fg_ref[...]
        o_ref[0, rows, :] = out


def _merge(x, mod, g, ys, z, wm, wbr, wout, final_g, final, tm):
    b, s, d = x.shape
    ytile = pl.BlockSpec((1, tm, 256), lambda i, j: (i, j, 0))
    return pl.pallas_call(
        functools.partial(_merge_kernel, d=d, final=final),
        grid=(b, s // tm),
        in_specs=[
            pl.BlockSpec((1, tm, d), lambda i, j: (i, j, 0)),
            pl.BlockSpec((1, 1, 3 * d), lambda i, j: (i, 0, 0)),
            pl.BlockSpec((1, d), lambda i, j: (0, 0)),
            ytile, ytile, ytile, ytile,
            pl.BlockSpec((1, tm, 4 * 256), lambda i, j: (i, j, 0)),
            pl.BlockSpec((4, d, d), lambda i, j: (0, 0, 0)),
            pl.BlockSpec((4, 256, d), lambda i, j: (0, 0, 0)),
            pl.BlockSpec((d, d), lambda i, j: (0, 0)),
            pl.BlockSpec((1, d), lambda i, j: (0, 0)),
        ],
        out_specs=pl.BlockSpec((1, tm, d), lambda i, j: (i, j, 0)),
        out_shape=jax.ShapeDtypeStruct((b, s, d), F32),
        compiler_params=_params("arbitrary", "arbitrary"),
        name="merge_out",
    )(x, mod, g.reshape(1, d), *ys, z, wm, wbr, wout, final_g.reshape(1, d))


def _proj_columns():
    a, bb, cc, dd = 0, 908, 1676, 2700
    perm = np.concatenate([np.arange(64) + 64 * h for h in SWA_HEAD_ORDER])
    pad = lambda n: np.full((n,), -1)
    cols = np.concatenate([
        np.arange(a, a + 256),
        np.arange(a + 256, a + 384),
        np.arange(a + 384, a + 640),
        np.arange(a + 640, a + 652), pad(116),
        np.arange(a + 652, a + 908),
        bb + 512 + perm,
        np.arange(cc + 768, cc + 1024),
        np.arange(dd + 768, dd + 1024),
        bb + perm,
        np.arange(bb + 256, bb + 512),
        np.arange(cc, cc + 768),
        np.arange(dd, dd + 768),
    ])
    assert cols.shape[0] == PROJ_W
    return cols, perm


def _take_columns(w, cols):
    pieces, i = [], 0
    while i < len(cols):
        j = i + 1
        if cols[i] < 0:
            while j < len(cols) and cols[j] < 0:
                j += 1
            pieces.append(jnp.zeros(w.shape[:-1] + (j - i,), w.dtype))
        else:
            while j < len(cols) and cols[j] == cols[j - 1] + 1:
                j += 1
            pieces.append(w[..., int(cols[i]):int(cols[j - 1]) + 1])
        i = j
    return jnp.concatenate(pieces, axis=-1)


def _position_features(s):
    pos = np.arange(s)
    feat = np.zeros((s, 64), np.float32)
    feat[:, 0] = SEL_BLOCK * (pos // SEL_BLOCK)
    feat[:, 1] = pos % SEL_BLOCK
    onehot = (pos[:, None] // SEL_BLOCK == np.arange(128)[None, :]).astype(np.float32)
    return jnp.asarray(feat, BF16), jnp.asarray(onehot, BF16)


def kernel(x, c, w_ada, b_ada, norm_g, w_in, cmp_pos, cmp_w1, cmp_w2, sink, w_merge, w_br, w_out, final_g):
    b, s, d = x.shape
    depth = w_ada.shape[0]
    tm = min(ROW_TILE, s)

    cols, perm = _proj_columns()
    w_in_p = _take_columns(w_in.astype(BF16), cols)
    wm_b = w_merge.astype(BF16)
    wbr_b = jnp.stack([w_br[:, 0], w_br[:, 1][:, perm, :], w_br[:, 2], w_br[:, 3]], axis=1).astype(BF16)
    wout_b = w_out.astype(BF16)
    pos_l, w1bd, w2bd = _compress_weights(cmp_pos, cmp_w1, cmp_w2)
    feat, onehot = _position_features(s)
    kconst = jnp.concatenate([jnp.zeros((s, HEAD_DIM), BF16), feat, onehot], axis=-1)
    cfeat = _compress_features(s)
    ret_consts = _retention_consts()
    mods = _modulation(c, w_ada, b_ada)

    for l in range(depth):
        p = _project(x, mods[l], norm_g[l], w_in_p[l], kconst, tm)
        kc, vc = _compress(p["acmp"], pos_l[l], w1bd[l], w2bd[l], cfeat)
        y_a = _nsa(p["aq"], p["ag"], kc, vc, p["ksel"], p["akv"], p["kwin"], NSA_TQ)
        y_b = _swa(sink[l], p["bq"], p["bkv"], SWA_TQ)
        y_c = _stick(p["cq"], p["ck"], p["cv"], STICK_TQ, STICK_SUBTILES)
        y_d = _retention(p["dq"], p["dk"], p["dv"], ret_consts)
        x = _merge(x, mods[l], norm_g[l], (y_a, y_b, y_c, y_d), p["z"], wm_b[l], wbr_b[l], wout_b[l],
                   final_g, l == depth - 1, tm)
    return x
```

```python
import functools

import numpy as np
import jax
import jax.numpy as jnp
from jax import lax
from jax.experimental import pallas as pl
from jax.experimental.pallas import tpu as pltpu

F32 = jnp.float32
BF16 = jnp.bfloat16

HEAD_DIM = 64
N_HEADS = 4
BRANCH_W = N_HEADS * HEAD_DIM
CMP_BLOCK = 32
SEL_BLOCK = 64
SEL_TOPK = 16
NSA_WINDOW = 512
NSA_TK = 512
SWA_WINDOW = 128
RET_CHUNK = 128
RMS_EPS = 1e-6
LN_EPS = 1e-5
NEG_INF = -1e30
TINY = 1e-30
FORCED_SCORE = 1e4
QK_SCALE = HEAD_DIM ** -0.5
STICK_EXIT = -110.0
STICK_EAGER_TILES = 2
MAX_BOUND_GAP = 60.0
BOUND_SLACK = 1.001

ROW_TILE = 512
MERGE_ROW_PARTS = 2
NSA_TQ = 256
NSA_SUB = 128
SWA_TQ = 2048
RET_CHUNKS_PER_STEP = 16
STICK_TQ = 128
STICK_SUBTILES = 8

VMEM_LIMIT = 56 * 1024 * 1024
SLOPES = tuple(float(2.0 ** (-8.0 * (h + 1) / N_HEADS)) for h in range(N_HEADS))

PROJ_OUTS = (
    ("aq", 0, 256, BF16),
    ("acmp", 256, 128, F32),
    ("akv", 384, 256, BF16),
    ("ag", 640, 128, F32),
    ("z", 768, 1024, F32),
    ("bq", 1792, 256, BF16),
    ("bkv", 2048, 256, BF16),
    ("cq", 2304, 256, BF16),
    ("ck", 2560, 256, BF16),
    ("cv", 2816, 256, BF16),
    ("dq", 3072, 256, BF16),
    ("dk", 3328, 256, BF16),
    ("dv", 3584, 256, BF16),
)
PROJ_W = 3840
SWA_HEAD_ORDER = (0, 2, 1, 3)


def _dot(a, b):
    return jnp.dot(a, b, preferred_element_type=F32)


def _dot_t(a, b):
    return lax.dot_general(a, b, (((1,), (1,)), ((), ())), preferred_element_type=F32)


def _dot_split(a, b):
    hi = a.astype(BF16)
    lo = (a - hi.astype(F32)).astype(BF16)
    return _dot(hi, b) + _dot(lo, b)


def _sigmoid(v):
    return 1.0 / (1.0 + jnp.exp(-v))


def _silu(v):
    return v * _sigmoid(v)


def _params(*sem):
    return pltpu.CompilerParams(dimension_semantics=sem, vmem_limit_bytes=VMEM_LIMIT)


def _mod_kernel(c_ref, w_ref, b_ref, o_ref):
    cc = c_ref[...]
    o_ref[0] = _dot(_silu(cc).astype(BF16), w_ref[0].astype(BF16)) + b_ref[0]


def _modulation(c, w_ada, b_ada):
    depth, d, n = w_ada.shape
    b = c.shape[0]
    rows = 8
    cp = jnp.zeros((rows, d), F32).at[:b].set(c)
    tn = 512
    out = pl.pallas_call(
        _mod_kernel,
        grid=(depth, n // tn),
        in_specs=[
            pl.BlockSpec((rows, d), lambda l, j: (0, 0)),
            pl.BlockSpec((1, d, tn), lambda l, j: (l, 0, j)),
            pl.BlockSpec((1, 1, tn), lambda l, j: (l, 0, j)),
        ],
        out_specs=pl.BlockSpec((1, rows, tn), lambda l, j: (l, 0, j)),
        out_shape=jax.ShapeDtypeStruct((depth, rows, n), F32),
        compiler_params=_params("arbitrary", "arbitrary"),
        name="adaln_mod",
    )(cp, w_ada, b_ada.reshape(depth, 1, n))
    return out[:, :b].reshape(depth, b, 1, n)


def _modulated_norm(xf, g, mod, d):
    ms = jnp.mean(xf * xf, axis=-1, keepdims=True)
    y = xf * lax.rsqrt(ms + RMS_EPS) * g
    return y * (1.0 + mod[:, d:2 * d]) + mod[:, 0:d]


def _proj_kernel(x_ref, mod_ref, g_ref, w_ref, kconst_ref, *out_refs, d):
    h = _modulated_norm(x_ref[0], g_ref[...], mod_ref[0], d).astype(BF16)
    ksel_ref, kwin_ref = out_refs[len(PROJ_OUTS):]
    for (name, start, width, dtype), o_ref in zip(PROJ_OUTS, out_refs):
        for c0 in range(0, width, 256):
            cw = min(256, width - c0)
            acc = _dot(h, w_ref[:, start + c0:start + c0 + cw])
            o_ref[0, :, c0:c0 + cw] = acc.astype(dtype)
            if name == "akv":
                ksel_ref[0] = kconst_ref[...]
                ksel_ref[0, :, 0:HEAD_DIM] = acc[:, 0:HEAD_DIM].astype(BF16)
                kwin_ref[0] = kconst_ref[:, 0:128]
                kwin_ref[0, :, 0:HEAD_DIM] = acc[:, 128:128 + HEAD_DIM].astype(BF16)


def _project(x, mod, g, w_p, kconst, tm):
    b, s, d = x.shape
    widths = [(wd, dt) for (_, _, wd, dt) in PROJ_OUTS] + [(256, BF16), (128, BF16)]
    out_shapes = [jax.ShapeDtypeStruct((b, s, wd), dt) for wd, dt in widths]
    out_specs = [pl.BlockSpec((1, tm, wd), lambda i, j: (i, j, 0)) for wd, _ in widths]
    outs = pl.pallas_call(
        functools.partial(_proj_kernel, d=d),
        grid=(b, s // tm),
        in_specs=[
            pl.BlockSpec((1, tm, d), lambda i, j: (i, j, 0)),
            pl.BlockSpec((1, 1, 3 * d), lambda i, j: (i, 0, 0)),
            pl.BlockSpec((1, d), lambda i, j: (0, 0)),
            pl.BlockSpec((d, PROJ_W), lambda i, j: (0, 0)),
            pl.BlockSpec((tm, 256), lambda i, j: (j, 0)),
        ],
        out_specs=out_specs,
        out_shape=out_shapes,
        compiler_params=_params("arbitrary", "arbitrary"),
        name="in_proj",
    )(x, mod, g.reshape(1, d), w_p, kconst)
    names = [name for (name, _, _, _) in PROJ_OUTS] + ["ksel", "kwin"]
    return dict(zip(names, outs))


def _compress_kernel(acmp_ref, pos_ref, w1_ref, w2_ref, cfeat_ref, kc_ref, vc_ref, *, nc):
    half = nc // 2
    for parity in range(2):
        hid = jnp.zeros((half, 128), F32)
        for i in range(CMP_BLOCK):
            z = acmp_ref[0, pl.ds(parity * CMP_BLOCK + i, half, stride=2 * CMP_BLOCK), :]
            hid = hid + _dot((z + pos_ref[i:i + 1, :]).astype(BF16), w1_ref[i])
        act = _silu(hid).astype(BF16)
        out_rows = slice(parity * half, (parity + 1) * half)
        kc_ref[0, out_rows, :] = (_dot(act, w2_ref[0]) + cfeat_ref[out_rows, :]).astype(BF16)
        vc_ref[0, out_rows, :] = _dot(act, w2_ref[1]).astype(BF16)


def _compress(acmp, pos, w1bd, w2bd, cfeat):
    b, s, _ = acmp.shape
    nc = s // CMP_BLOCK
    spec_o = pl.BlockSpec((1, nc, 128), lambda i: (i, 0, 0))
    return pl.pallas_call(
        functools.partial(_compress_kernel, nc=nc),
        grid=(b,),
        in_specs=[
            pl.BlockSpec((1, s, 128), lambda i: (i, 0, 0)),
            pl.BlockSpec((CMP_BLOCK, 128), lambda i: (0, 0)),
            pl.BlockSpec((CMP_BLOCK, 128, 128), lambda i: (0, 0, 0)),
            pl.BlockSpec((2, 128, 128), lambda i: (0, 0, 0)),
            pl.BlockSpec((nc, 128), lambda i: (0, 0)),
        ],
        out_specs=[spec_o, spec_o],
        out_shape=[jax.ShapeDtypeStruct((b, nc, 128), BF16)] * 2,
        compiler_params=_params("arbitrary"),
        name="nsa_compress",
    )(acmp, pos, w1bd, w2bd, cfeat)


def _compress_features(s):
    nc = s // CMP_BLOCK
    r = np.arange(nc)
    blk = np.where(r < nc // 2, 2 * r, 2 * (r - nc // 2) + 1)
    end = blk * CMP_BLOCK + CMP_BLOCK - 1
    feat = np.zeros((nc, 128), np.float32)
    feat[:, HEAD_DIM] = SEL_BLOCK * (end // SEL_BLOCK)
    feat[:, HEAD_DIM + 1] = end % SEL_BLOCK
    return jnp.asarray(feat)


def _compress_weights(cmp_pos, cmp_w1, cmp_w2):
    depth = cmp_pos.shape[0]
    hd = HEAD_DIM
    pos = jnp.concatenate([cmp_pos[:, 0], cmp_pos[:, 1]], axis=-1)
    w1 = cmp_w1.reshape(depth, 2, CMP_BLOCK, hd, hd)
    z = jnp.zeros((depth, CMP_BLOCK, hd, hd), F32)
    w1bd = jnp.concatenate([jnp.concatenate([w1[:, 0], z], axis=-1),
                            jnp.concatenate([z, w1[:, 1]], axis=-1)], axis=-2).astype(BF16)
    z2 = jnp.zeros((depth, hd, hd), F32)
    w2k = jnp.concatenate([jnp.concatenate([cmp_w2[:, 0], z2], axis=-1),
                           jnp.concatenate([z2, z2], axis=-1)], axis=-2)
    w2v = jnp.concatenate([jnp.concatenate([z2, z2], axis=-1),
                           jnp.concatenate([z2, cmp_w2[:, 1]], axis=-1)], axis=-2)
    return pos, w1bd, jnp.stack([w2k, w2v], axis=1).astype(BF16)


def _nsa_kernel(aq_ref, ag_ref, kc_ref, vc_ref, ksel_ref, vsel_ref, kwin_ref, vwin_ref, wmask_ref, gsel_ref, rowc_ref, ya_ref,
                qaug_ref, mx_ref, sm_ref, acc_ref, owin_ref, kmax_ref, *, tq, seq):
    qi = pl.program_id(1)
    qstart = qi * tq
    nc = seq // CMP_BLOCK
    half = nc // 2
    ns = seq // SEL_BLOCK
    t_lane = lax.broadcasted_iota(jnp.int32, (1, tq), 1) + qstart
    lane128 = lax.broadcasted_iota(jnp.int32, (1, 128), 1)
    nsub = tq // NSA_SUB
    row0 = lambda u, h: (u * N_HEADS + h) * NSA_SUB
    rows = N_HEADS * tq

    @pl.when(qi == 0)
    def _():
        def chunk(c, best):
            at = pl.ds(pl.multiple_of(c * NSA_TK, NSA_TK), NSA_TK)
            for k_ref in (ksel_ref, kwin_ref):
                kk = k_ref[0, at, 0:128].astype(F32)
                sq = jnp.sum(jnp.where(lane128 < HEAD_DIM, kk * kk, 0.0), axis=-1, keepdims=True)
                best = jnp.maximum(best, jnp.max(sq, axis=0, keepdims=True))
            return best
        best = lax.fori_loop(0, seq // NSA_TK, chunk, jnp.zeros((1, 1), F32))
        kmax_ref[0] = jnp.sqrt(best)[0, 0]

    row_c = lax.broadcasted_iota(jnp.int32, (nc, 1), 0)
    blk_c = jnp.where(row_c < half, 2 * row_c, 2 * (row_c - half) + 1)
    mask_c = t_lane >= blk_c * CMP_BLOCK + (CMP_BLOCK - 1)
    kc = kc_ref[0]
    vc = vc_ref[0]
    psum = jnp.zeros((nc, tq), F32)
    ocmp = []
    for h in range(N_HEADS):
        grp = aq_ref[0, :, 128 * (h // 2):128 * (h // 2) + 128].astype(F32)
        if h % 2 == 1:
            grp = pltpu.roll(grp, 64, 1)
        left = jnp.where(lane128 < HEAD_DIM, grp * QK_SCALE,
                         jnp.where(lane128 < HEAD_DIM + 2, SLOPES[h], 0.0)).astype(BF16)
        for u in range(nsub):
            qaug_ref[row0(u, h):row0(u, h) + NSA_SUB, 0:128] = left[u * NSA_SUB:(u + 1) * NSA_SUB]
        s = _dot_t(kc, left)
        s = jnp.where(mask_c, s, NEG_INF)
        m = jnp.max(s, axis=0, keepdims=True)
        e = jnp.where(mask_c, jnp.exp(s - m), 0.0)
        den = jnp.sum(e, axis=0, keepdims=True)
        p = e * (1.0 / jnp.maximum(den, TINY))
        psum = psum + p
        ocmp.append(lax.dot_general(p.astype(BF16), vc, (((0,), (0,)), ((), ())),
                                    preferred_element_type=F32))

    everything = slice(0, rows)

    def fold(sc, at=everything):
        width = sc.shape[1]
        if width < 128:
            mx_ref[at, 0:width] = jnp.maximum(mx_ref[at, 0:width], sc)
            return
        m = mx_ref[at, :]
        for g in range(width // 128):
            m = jnp.maximum(m, sc[:, 128 * g:128 * (g + 1)])
        mx_ref[at, :] = m

    def accumulate(sc, v, at=everything):
        width = sc.shape[1]
        if width < 128:
            e = jnp.exp(sc - mx_ref[at, 0:width])
            sm_ref[at, 0:width] += e
        else:
            m = mx_ref[at, :]
            e = jnp.exp(sc - jnp.concatenate([m] * (width // 128), axis=1))
            part = sm_ref[at, :]
            for g in range(width // 128):
                part = part + e[:, 128 * g:128 * (g + 1)]
            sm_ref[at, :] = part
        acc_ref[at, :] += _dot(e.astype(BF16), v)

    own0 = pl.multiple_of(qstart, tq)

    def own_keys(use):
        for u in range(nsub):
            at = slice(row0(u, 0), row0(u, 0) + N_HEADS * NSA_SUB)
            n = (u + 1) * NSA_SUB
            sc = _dot_t(qaug_ref[at, 0:128], ksel_ref[0, pl.ds(own0, n), 0:128])
            last = sc[:, u * NSA_SUB:] + wmask_ref[1]
            sc = jnp.concatenate([sc[:, 0:u * NSA_SUB], last], axis=1) if u else last
            use(sc, vsel_ref[0, pl.ds(own0, n), :], at)

    def reset_sums():
        sm_ref[...] = jnp.zeros(sm_ref.shape, F32)
        acc_ref[...] = jnp.zeros(acc_ref.shape, F32)

    qpart = qaug_ref[:, 0:128].astype(F32)
    qnorm = jnp.sqrt(jnp.sum(jnp.where(lane128 < HEAD_DIM, qpart * qpart, 0.0), axis=-1, keepdims=True))
    reach = qnorm * (kmax_ref[0] * BOUND_SLACK)
    bound_ok = jnp.max(reach, axis=0, keepdims=True)[0, 0] * 2.0 <= MAX_BOUND_GAP
    mx_ref[...] = reach + (rowc_ref[1] + rowc_ref[0] * qstart.astype(F32))
    reset_sums()
    own_keys(accumulate)

    n_win = NSA_WINDOW + NSA_SUB
    full_chunks = NSA_WINDOW // NSA_SUB

    def window_branch(exact_max):
        for u in range(nsub):
            q0 = qstart + u * NSA_SUB
            w0 = pl.multiple_of(jnp.clip(q0 - NSA_WINDOW, 0, seq - n_win), NSA_SUB)
            sub_rows = slice(row0(u, 0), row0(u, 0) + N_HEADS * NSA_SUB)
            sw = _dot_t(qaug_ref[sub_rows, 0:128], kwin_ref[0, pl.ds(w0, n_win), :])
            diag_chunk = (q0 - w0) >> (NSA_SUB.bit_length() - 1)
            pieces = []
            for c in range(n_win // NSA_SUB):
                d = diag_chunk - c
                pattern = jnp.where(d == 0, 1, jnp.where((d > 0) & (d < full_chunks), 2,
                                                         jnp.where(d == full_chunks, 3, 0)))
                pieces.append(sw[:, c * NSA_SUB:(c + 1) * NSA_SUB] + wmask_ref[pattern])
            sw = jnp.concatenate(pieces, axis=1)
            if exact_max:
                ew = jnp.exp(sw - jnp.max(sw, axis=-1, keepdims=True))
            else:
                ew = jnp.exp(sw - jnp.concatenate([mx_ref[sub_rows, :]] * (n_win // 128), axis=1))
            owin_ref[sub_rows, :] = (_dot(ew.astype(BF16), vwin_ref[0, pl.ds(w0, n_win), :])
                                     * (1.0 / jnp.sum(ew, axis=-1, keepdims=True)))

    window_branch(exact_max=False)

    imp = psum[:half] + psum[half:]
    blk = lax.broadcasted_iota(jnp.int32, (ns, 1), 0)
    cur = t_lane >> 6
    future = blk * SEL_BLOCK > t_lane
    forced = (blk == 0) | (blk == cur) | (blk == cur - 1)
    score = jnp.where(forced, FORCED_SCORE, jnp.where(future, -1.0, imp))
    blk_f = blk.astype(F32)
    for _ in range(min(SEL_TOPK, ns)):
        mx = jnp.max(score, axis=0, keepdims=True)
        first = jnp.min(jnp.where(score == mx, blk_f, float(ns)), axis=0, keepdims=True)
        score = jnp.where(blk_f == first, -jnp.inf, score)
    sel_f = jnp.transpose(jnp.where(score == -jnp.inf, 1.0, 0.0))
    negmask = jnp.where(sel_f > 0.5, 0.0, NEG_INF).astype(BF16)
    if ns < 128:
        negmask = jnp.concatenate([negmask, jnp.zeros((tq, 128 - ns), BF16)], axis=1)
    for u in range(nsub):
        for h in range(N_HEADS):
            qaug_ref[row0(u, h):row0(u, h) + NSA_SUB, 128:256] = negmask[u * NSA_SUB:(u + 1) * NSA_SUB]

    if ns < 128:
        sel_f = jnp.concatenate([sel_f, jnp.zeros((tq, 128 - ns), F32)], axis=1)
    any_blk = jnp.max(sel_f, axis=0, keepdims=True)
    blocks_per_tile = NSA_TK // SEL_BLOCK
    beyond_first = jnp.max(jnp.where((lane128 >= 1) & (lane128 < blocks_per_tile), any_blk, 0.0),
                           axis=1, keepdims=True)[0, 0] > 0.5
    any_blk = jnp.where(lane128 == 0, 0.0, any_blk)
    shift = 1
    while shift < blocks_per_tile:
        any_blk = jnp.maximum(any_blk, pltpu.roll(any_blk, 128 - shift, 1))
        shift *= 2
    tile_of_lane = lane128 >> (blocks_per_tile.bit_length() - 1)
    first_of_tile = (lane128 & (blocks_per_tile - 1)) == 0
    pow2 = lax.bitcast_convert_type(((tile_of_lane & 15) + 127) << 23, F32)
    weighted = jnp.where(first_of_tile, any_blk * pow2, 0.0)
    bits_lo = jnp.sum(jnp.where(tile_of_lane < 16, weighted, 0.0), axis=1, keepdims=True).astype(jnp.int32)[0, 0]
    bits_hi = jnp.sum(jnp.where(tile_of_lane >= 16, weighted, 0.0), axis=1, keepdims=True).astype(jnp.int32)[0, 0]

    def tile_selected(t):
        return ((jnp.where(t < 16, bits_lo, bits_hi) >> (t & 15)) & 1) == 1

    pieces_per_tile = NSA_TK // tq
    t_own = qi // pieces_per_tile
    n_pieces = qi - t_own * pieces_per_tile

    def sweep(use):
        def piece(p, carry):
            k0 = pl.multiple_of(t_own * NSA_TK + p * tq, tq)
            use(_dot_t(qaug_ref[...], ksel_ref[0, pl.ds(k0, tq), :]), k0, tq)
            return carry

        def tile(t, carry):
            def visit():
                k0 = pl.multiple_of(t * NSA_TK, NSA_TK)
                use(_dot_t(qaug_ref[...], ksel_ref[0, pl.ds(k0, NSA_TK), :]), k0, NSA_TK)
            pl.when(tile_selected(t))(visit)
            return carry

        def first_block():
            use(_dot_t(qaug_ref[:, 0:128], ksel_ref[0, 0:SEL_BLOCK, 0:128]), 0, SEL_BLOCK)

        lax.fori_loop(0, n_pieces, piece, 0)
        lax.fori_loop(0, t_own, tile, 0)
        pl.when((t_own >= 1) & jnp.logical_not(beyond_first))(first_block)

    @pl.when(jnp.logical_not(bound_ok))
    def _():
        window_branch(exact_max=True)
        mx_ref[...] = jnp.full(mx_ref.shape, NEG_INF, F32)
        sweep(lambda sc, k0, n: fold(sc))
        own_keys(lambda sc, v, at: fold(sc, at))
        mx_ref[...] = jnp.broadcast_to(jnp.max(mx_ref[...], axis=-1, keepdims=True), mx_ref.shape)
        reset_sums()
        own_keys(accumulate)

    sweep(lambda sc, k0, n: accumulate(sc, vsel_ref[0, pl.ds(k0, n), :]))
    osel = acc_ref[...] * (1.0 / jnp.sum(sm_ref[...], axis=-1, keepdims=True))

    gate = _sigmoid(ag_ref[0])
    gate_hi = gate.astype(BF16)
    gate_lo = (gate - gate_hi.astype(F32)).astype(BF16)
    for u in range(nsub):
        q_rows = slice(u * NSA_SUB, (u + 1) * NSA_SUB)
        spread = _dot(jnp.concatenate([gate_hi[q_rows, :], gate_lo[q_rows, :]], axis=1), gsel_ref[...])
        comb = []
        for h in range(N_HEADS):
            head_rows = slice(row0(u, h), row0(u, h) + NSA_SUB)
            g_cmp, g_sel, g_win = (spread[:, 128 * (3 * h + br):128 * (3 * h + br + 1)] for br in range(3))
            comb.append(g_cmp * ocmp[h][q_rows, :] + g_sel * osel[head_rows, :]
                        + g_win * owin_ref[head_rows, :])
        for g in range(2):
            ya_ref[0, q_rows, 128 * g:128 * (g + 1)] = jnp.where(
                lane128 < HEAD_DIM, pltpu.roll(comb[2 * g], 64, 1), comb[2 * g + 1])


def _nsa(aq, ag, kc, vc, ksel, akv, kwin, tq):
    b, s, _ = aq.shape
    nc = kc.shape[1]
    rows = N_HEADS * tq
    whole = lambda w, blk: pl.BlockSpec((1, s, w), lambda i, j, blk=blk: (i, 0, blk))
    return pl.pallas_call(
        functools.partial(_nsa_kernel, tq=tq, seq=s),
        grid=(b, s // tq),
        in_specs=[
            pl.BlockSpec((1, tq, 256), lambda i, j: (i, j, 0)),
            pl.BlockSpec((1, tq, 128), lambda i, j: (i, j, 0)),
            pl.BlockSpec((1, nc, 128), lambda i, j: (i, 0, 0)),
            pl.BlockSpec((1, nc, 128), lambda i, j: (i, 0, 0)),
            whole(256, 0),
            whole(128, 0),
            whole(128, 0),
            whole(128, 1),
            pl.BlockSpec((4, N_HEADS * NSA_SUB, NSA_SUB), lambda i, j: (0, 0, 0)),
            pl.BlockSpec((256, 3 * N_HEADS * 128), lambda i, j: (0, 0)),
            pl.BlockSpec((2, rows, 128), lambda i, j: (0, 0, 0)),
        ],
        out_specs=pl.BlockSpec((1, tq, 256), lambda i, j: (i, j, 0)),
        out_shape=jax.ShapeDtypeStruct((b, s, 256), F32),
        scratch_shapes=[
            pltpu.VMEM((rows, 256), BF16),
            pltpu.VMEM((rows, 128), F32),
            pltpu.VMEM((rows, 128), F32),
            pltpu.VMEM((rows, 128), F32),
            pltpu.VMEM((rows, 128), F32),
            pltpu.SMEM((1,), F32),
        ],
        compiler_params=_params("arbitrary", "arbitrary"),
        name="nsa_attention",
    )(aq, ag, kc, vc, ksel, akv, kwin, akv, _window_mask_table(), _gate_selector(), _nsa_row_tables(tq))


def _nsa_row_tables(tq):
    row = np.arange(N_HEADS * tq)
    u, h, r = row // (N_HEADS * NSA_SUB), (row // NSA_SUB) % N_HEADS, row % NSA_SUB
    slope = np.asarray(SLOPES, np.float32)[h]
    tab = np.stack([slope, slope * (u * NSA_SUB + r).astype(np.float32)])
    return jnp.asarray(np.repeat(tab[:, :, None], 128, axis=2), F32)


def _gate_selector():
    k = np.arange(256)[:, None] % 128
    col = np.arange(3 * N_HEADS * 128)[None, :] // 128
    return jnp.asarray(k == col, BF16)


def _window_mask_table():
    r = np.arange(N_HEADS * NSA_SUB)[:, None] % NSA_SUB
    j = np.arange(NSA_SUB)[None, :]
    keep = np.stack([np.zeros_like(j <= r), j <= r, np.ones_like(j <= r), j > r])
    return jnp.asarray(np.where(keep, 0.0, NEG_INF), F32)


def _swa_kernel(sink_ref, bq_ref, bkv_ref, bias_ref, yb_ref, *, tq, seq):
    qi = pl.program_id(1)
    sub = SWA_WINDOW
    nk = 2 * sub
    lane128 = lax.broadcasted_iota(jnp.int32, (1, 128), 1)
    for u in range(tq // sub):
        qstart = qi * tq + u * sub
        k0 = pl.multiple_of(jnp.maximum(qstart - sub, 0), sub)
        at_start = jnp.where(qstart == 0, 1, 0)
        kk = bkv_ref[0, pl.ds(k0, nk), 0:128]
        vv = bkv_ref[0, pl.ds(k0, nk), 128:256]
        for g in range(2):
            qg = bq_ref[0, u * sub:(u + 1) * sub, 128 * g:128 * (g + 1)].astype(F32) * QK_SCALE
            outs = []
            for p in range(2):
                h = SWA_HEAD_ORDER[2 * g + p]
                in_half = (lane128 >= HEAD_DIM * p) & (lane128 < HEAD_DIM * (p + 1))
                qh = jnp.where(in_half, qg, 0.0).astype(BF16)
                s = _dot_t(kk, qh) + bias_ref[h, at_start]
                sink = sink_ref[h]
                m = jnp.maximum(jnp.max(s, axis=0, keepdims=True), sink)
                e = jnp.exp(s - m)
                den = jnp.sum(e, axis=0, keepdims=True) + jnp.exp(sink - m)
                pr = e * (1.0 / jnp.maximum(den, TINY))
                outs.append(lax.dot_general(pr.astype(BF16), vv, (((0,), (0,)), ((), ())),
                                            preferred_element_type=F32))
            yb_ref[0, u * sub:(u + 1) * sub, 128 * g:128 * (g + 1)] = jnp.where(
                lane128 < HEAD_DIM, outs[0], outs[1])


def _swa_bias_table():
    sub = SWA_WINDOW
    j = np.arange(2 * sub)[:, None]
    r = np.arange(sub)[None, :]
    rel = np.stack([r + sub - j, r - j])
    ok = (rel >= 0) & (rel < sub)
    slopes = np.asarray(SLOPES, np.float32)[:, None, None, None]
    return jnp.asarray(np.where(ok[None], -slopes * rel[None].astype(np.float32), NEG_INF), F32)


def _swa(sink, bq, bkv, tq):
    b, s, _ = bq.shape
    return pl.pallas_call(
        functools.partial(_swa_kernel, tq=tq, seq=s),
        grid=(b, s // tq),
        in_specs=[
            pl.BlockSpec(memory_space=pltpu.SMEM),
            pl.BlockSpec((1, tq, 256), lambda i, j: (i, j, 0)),
            pl.BlockSpec((1, s, 256), lambda i, j: (i, 0, 0)),
            pl.BlockSpec((N_HEADS, 2, 2 * SWA_WINDOW, SWA_WINDOW), lambda i, j: (0, 0, 0, 0)),
        ],
        out_specs=pl.BlockSpec((1, tq, 256), lambda i, j: (i, j, 0)),
        out_shape=jax.ShapeDtypeStruct((b, s, 256), F32),
        compiler_params=_params("arbitrary", "arbitrary"),
        name="swa_attention",
    )(sink, bq, bkv, _swa_bias_table())


def _stick_kernel(cq_ref, ck_ref, cv_ref, yc_ref, qm_ref, carry_ref, o_ref, *, tq, nsub, seq):
    lane256 = lax.broadcasted_iota(jnp.int32, (1, 256), 1)
    head_of_lane = lane256 >> 6
    rows = N_HEADS * tq
    jj = lax.broadcasted_iota(jnp.int32, (tq, 2 * tq), 0)
    ss = lax.broadcasted_iota(jnp.int32, (tq, 2 * tq), 1)
    upper = ((jj > ss) | (ss >= tq)).astype(BF16)
    upper2 = jnp.concatenate([upper, upper], axis=0)
    t_in = lax.broadcasted_iota(jnp.int32, (rows, tq), 0) & (tq - 1)
    strict = lax.broadcasted_iota(jnp.int32, (rows, tq), 1) < t_in

    def tile(u, k0, carry, acc, valid):
        diagonal = carry is None
        kt = ck_ref[0, pl.ds(k0, tq), :]
        vt = cv_ref[0, pl.ds(k0, tq), :]
        z = _dot_t(qm_ref[u], kt)
        soft = jnp.log(1.0 + jnp.exp(-jnp.abs(z)))
        log_beta = jnp.minimum(z, 0.0) - soft
        log_1m = log_beta - z
        if diagonal:
            log_1m = jnp.where(strict, log_1m, 0.0)
        hi = log_1m.astype(BF16)
        lo = (log_1m - hi.astype(F32)).astype(BF16)
        sums = _dot(jnp.concatenate([hi, lo], axis=1), upper2)
        suffix = sums[:, :tq] if diagonal else sums[:, :tq] + carry
        a = jnp.exp(log_beta + suffix)
        if diagonal:
            a = jnp.where(strict, a, 0.0)
        if valid is not None:
            a = jnp.where(valid, a, 0.0)
        a_b = a.astype(BF16)
        a_cat = jnp.concatenate([a_b[h * tq:(h + 1) * tq, :] for h in range(N_HEADS)], axis=1)
        v_bd = jnp.concatenate([jnp.where(head_of_lane == h, vt, jnp.zeros_like(vt))
                                for h in range(N_HEADS)], axis=0)
        acc = acc + _dot(a_cat, v_bd)
        return (sums[:, tq:] if diagonal else carry + sums[:, tq:]), acc

    worst = []
    for u in range(nsub):
        qi = pl.program_id(1) * nsub + u
        q = cq_ref[0, u * tq:(u + 1) * tq, :].astype(F32) * QK_SCALE
        for h in range(N_HEADS):
            qm_ref[u, h * tq:(h + 1) * tq, :] = jnp.where(head_of_lane == h, q, 0.0).astype(BF16)
        carry, acc = tile(u, pl.multiple_of(qi * tq, tq), None, jnp.zeros((tq, 256), F32), None)
        for d in range(1, STICK_EAGER_TILES + 1):
            j = qi - d
            carry, acc = tile(u, pl.multiple_of(jnp.maximum(j, 0) * tq, tq), carry, acc, j >= 0)
        carry_ref[u] = carry
        o_ref[u] = acc
        worst.append(jnp.max(carry, axis=0, keepdims=True)[0, 0])

    for u in range(nsub):
        qi = pl.program_id(1) * nsub + u

        def cond(state):
            j, worst_carry = state
            return (j >= 0) & (worst_carry > STICK_EXIT)

        def body(state, u=u):
            j, _ = state
            new_carry, new_acc = tile(u, pl.multiple_of(j * tq, tq), carry_ref[u], o_ref[u], None)
            carry_ref[u] = new_carry
            o_ref[u] = new_acc
            return j - 1, jnp.max(new_carry, axis=0, keepdims=True)[0, 0]

        lax.while_loop(cond, body, (qi - 1 - STICK_EAGER_TILES, worst[u]))
        yc_ref[0, u * tq:(u + 1) * tq, :] = o_ref[u]


def _stick(cq, ck, cv, tq, nsub):
    b, s, _ = cq.shape
    step = tq * nsub
    return pl.pallas_call(
        functools.partial(_stick_kernel, tq=tq, nsub=nsub, seq=s),
        grid=(b, s // step),
        in_specs=[
            pl.BlockSpec((1, step, 256), lambda i, j: (i, j, 0)),
            pl.BlockSpec((1, s, 256), lambda i, j: (i, 0, 0)),
            pl.BlockSpec((1, s, 256), lambda i, j: (i, 0, 0)),
        ],
        out_specs=pl.BlockSpec((1, step, 256), lambda i, j: (i, j, 0)),
        out_shape=jax.ShapeDtypeStruct((b, s, 256), F32),
        scratch_shapes=[
            pltpu.VMEM((nsub, N_HEADS * tq, 256), BF16),
            pltpu.VMEM((nsub, N_HEADS * tq, tq), F32),
            pltpu.VMEM((nsub, tq, 256), F32),
        ],
        compiler_params=_params("arbitrary", "arbitrary"),
        name="stick_breaking",
    )(cq, ck, cv)


def _ret_kernel(dq_ref, dk_ref, dv_ref, dmat_ref, xi_ref, zeta_ref, decay_ref, bd_ref, yd_ref, r_ref):
    n = pl.program_id(1)

    @pl.when(n == 0)
    def _():
        r_ref[...] = jnp.zeros(r_ref.shape, F32)

    c = RET_CHUNK
    lane256 = lax.broadcasted_iota(jnp.int32, (1, 256), 1)
    head_of_lane = lane256 >> 6
    r = r_ref[...]
    for u in range(dq_ref.shape[1] // c):
        rows = slice(u * c, (u + 1) * c)
        q = dq_ref[0, rows, :].astype(F32) * QK_SCALE
        k = dk_ref[0, rows, :]
        v = dv_ref[0, rows, :]
        o = _dot(q.astype(BF16), r.astype(BF16)) * xi_ref[...]
        for h in range(N_HEADS):
            qh = jnp.where(head_of_lane == h, q, 0.0).astype(BF16)
            s = _dot_t(qh, k) * dmat_ref[h]
            o = o + jnp.where(head_of_lane == h, _dot(s.astype(BF16), v), 0.0)
        mu = jnp.zeros((c, 256), F32)
        for h in range(N_HEADS):
            in_h = head_of_lane == h
            mu = mu + jnp.where(in_h, jnp.sum(jnp.where(in_h, o, 0.0), axis=-1, keepdims=True), 0.0)
        cen = o - mu * (1.0 / HEAD_DIM)
        var = jnp.zeros((c, 256), F32)
        for h in range(N_HEADS):
            in_h = head_of_lane == h
            var = var + jnp.where(in_h, jnp.sum(jnp.where(in_h, cen * cen, 0.0), axis=-1, keepdims=True), 0.0)
        yd_ref[0, rows, :] = cen * lax.rsqrt(var * (1.0 / HEAD_DIM) + LN_EPS)
        kz = (k.astype(F32) * zeta_ref[...]).astype(BF16)
        upd = lax.dot_general(kz, v, (((0,), (0,)), ((), ())), preferred_element_type=F32)
        r = r * decay_ref[...] + upd * bd_ref[...]
    r_ref[...] = r


def _retention(dq, dk, dv, consts):
    b, s, _ = dq.shape
    c = RET_CHUNK
    dmat, xi, zeta, decay, bd = consts
    step = RET_CHUNKS_PER_STEP * c
    tile = pl.BlockSpec((1, step, 256), lambda i, j: (i, j, 0))
    return pl.pallas_call(
        _ret_kernel,
        grid=(b, s // step),
        in_specs=[
            tile, tile, tile,
            pl.BlockSpec((N_HEADS, c, c), lambda i, j: (0, 0, 0)),
            pl.BlockSpec((c, 256), lambda i, j: (0, 0)),
            pl.BlockSpec((c, 256), lambda i, j: (0, 0)),
            pl.BlockSpec((256, 256), lambda i, j: (0, 0)),
            pl.BlockSpec((256, 256), lambda i, j: (0, 0)),
        ],
        out_specs=tile,
        out_shape=jax.ShapeDtypeStruct((b, s, 256), F32),
        scratch_shapes=[pltpu.VMEM((256, 256), F32)],
        compiler_params=_params("arbitrary", "arbitrary"),
        name="retention",
    )(dq, dk, dv, dmat, xi, zeta, decay, bd)


def _retention_consts():
    c = RET_CHUNK
    log_g = jnp.log(1.0 - jnp.asarray(2.0 ** (-5.0 - np.arange(N_HEADS)), dtype=F32))
    i = jnp.arange(c)
    diff = (i[:, None] - i[None, :]).astype(F32)
    dmat = jnp.where(diff >= 0, jnp.exp(log_g[:, None, None] * jnp.maximum(diff, 0.0)), 0.0)
    zeta = jnp.exp(log_g[:, None] * (c - 1 - i)[None, :].astype(F32))
    xi = jnp.exp(log_g[:, None] * (i + 1)[None, :].astype(F32))
    g_chunk = jnp.exp(log_g * c)
    lanes = lambda hc: jnp.repeat(hc.T, HEAD_DIM, axis=1)
    head = np.arange(256) // HEAD_DIM
    bd = jnp.asarray(head[:, None] == head[None, :], F32)
    decay = jnp.repeat(g_chunk, HEAD_DIM)[:, None] * jnp.ones((1, 256), F32)
    return dmat, lanes(xi), lanes(zeta), decay, bd


def _merge_kernel(x_ref, mod_ref, g_ref, ya_ref, yb_ref, yc_ref, yd_ref, z_ref,
                  wm_ref, wbr_ref, wout_ref, fg_ref, o_ref, *, d, final):
    mod = mod_ref[0]
    tm = x_ref.shape[1]
    part = tm // MERGE_ROW_PARTS
    for r in range(MERGE_ROW_PARTS):
        rows = slice(r * part, (r + 1) * part)
        xf = x_ref[0, rows, :]
        h = _modulated_norm(xf, g_ref[...], mod, d).astype(BF16)
        merged = None
        for i, y_ref in enumerate((ya_ref, yb_ref, yc_ref, yd_ref)):
            yi = (y_ref[0, rows, :] * _silu(z_ref[0, rows, 256 * i:256 * (i + 1)])).astype(BF16)
            term = _sigmoid(_dot(h, wm_ref[i])) * _dot(yi, wbr_ref[i])
            merged = term if merged is None else merged + term
        out = xf + mod[:, 2 * d:3 * d] * _dot(merged.astype(BF16), wout_ref[...])
        if final:
            out = out * lax.rsqrt(jnp.mean(out * out, axis=-1, keepdims=True) + RMS_EPS) * fg_ref[...]
        o_ref[0, rows, :] = out


def _merge(x, mod, g, ys, z, wm, wbr, wout, final_g, final, tm):
    b, s, d = x.shape
    ytile = pl.BlockSpec((1, tm, 256), lambda i, j: (i, j, 0))
    return pl.pallas_call(
        functools.partial(_merge_kernel, d=d, final=final),
        grid=(b, s // tm),
        in_specs=[
            pl.BlockSpec((1, tm, d), lambda i, j: (i, j, 0)),
            pl.BlockSpec((1, 1, 3 * d), lambda i, j: (i, 0, 0)),
            pl.BlockSpec((1, d), lambda i, j: (0, 0)),
            ytile, ytile, ytile, ytile,
            pl.BlockSpec((1, tm, 4 * 256), lambda i, j: (i, j, 0)),
            pl.BlockSpec((4, d, d), lambda i, j: (0, 0, 0)),
            pl.BlockSpec((4, 256, d), lambda i, j: (0, 0, 0)),
            pl.BlockSpec((d, d), lambda i, j: (0, 0)),
            pl.BlockSpec((1, d), lambda i, j: (0, 0)),
        ],
        out_specs=pl.BlockSpec((1, tm, d), lambda i, j: (i, j, 0)),
        out_shape=jax.ShapeDtypeStruct((b, s, d), F32),
        compiler_params=_params("arbitrary", "arbitrary"),
        name="merge_out",
    )(x, mod, g.reshape(1, d), *ys, z, wm, wbr, wout, final_g.reshape(1, d))


def _proj_columns():
    a, bb, cc, dd = 0, 908, 1676, 2700
    perm = np.concatenate([np.arange(64) + 64 * h for h in SWA_HEAD_ORDER])
    pad = lambda n: np.full((n,), -1)
    cols = np.concatenate([
        np.arange(a, a + 256),
        np.arange(a + 256, a + 384),
        np.arange(a + 384, a + 640),
        np.arange(a + 640, a + 652), pad(116),
        np.arange(a + 652, a + 908),
        bb + 512 + perm,
        np.arange(cc + 768, cc + 1024),
        np.arange(dd + 768, dd + 1024),
        bb + perm,
        np.arange(bb + 256, bb + 512),
        np.arange(cc, cc + 768),
        np.arange(dd, dd + 768),
    ])
    assert cols.shape[0] == PROJ_W
    return cols, perm


def _take_columns(w, cols):
    pieces, i = [], 0
    while i < len(cols):
        j = i + 1
        if cols[i] < 0:
            while j < len(cols) and cols[j] < 0:
                j += 1
            pieces.append(jnp.zeros(w.shape[:-1] + (j - i,), w.dtype))
        else:
            while j < len(cols) and cols[j] == cols[j - 1] + 1:
                j += 1
            pieces.append(w[..., int(cols[i]):int(cols[j - 1]) + 1])
        i = j
    return jnp.concatenate(pieces, axis=-1)


def _position_features(s):
    pos = np.arange(s)
    feat = np.zeros((s, 64), np.float32)
    feat[:, 0] = SEL_BLOCK * (pos // SEL_BLOCK)
    feat[:, 1] = pos % SEL_BLOCK
    onehot = (pos[:, None] // SEL_BLOCK == np.arange(128)[None, :]).astype(np.float32)
    return jnp.asarray(feat, BF16), jnp.asarray(onehot, BF16)


def kernel(x, c, w_ada, b_ada, norm_g, w_in, cmp_pos, cmp_w1, cmp_w2, sink, w_merge, w_br, w_out, final_g):
    b, s, d = x.shape
    depth = w_ada.shape[0]
    tm = min(ROW_TILE, s)

    cols, perm = _proj_columns()
    w_in_p = _take_columns(w_in.astype(BF16), cols)
    wm_b = w_merge.astype(BF16)
    wbr_b = jnp.stack([w_br[:, 0], w_br[:, 1][:, perm, :], w_br[:, 2], w_br[:, 3]], axis=1).astype(BF16)
    wout_b = w_out.astype(BF16)
    pos_l, w1bd, w2bd = _compress_weights(cmp_pos, cmp_w1, cmp_w2)
    feat, onehot = _position_features(s)
    kconst = jnp.concatenate([jnp.zeros((s, HEAD_DIM), BF16), feat, onehot], axis=-1)
    cfeat = _compress_features(s)
    ret_consts = _retention_consts()
    mods = _modulation(c, w_ada, b_ada)

    for l in range(depth):
        p = _project(x, mods[l], norm_g[l], w_in_p[l], kconst, tm)
        kc, vc = _compress(p["acmp"], pos_l[l], w1bd[l], w2bd[l], cfeat)
        y_a = _nsa(p["aq"], p["ag"], kc, vc, p["ksel"], p["akv"], p["kwin"], NSA_TQ)
        y_b = _swa(sink[l], p["bq"], p["bkv"], SWA_TQ)
        y_c = _stick(p["cq"], p["ck"], p["cv"], STICK_TQ, STICK_SUBTILES)
        y_d = _retention(p["dq"], p["dk"], p["dv"], ret_consts)
        x = _merge(x, mods[l], norm_g[l], (y_a, y_b, y_c, y_d), p["z"], wm_b[l], wbr_b[l], wout_b[l],
                   final_g, l == depth - 1, tm)
    return x
```
